```python
import math
import jax
import jax.numpy as jnp
from jax import lax
import numpy as np

D_MODEL = 1024
BATCH = 32
SEQ = 2048
DEPTH = 1
DEC_BATCH = 128
DEC_SEQ = 4
PAST_LEN = 8192
PAGE_SIZE = 128

D_MIX = D_MODEL
HG_WIDTH = D_MIX // 2
HG_HEADS = 4
HG_DK = HG_WIDTH // HG_HEADS
HG_DV = HG_DK
HG_CHUNK = 64
NSA_WIDTH = D_MIX - HG_WIDTH
NSA_HEADS = 8
NSA_DK = NSA_WIDTH // NSA_HEADS
NSA_KV_HEADS = 2
NSA_GROUP = NSA_HEADS // NSA_KV_HEADS
CMP_STRIDE = 16
CMP_BLOCK = 32
CMP_PARTS = CMP_BLOCK // CMP_STRIDE
CMP_HIDDEN = 4 * NSA_DK
SLC_BLOCK = 64
SLC_TOPK = 16
WINDOW = 512
WIN_QBLOCK = 128
SLC_QBLOCK = 16
N_KV_SLOTS = 4
D_FF = 2816
CONV_W = 3
EPS = 1e-6
COL_KV = N_KV_SLOTS * NSA_KV_HEADS * NSA_DK
COL_WIN = 2 * NSA_KV_HEADS * NSA_DK
COL_GATE = 3 * NSA_HEADS
SPLITS = (HG_WIDTH, HG_WIDTH, HG_WIDTH, HG_WIDTH, NSA_WIDTH, COL_KV, COL_WIN, COL_GATE)
D_IN = 4 * HG_WIDTH + NSA_WIDTH + COL_KV + COL_WIN + COL_GATE

kernel_name = "hgrn2_nsa_convffn_hybrid_step"


def rmsnorm(x, g):
    xf = x.astype(jnp.float32)
    xf = xf * lax.rsqrt(jnp.mean(xf * xf, axis=-1, keepdims=True) + EPS)
    return xf.astype(x.dtype) * g


def masked_softmax(s, mask):
    s = jnp.where(mask, s.astype(jnp.float32), -jnp.inf)
    m = jnp.max(s, axis=-1, keepdims=True)
    m = jnp.where(jnp.isfinite(m), m, 0.0)
    e = jnp.where(mask, jnp.exp(s - m), 0.0)
    d = jnp.sum(e, axis=-1, keepdims=True)
    return e / jnp.where(d > 0, d, 1.0)


def hgrn2_scan(q, logf, k, v, s0):
    B, T, H, DK = q.shape
    DV = v.shape[-1]
    C = min(HG_CHUNK, T)
    pad = (-T) % C
    if pad:
        pw = ((0, 0), (0, pad), (0, 0), (0, 0))
        q, logf, k, v = (jnp.pad(a, pw) for a in (q, logf, k, v))
    n = (T + pad) // C

    def chunks(a):
        return a.reshape(B, n, C, H, a.shape[-1]).transpose(1, 0, 3, 2, 4)

    causal = jnp.tril(jnp.ones((C, C), dtype=bool))[:, :, None]

    def step(s, xs):
        qb, fb, kb, vb = xs
        a = jnp.cumsum(fb.astype(jnp.float32), axis=2)
        decay = jnp.exp(jnp.where(causal, a[:, :, :, None, :] - a[:, :, None, :, :], -jnp.inf))
        att = jnp.einsum('bhtk,bhsk,bhtsk->bhts', qb, kb, decay)
        o = jnp.einsum('bhts,bhsv->bhtv', att, vb) + jnp.einsum('bhtk,bhkv->bhtv', qb * jnp.exp(a), s)
        a_end = a[:, :, -1:, :]
        s = jnp.exp(a_end[:, :, 0, :, None]) * s + jnp.einsum('bhsk,bhsv->bhkv', kb * jnp.exp(a_end - a), vb)
        return s, o

    s_fin, o = lax.scan(step, s0.astype(jnp.float32), tuple(chunks(a) for a in (q, logf, k, v)))
    o = o.transpose(1, 0, 3, 2, 4).reshape(B, n * C, H, DV)[:, :T]
    return o, s_fin


def hgrn2_mixer(q_raw, f_raw, i_raw, g_raw, lb, norm_w, s0):
    B, T, _ = q_raw.shape
    fr = f_raw.astype(jnp.float32)
    lb = lb.astype(jnp.float32)
    f = lb + (1.0 - lb) * jax.nn.sigmoid(fr)
    k = (1.0 - lb) * jax.nn.sigmoid(-fr)
    q = jax.nn.silu(q_raw)
    heads = lambda a: a.reshape(B, T, HG_HEADS, a.shape[-1] // HG_HEADS)
    o, s_fin = hgrn2_scan(heads(q), heads(jnp.log(f)), heads(k), heads(i_raw), s0)
    o = rmsnorm(o, norm_w.reshape(HG_HEADS, HG_DV)).reshape(B, T, HG_WIDTH) * jax.nn.silu(g_raw)
    return o, s_fin.astype(s0.dtype)


def compress_rows(rows, pe, w1, w2):
    B, S, G, D = rows.shape
    n_chunks = S // CMP_STRIDE
    n_blk = n_chunks - CMP_PARTS + 1
    ch = rows[:, :n_chunks * CMP_STRIDE].reshape(B, n_chunks, CMP_STRIDE, G, D)
    proj = jnp.einsum('bnsgd,psdh->pbngh', ch, w1)
    bias = jnp.einsum('psd,psdh->ph', pe, w1)
    h = proj[0, :, 0:n_blk] + bias[0]
    for p in range(1, CMP_PARTS):
        h = h + proj[p, :, p:p + n_blk] + bias[p]
    return jnp.einsum('bngh,hd->bngd', jax.nn.silu(h), w2)


def cmp_attend(q, kc, vc, qpos):
    N = kc.shape[1]
    blk_end = jnp.arange(N) * CMP_STRIDE + CMP_BLOCK - 1
    mask = blk_end[None, :] <= qpos[:, None]
    s = jnp.einsum('btgrd,bngd->bgrtn', q, kc) * (NSA_DK ** -0.5)
    p = masked_softmax(s, mask[None, None, None])
    return jnp.einsum('bgrtn,bngd->btgrd', p, vc), p


def select_blocks(p_cmp, qpos, n_slc):
    N = p_cmp.shape[-1]
    i = jnp.arange(N)[:, None]
    j = jnp.arange(n_slc)[None, :]
    lo = jnp.maximum(i * CMP_STRIDE, j * SLC_BLOCK)
    hi = jnp.minimum(i * CMP_STRIDE + CMP_BLOCK, (j + 1) * SLC_BLOCK)
    overlap = jnp.maximum(hi - lo, 0).astype(jnp.float32) / CMP_STRIDE
    imp = jnp.einsum('bgrtn,nj->btgj', p_cmp, overlap)
    cur = qpos // SLC_BLOCK
    jj = jnp.arange(n_slc)[None, :]
    valid = jj <= cur[:, None]
    forced = (jj == 0) | (jj == cur[:, None]) | (jj == cur[:, None] - 1)
    score = jnp.where(forced[None, :, None, :], jnp.inf, jnp.where(valid[None, :, None, :], imp, -jnp.inf))
    vals, idx = lax.top_k(score, min(SLC_TOPK, n_slc))
    return idx, vals > -jnp.inf


def slc_attend(q, ks, vs, idx, sel_ok, qpos):
    B, T, G, R, D = q.shape
    S = ks.shape[1]
    n_slc = -(-S // SLC_BLOCK)
    pad = n_slc * SLC_BLOCK - S
    ks_t = jnp.pad(ks, ((0, 0), (0, pad), (0, 0), (0, 0))).transpose(0, 2, 1, 3)
    vs_t = jnp.pad(vs, ((0, 0), (0, pad), (0, 0), (0, 0))).transpose(0, 2, 1, 3)
    QB = SLC_QBLOCK if T % SLC_QBLOCK == 0 else T
    nb = T // QB
    bi = jnp.arange(B)[:, None, None, None]
    gi = jnp.arange(G)[None, None, :, None]

    def block(xs):
        qb, ib, okb, pb = xs
        tok = ib[..., None] * SLC_BLOCK + jnp.arange(SLC_BLOCK)
        ok = okb[..., None] & (tok <= pb[None, :, None, None, None])
        tok = tok.reshape(B, QB, G, -1)
        ok = ok.reshape(B, QB, G, -1)
        kg = ks_t[bi, gi, tok]
        vg = vs_t[bi, gi, tok]
        s = jnp.einsum('bqgrd,bqgnd->bqgrn', qb, kg) * (NSA_DK ** -0.5)
        p = masked_softmax(s, ok[:, :, :, None, :])
        return jnp.einsum('bqgrn,bqgnd->bqgrd', p, vg)

    blk = lambda a: a.reshape((B, nb, QB) + a.shape[2:]).swapaxes(0, 1)
    o = lax.map(block, (blk(q), blk(idx), blk(sel_ok), qpos.reshape(nb, QB)))
    return o.swapaxes(0, 1).reshape(B, T, G, R, D)


def win_attend(q, kw, vw, qpos, kpos):
    B, T, G, R, D = q.shape
    Sk = kw.shape[1]
    off = Sk - T
    QB = WIN_QBLOCK if T % WIN_QBLOCK == 0 else T
    nb = T // QB
    KB = QB + WINDOW
    padw = ((0, 0), (WINDOW, 0), (0, 0), (0, 0))
    kp = jnp.pad(kw, padw)
    vp = jnp.pad(vw, padw)
    kpos_p = jnp.pad(kpos, (WINDOW, 0))
    real = jnp.arange(Sk + WINDOW) >= WINDOW
    idx = off + jnp.arange(nb)[:, None] * QB + jnp.arange(KB)[None, :]
    kb = kp[:, idx]
    vb = vp[:, idx]
    qp = qpos.reshape(nb, QB)[:, :, None]
    kpb = kpos_p[idx][:, None, :]
    mask = real[idx][:, None, :] & (kpb <= qp) & (kpb > qp - WINDOW)
    qb = q.reshape(B, nb, QB, G, R, D)
    s = jnp.einsum('bnqgrd,bnkgd->bngrqk', qb, kb) * (NSA_DK ** -0.5)
    p = masked_softmax(s, mask[None, :, None, None])
    o = jnp.einsum('bngrqk,bnkgd->bnqgrd', p, vb)
    return o.reshape(B, T, G, R, D)


def nsa_mixer(q_raw, kv_all, win_all, gate_raw, qpos, win_kpos, pe, w1, w2):
    B, T, _ = q_raw.shape
    q = q_raw.reshape(B, T, NSA_KV_HEADS, NSA_GROUP, NSA_DK)
    kc = compress_rows(kv_all[:, :, 0], pe[0], w1[0], w2[0])
    vc = compress_rows(kv_all[:, :, 1], pe[1], w1[1], w2[1])
    o_cmp, p_cmp = cmp_attend(q, kc, vc, qpos)
    n_slc = -(-kv_all.shape[1] // SLC_BLOCK)
    idx, sel_ok = select_blocks(p_cmp, qpos, n_slc)
    o_slc = slc_attend(q, kv_all[:, :, 2], kv_all[:, :, 3], idx, sel_ok, qpos)
    o_win = win_attend(q, win_all[:, :, 0], win_all[:, :, 1], qpos, win_kpos)
    g = jax.nn.sigmoid(gate_raw.astype(jnp.float32)).reshape(B, T, 3, NSA_KV_HEADS, NSA_GROUP)[..., None]
    o = g[:, :, 0] * o_cmp + g[:, :, 1] * o_slc + g[:, :, 2] * o_win
    return o.reshape(B, T, NSA_WIDTH)


def conv_ffn(h, conv_past, w_gate, w_val, conv_w, conv_b, w_down):
    T = h.shape[1]
    u = h @ w_gate
    u_ext = jnp.concatenate([conv_past.astype(u.dtype), u], axis=1)
    c = conv_b + u_ext[:, 0:T] * conv_w[0]
    for j in range(1, CONV_W):
        c = c + u_ext[:, j:j + T] * conv_w[j]
    y = (jax.nn.gelu(c) * (h @ w_val)) @ w_down
    return y, u_ext[:, T:]


def layer(x, kv_past, win_past, s0, conv_past, qpos, win_kpos, win_keep,
          lb, norm1, w_in, hg_norm, cmp_pe, cmp_w1, cmp_w2, w_out, norm2, w_gate, w_val, conv_w, conv_b, w_down):
    B, T, _ = x.shape
    z = rmsnorm(x, norm1) @ w_in
    q_hg, f_hg, i_hg, g_hg, q_nsa, kv_new, win_new, gate = jnp.split(z, np.cumsum(SPLITS)[:-1].tolist(), axis=-1)
    o_hg, s_fin = hgrn2_mixer(q_hg, f_hg, i_hg, g_hg, lb, hg_norm, s0)
    kv_new = kv_new.reshape(B, T, N_KV_SLOTS, NSA_KV_HEADS, NSA_DK)
    win_new = win_new.reshape(B, T, 2, NSA_KV_HEADS, NSA_DK)
    kv_all = kv_new if kv_past is None else jnp.concatenate([kv_past.astype(kv_new.dtype), kv_new], axis=1)
    win_all = win_new if win_past is None else jnp.concatenate([win_past.astype(win_new.dtype), win_new], axis=1)
    o_nsa = nsa_mixer(q_nsa, kv_all, win_all, gate, qpos, win_kpos, cmp_pe, cmp_w1, cmp_w2)
    x = x + (jnp.concatenate([o_hg, o_nsa], axis=-1).astype(x.dtype) @ w_out)
    y, conv_new = conv_ffn(rmsnorm(x, norm2), conv_past, w_gate, w_val, conv_w, conv_b, w_down)
    x = x + y.astype(x.dtype)
    return x, kv_new, win_all[:, -win_keep:], s_fin, conv_new


def setup_inputs(seed: int = 0) -> dict:
    key = jax.random.key(seed)
    ks = jax.random.split(key, 24)
    f32 = jnp.float32
    nrm = lambda k, shape, scale=1.0: scale * jax.random.normal(k, shape, f32)
    n_pages = PAST_LEN // PAGE_SIZE
    n_used = DEC_BATCH * n_pages
    n_pool = n_used + n_used // 4
    win_len = min(WINDOW, PAST_LEN)
    page_table = jax.random.permutation(ks[3], n_pool)[:n_used].reshape(DEC_BATCH, n_pages).astype(jnp.int32)
    return {
        "x_prompt": nrm(ks[0], (BATCH, SEQ, D_MODEL)),
        "x_sample": nrm(ks[1], (DEC_BATCH, DEC_SEQ, D_MODEL)),
        "cache_nsa_kv": nrm(ks[2], (DEPTH, n_pool, PAGE_SIZE, N_KV_SLOTS, NSA_KV_HEADS, NSA_DK)),
        "cache_win_kv": nrm(ks[4], (DEPTH, DEC_BATCH, win_len, 2, NSA_KV_HEADS, NSA_DK)),
        "state_hgrn": nrm(ks[5], (DEPTH, DEC_BATCH, HG_HEADS, HG_DK, HG_DV), 0.3),
        "state_ffn_conv": nrm(ks[6], (DEPTH, DEC_BATCH, CONV_W - 1, D_FF)),
        "page_table": page_table,
        "norm1": 1.0 + nrm(ks[7], (DEPTH, D_MODEL), 0.05),
        "w_in": nrm(ks[8], (DEPTH, D_MODEL, D_IN), D_MODEL ** -0.5),
        "hg_lb_logits": nrm(ks[9], (DEPTH + 1, HG_WIDTH)),
        "hg_norm": 1.0 + nrm(ks[10], (DEPTH, HG_WIDTH), 0.05),
        "cmp_pe": nrm(ks[11], (DEPTH, 2, CMP_PARTS, CMP_STRIDE, NSA_DK), 0.5),
        "cmp_w1": nrm(ks[12], (DEPTH, 2, CMP_PARTS, CMP_STRIDE, NSA_DK, CMP_HIDDEN), (CMP_BLOCK * NSA_DK) ** -0.5),
        "cmp_w2": nrm(ks[13], (DEPTH, 2, CMP_HIDDEN, NSA_DK), CMP_HIDDEN ** -0.5),
        "w_out": nrm(ks[14], (DEPTH, D_MIX, D_MODEL), D_MIX ** -0.5),
        "norm2": 1.0 + nrm(ks[15], (DEPTH, D_MODEL), 0.05),
        "w_gate": nrm(ks[16], (DEPTH, D_MODEL, D_FF), D_MODEL ** -0.5),
        "w_val": nrm(ks[17], (DEPTH, D_MODEL, D_FF), D_MODEL ** -0.5),
        "conv_w": nrm(ks[18], (DEPTH, CONV_W, D_FF), CONV_W ** -0.5),
        "conv_b": nrm(ks[19], (DEPTH, D_FF), 0.02),
        "w_down": nrm(ks[20], (DEPTH, D_FF, D_MODEL), D_FF ** -0.5),
        "norm_f": 1.0 + nrm(ks[21], (D_MODEL,), 0.05),
    }


def reference(x_prompt, x_sample, cache_nsa_kv, cache_win_kv, state_hgrn, state_ffn_conv, page_table,
              norm1, w_in, hg_lb_logits, hg_norm, cmp_pe, cmp_w1, cmp_w2, w_out, norm2,
              w_gate, w_val, conv_w, conv_b, w_down, norm_f):
    B, T = x_prompt.shape[:2]
    Bd, Td = x_sample.shape[:2]
    past = page_table.shape[1] * PAGE_SIZE
    win_len = cache_win_kv.shape[2]
    qpos_p = jnp.arange(T)
    qpos_s = past + jnp.arange(Td)
    win_kpos_s = past - win_len + jnp.arange(win_len + Td)
    lb_all = jnp.cumsum(jax.nn.softmax(hg_lb_logits.astype(jnp.float32), axis=0), axis=0)
    hp, hs = x_prompt, x_sample
    kv_p, kv_s, win_p, win_s, hg_p, hg_s, cv_p, cv_s = [], [], [], [], [], [], [], []
    for l in range(DEPTH):
        w = (lb_all[l], norm1[l], w_in[l], hg_norm[l], cmp_pe[l], cmp_w1[l], cmp_w2[l], w_out[l],
             norm2[l], w_gate[l], w_val[l], conv_w[l], conv_b[l], w_down[l])
        s0_p = jnp.zeros((B, HG_HEADS, HG_DK, HG_DV), hp.dtype)
        c0_p = jnp.zeros((B, CONV_W - 1, D_FF), hp.dtype)
        hp, a, b, c, d = layer(hp, None, None, s0_p, c0_p, qpos_p, qpos_p, min(WINDOW, T), *w)
        kv_p.append(a); win_p.append(b); hg_p.append(c); cv_p.append(d)
        kv_past = cache_nsa_kv[l][page_table].reshape(Bd, past, N_KV_SLOTS, NSA_KV_HEADS, NSA_DK)
        hs, a, b, c, d = layer(hs, kv_past, cache_win_kv[l], state_hgrn[l], state_ffn_conv[l],
                               qpos_s, win_kpos_s, win_len, *w)
        kv_s.append(a); win_s.append(b); hg_s.append(c); cv_s.append(d)
    y_prompt = rmsnorm(hp, norm_f)
    y_sample = rmsnorm(hs, norm_f)
    return (y_prompt, y_sample, jnp.stack(kv_p), jnp.stack(kv_s), jnp.stack(win_p), jnp.stack(win_s),
            jnp.stack(hg_p), jnp.stack(hg_s), jnp.stack(cv_p), jnp.stack(cv_s))
```

```python
import functools

import numpy as np
import jax
import jax.numpy as jnp
from jax import lax
from jax.experimental import pallas as pl
from jax.experimental.pallas import tpu as pltpu

F32 = jnp.float32
BF16 = jnp.bfloat16

D_MODEL = 1024
PAGE_SIZE = 128
HG_WIDTH = 512
HG_HEADS = 4
HG_DK = 128
HG_CHUNK = 64
NSA_WIDTH = 512
NSA_HEADS = 8
NSA_DK = 64
NSA_G = 2
NSA_R = 4
CMP_STRIDE = 16
CMP_BLOCK = 32
CMP_HIDDEN = 256
SLC_BLOCK = 64
SLC_TOPK = 16
WINDOW = 512
D_FF = 2816
CONV_W = 3
EPS = 1e-6
COL_KV = 512
COL_WIN = 256
COL_GATE = 24

LANES = 128
SUBLANES = 8
HG_SUB = 16
VMEM_LIMIT = 56 * 1024 * 1024

_C_HG = 4 * HG_WIDTH
_C_Q = _C_HG + NSA_WIDTH
_C_KV = _C_Q + COL_KV
_C_WIN = _C_KV + COL_WIN
_C_GATE = _C_WIN + LANES
D_IN = _C_WIN + COL_GATE


def _sigmoid(x):
    return 1.0 / (1.0 + jnp.exp(-x))


def _split3_bf16(x):
    hi = x.astype(BF16)
    r1 = x - hi.astype(F32)
    mid = r1.astype(BF16)
    lo = (r1 - mid.astype(F32)).astype(BF16)
    return hi, mid, lo


def _in_proj_kernel(x_ref, g_ref, w_ref, hg_ref, q_ref, kv_ref, win_ref, gate_ref):
    x = x_ref[...]
    ms = jnp.mean(x * x, axis=-1, keepdims=True)
    h = ((x * lax.rsqrt(ms + EPS)) * g_ref[...]).astype(BF16)
    hg_ref[...] = jnp.dot(h, w_ref[:, 0:_C_HG], preferred_element_type=F32)
    q_ref[...] = jnp.dot(h, w_ref[:, _C_HG:_C_Q], preferred_element_type=F32)
    kv_ref[...] = jnp.dot(h, w_ref[:, _C_Q:_C_KV], preferred_element_type=F32)
    win_ref[...] = jnp.dot(h, w_ref[:, _C_KV:_C_WIN], preferred_element_type=F32)
    gate_ref[...] = jnp.dot(h, w_ref[:, _C_WIN:_C_GATE], preferred_element_type=F32)


def _in_proj(x2d, norm_w, w_in_bf):
    n = x2d.shape[0]
    tm = 512
    assert n % tm == 0
    widths = (_C_HG, NSA_WIDTH, COL_KV, COL_WIN, LANES)
    return pl.pallas_call(
        _in_proj_kernel,
        grid=(n // tm,),
        in_specs=[
            pl.BlockSpec((tm, D_MODEL), lambda i: (i, 0)),
            pl.BlockSpec((1, D_MODEL), lambda i: (0, 0)),
            pl.BlockSpec((D_MODEL, _C_GATE), lambda i: (0, 0)),
        ],
        out_specs=[pl.BlockSpec((tm, w), lambda i: (i, 0)) for w in widths],
        out_shape=[jax.ShapeDtypeStruct((n, w), F32) for w in widths],
        compiler_params=pltpu.CompilerParams(dimension_semantics=("arbitrary",), vmem_limit_bytes=VMEM_LIMIT),
        name="in_proj",
    )(x2d, norm_w.reshape(1, D_MODEL), w_in_bf)


def _hgrn_kernel(q_ref, f_ref, i_ref, g_ref, lb_ref, nw_ref, s0_ref, e_ref, o_ref, sfin_ref, st_ref,
                 *, chunk, rows_per_step, t_valid, layer):
    C = chunk
    nsb = C // HG_SUB
    ti = pl.program_id(2)

    @pl.when(ti == 0)
    def _init():
        st_ref[...] = s0_ref[0, 0].T

    lg = lb_ref[...]
    le = jnp.exp(lg - jnp.max(lg, axis=0, keepdims=True))
    lb = jnp.sum(le[0:layer + 1, :], axis=0, keepdims=True) / jnp.sum(le, axis=0, keepdims=True)
    one_m_lb = 1.0 - lb
    nw = nw_ref[...]
    row_i = lax.broadcasted_iota(jnp.int32, (C, C), 0)
    col_i = lax.broadcasted_iota(jnp.int32, (C, C), 1)
    tri_bf = jnp.where(col_i <= row_i, 1.0, 0.0).astype(BF16)
    diag_mask = ((row_i // HG_SUB) == (col_i // HG_SUB)) & (col_i <= row_i)
    off_mask = (col_i // HG_SUB) < (row_i // HG_SUB)
    trel = lax.broadcasted_iota(jnp.int32, (C, HG_DK), 0) % HG_SUB
    nt_dims = (((1,), (1,)), ((), ()))
    tn_dims = (((0,), (0,)), ((), ()))

    def do_chunk(c, carry):
        r0 = pl.multiple_of(c * C, C)
        rows = pl.ds(r0, C)
        fr = f_ref[0, rows, :]
        qr = q_ref[0, rows, :]
        v = i_ref[0, rows, :]
        gr = g_ref[0, rows, :]
        f = lb + one_m_lb * _sigmoid(fr)
        kk = one_m_lb * _sigmoid(-fr)
        logf = jnp.log(f)
        if t_valid is not None:
            tok = ti * rows_per_step + r0 + lax.broadcasted_iota(jnp.int32, (C, HG_DK), 0)
            ok = tok < t_valid
            logf = jnp.where(ok, logf, 0.0)
            kk = jnp.where(ok, kk, 0.0)
        q = qr * _sigmoid(qr)
        hi, mid, lo = _split3_bf16(logf)
        a = (jnp.dot(tri_bf, hi, preferred_element_type=F32)
             + jnp.dot(tri_bf, mid, preferred_element_type=F32)
             + jnp.dot(tri_bf, lo, preferred_element_type=F32))
        st = st_ref[...]
        vb = v.astype(BF16)
        o = lax.dot_general((q * jnp.exp(a)).astype(BF16), st.astype(BF16), nt_dims, preferred_element_type=F32)

        if nsb > 1:
            parts = [jnp.zeros((HG_SUB, C), F32)]
            for i in range(1, nsb):
                bi = a[HG_SUB * i - 1:HG_SUB * i, :]
                qi = q[HG_SUB * i:HG_SUB * (i + 1), :] * jnp.exp(a[HG_SUB * i:HG_SUB * (i + 1), :] - bi)
                ki = kk * jnp.exp(jnp.minimum(bi - a, 0.0))
                parts.append(lax.dot_general(qi.astype(BF16), ki.astype(BF16), nt_dims, preferred_element_type=F32))
            att = jnp.where(off_mask, jnp.concatenate(parts, axis=0), 0.0)
        else:
            att = jnp.zeros((C, C), F32)

        a3 = a.reshape(nsb, HG_SUB, HG_DK)
        k3 = kk.reshape(nsb, HG_SUB, HG_DK)
        zs = []
        for j in range(HG_SUB):
            aj = jnp.broadcast_to(a3[:, j:j + 1, :], (nsb, HG_SUB, HG_DK)).reshape(C, HG_DK)
            kj = jnp.broadcast_to(k3[:, j:j + 1, :], (nsb, HG_SUB, HG_DK)).reshape(C, HG_DK)
            z = q * kj * jnp.exp(jnp.minimum(a - aj, 0.0))
            zs.append(jnp.where(trel >= j, z, 0.0).astype(BF16))
        zcat = jnp.concatenate(zs, axis=1)
        att_d = jnp.dot(zcat, e_ref[...], preferred_element_type=F32)
        att = att + jnp.where(diag_mask, att_d, 0.0)
        o = o + jnp.dot(att.astype(BF16), vb, preferred_element_type=F32)

        a_end = a[C - 1:C, :]
        kdec = kk * jnp.exp(a_end - a)
        st_ref[...] = st * jnp.exp(a_end) + lax.dot_general(vb, kdec.astype(BF16), tn_dims, preferred_element_type=F32)

        on = (o * lax.rsqrt(jnp.mean(o * o, axis=-1, keepdims=True) + EPS)) * nw
        o_ref[0, rows, :] = on * (gr * _sigmoid(gr))
        return carry

    lax.fori_loop(0, rows_per_step // C, do_chunk, 0)

    @pl.when(ti == pl.num_programs(2) - 1)
    def _fin():
        sfin_ref[0, 0] = st_ref[...].T


def _hgrn(z_hg, lb_logits, norm_w, s0, *, chunk, rows_per_step, t_valid, layer=0):
    b, t, _ = z_hg.shape
    n_lb = lb_logits.shape[0]
    assert t % rows_per_step == 0 and rows_per_step % chunk == 0 and chunk % HG_SUB == 0
    nt = t // rows_per_step
    sub = np.arange(chunk) % HG_SUB
    e_mat = jnp.asarray((np.repeat(np.arange(HG_SUB), HG_DK)[:, None] == sub[None, :]).astype(np.float32), BF16)

    def zspec(k):
        return pl.BlockSpec((1, rows_per_step, HG_DK), lambda bi, h, ti, k=k: (bi, ti, k * HG_HEADS + h))

    kern = functools.partial(_hgrn_kernel, chunk=chunk, rows_per_step=rows_per_step, t_valid=t_valid, layer=layer)
    return pl.pallas_call(
        kern,
        grid=(b, HG_HEADS, nt),
        in_specs=[
            zspec(0), zspec(1), zspec(2), zspec(3),
            pl.BlockSpec((n_lb, HG_DK), lambda bi, h, ti: (0, h)),
            pl.BlockSpec((1, HG_DK), lambda bi, h, ti: (0, h)),
            pl.BlockSpec((1, 1, HG_DK, HG_DK), lambda bi, h, ti: (bi, h, 0, 0)),
            pl.BlockSpec((HG_SUB * HG_DK, chunk), lambda bi, h, ti: (0, 0)),
        ],
        out_specs=[
            pl.BlockSpec((1, rows_per_step, HG_DK), lambda bi, h, ti: (bi, ti, h)),
            pl.BlockSpec((1, 1, HG_DK, HG_DK), lambda bi, h, ti: (bi, h, 0, 0)),
        ],
        out_shape=[jax.ShapeDtypeStruct((b, t, HG_WIDTH), F32), jax.ShapeDtypeStruct((b, HG_HEADS, HG_DK, HG_DK), F32)],
        scratch_shapes=[pltpu.VMEM((HG_DK, HG_DK), F32)],
        compiler_params=pltpu.CompilerParams(dimension_semantics=("arbitrary", "arbitrary", "arbitrary"),
                                             vmem_limit_bytes=VMEM_LIMIT),
        name="hgrn2",
    )(z_hg, z_hg, z_hg, z_hg, lb_logits, norm_w.reshape(1, HG_WIDTH), s0, e_mat)


KEY_TILE = 256
NEG_INF = float("-inf")
_NN = (((1,), (0,)), ((), ()))


def _compress_pair(load_xs, w1_ref, pe_ref, w2_ref, slot, nch):
    r = jnp.zeros((SUBLANES, 2 * CMP_HIDDEN), F32)
    for s in range(CMP_STRIDE):
        r = r + jnp.dot(pe_ref[slot, s], w1_ref[slot, s], preferred_element_type=F32)
    bias = r[0:1, 0:CMP_HIDDEN] + r[1:2, CMP_HIDDEN:2 * CMP_HIDDEN]
    outs = []
    for g in range(NSA_G):
        acc = jnp.zeros((nch, 2 * CMP_HIDDEN), F32)
        for s in range(CMP_STRIDE):
            xg = load_xs(s)[:, NSA_DK * g:NSA_DK * (g + 1)].astype(BF16)
            acc = acc + jnp.dot(xg, w1_ref[slot, s], preferred_element_type=F32)
        h = acc[:, 0:CMP_HIDDEN] + pltpu.roll(acc[:, CMP_HIDDEN:2 * CMP_HIDDEN], nch - 1, axis=0) + bias
        hs = (h * _sigmoid(h)).astype(BF16)
        outs.append(jnp.dot(hs, w2_ref[slot], preferred_element_type=F32))
    return jnp.concatenate(outs, axis=1)


def _flash_t(qtz, nq, load_k, load_vt, kt_lo, kt_hi, mask_fn):
    def body(kt, carry):
        m, l, acc = carry
        s = jnp.dot(load_k(kt), qtz, preferred_element_type=F32)
        s = jnp.where(mask_fn(kt), s, NEG_INF)
        m_new = jnp.maximum(m, jnp.max(s, axis=0, keepdims=True))
        m_safe = jnp.where(m_new == NEG_INF, 0.0, m_new)
        alpha = jnp.exp(m - m_safe)
        p = jnp.exp(s - m_safe)
        l = l * alpha + jnp.sum(p, axis=0, keepdims=True)
        acc = acc * alpha + jnp.dot(load_vt(kt), p.astype(BF16), preferred_element_type=F32)
        return m_new, l, acc

    init = (jnp.full((1, nq), NEG_INF, F32), jnp.zeros((1, nq), F32), jnp.zeros((NSA_DK, nq), F32))
    _, l, acc = lax.fori_loop(kt_lo, kt_hi, body, init)
    return acc / jnp.where(l > 0.0, l, 1.0)


def _nsa_attend(g, qtz, qpos, gate_raw, tq, nq, kc_ref, vct_ref, ov_ref, sc_ref, sel_ref, n_blk, n_slc,
                slc_k, slc_vt, slc_lo, slc_hi, win_k, win_vt, win_lo, win_hi, win_kbase):
    nb_pad = kc_ref.shape[0]
    nj_pad = ov_ref.shape[0]
    s = jnp.dot(kc_ref[...], qtz, preferred_element_type=F32)
    n_idx = lax.broadcasted_iota(jnp.int32, (nb_pad, nq), 0)
    cmask = (n_idx * CMP_STRIDE + (CMP_BLOCK - 1) <= qpos) & (n_idx < n_blk)
    s = jnp.where(cmask, s, NEG_INF)
    m = jnp.max(s, axis=0, keepdims=True)
    m = jnp.where(m == NEG_INF, 0.0, m)
    e = jnp.exp(s - m)
    d = jnp.sum(e, axis=0, keepdims=True)
    p = e / jnp.where(d > 0.0, d, 1.0)
    o_cmp = jnp.dot(vct_ref[NSA_DK * g:NSA_DK * (g + 1), :], p.astype(BF16), preferred_element_type=F32)

    if tq % LANES == 0:
        ps = p[:, 0:tq]
        for r in range(1, NSA_R):
            ps = ps + p[:, r * tq:(r + 1) * tq]
        p4 = jnp.concatenate([ps] * NSA_R, axis=1)
    else:
        assert nq == LANES and NSA_R * tq == LANES
        p4 = p
        for r in range(1, NSA_R):
            p4 = p4 + pltpu.roll(p, r * tq, axis=1)
    hi, mid, lo = _split3_bf16(p4)
    ov = ov_ref[...]
    imp = (jnp.dot(ov, hi, preferred_element_type=F32) + jnp.dot(ov, mid, preferred_element_type=F32)
           + jnp.dot(ov, lo, preferred_element_type=F32))
    j_idx = lax.broadcasted_iota(jnp.int32, (nj_pad, nq), 0)
    cur = qpos // SLC_BLOCK
    forced = (j_idx == 0) | (j_idx == cur) | (j_idx == cur - 1)
    score = jnp.where(forced, jnp.inf, jnp.where(j_idx <= cur, imp, NEG_INF))
    sc_ref[...] = score

    def rank_body(jp, rank):
        row = sc_ref[pl.ds(jp, 1), :]
        beats = (row > score) | ((row == score) & (jp < j_idx))
        return rank + jnp.where(beats, 1.0, 0.0)

    rank = lax.fori_loop(0, n_slc, rank_body, jnp.zeros((nj_pad, nq), F32))
    sel_ref[...] = jnp.where((rank < float(SLC_TOPK)) & (score > NEG_INF), 1.0, 0.0)

    blocks_per_tile = KEY_TILE // SLC_BLOCK
    krow = lax.broadcasted_iota(jnp.int32, (KEY_TILE, nq), 0)

    def slc_mask(kt):
        rows = [jnp.broadcast_to(sel_ref[pl.ds(kt * blocks_per_tile + jj, 1), :], (SLC_BLOCK, nq))
                for jj in range(blocks_per_tile)]
        return (jnp.concatenate(rows, axis=0) > 0.5) & (kt * KEY_TILE + krow <= qpos)

    o_slc = _flash_t(qtz, nq, slc_k, slc_vt(g), slc_lo, slc_hi, slc_mask)

    def win_mask(kt):
        kpos = win_kbase + kt * KEY_TILE + krow
        return (kpos <= qpos) & (kpos > qpos - WINDOW)

    o_win = _flash_t(qtz, nq, win_k, win_vt(g), win_lo, win_hi, win_mask)

    gt = _sigmoid(gate_raw)
    return gt[0:1, :] * o_cmp + gt[1:2, :] * o_slc + gt[2:3, :] * o_win


def _nsa_prompt_kernel(ck_ref, cv_ref, sk_ref, sv_ref, wk_ref, wv_ref, qtz_ref, gate_ref, ov_ref, w1_ref, pe_ref, w2_ref,
                       o_ref, kc_ref, vct_ref, svt_ref, wvt_ref, sc_ref, sel_ref, *, seq, tq):
    nq = NSA_R * tq
    nch = seq // CMP_STRIDE
    n_tiles = seq // KEY_TILE
    qt = pl.program_id(1)

    @pl.when(qt == 0)
    def _prep():
        kc = _compress_pair(lambda s: ck_ref[0, pl.ds(s, nch, stride=CMP_STRIDE), :], w1_ref, pe_ref, w2_ref, 0, nch)
        vc = _compress_pair(lambda s: cv_ref[0, pl.ds(s, nch, stride=CMP_STRIDE), :], w1_ref, pe_ref, w2_ref, 1, nch)
        kc_ref[...] = kc.astype(BF16)
        vct_ref[...] = vc.T.astype(BF16)
        for kt in range(n_tiles):
            rows = slice(kt * KEY_TILE, (kt + 1) * KEY_TILE)
            svt_ref[kt] = sv_ref[0, rows, :].T.astype(BF16)
            wvt_ref[kt] = wv_ref[0, rows, :].T.astype(BF16)

    t0 = qt * tq
    qpos = t0 + lax.broadcasted_iota(jnp.int32, (1, nq), 1) % tq
    hi_tile = (t0 + tq - 1) // KEY_TILE + 1
    win_lo = jnp.maximum(t0 - (WINDOW - 1), 0) // KEY_TILE

    def slc_k(kt):
        return sk_ref[0, pl.ds(pl.multiple_of(kt * KEY_TILE, KEY_TILE), KEY_TILE), :].astype(BF16)

    def win_k(kt):
        return wk_ref[0, pl.ds(pl.multiple_of(kt * KEY_TILE, KEY_TILE), KEY_TILE), :].astype(BF16)

    for g in range(NSA_G):
        o_ref[0, 0, g] = _nsa_attend(
            g, qtz_ref[0, 0, g], qpos, gate_ref[0, 0, g], tq, nq, kc_ref, vct_ref, ov_ref, sc_ref, sel_ref,
            nch - 1, seq // SLC_BLOCK,
            slc_k, lambda g_: (lambda kt: svt_ref[kt, NSA_DK * g_:NSA_DK * (g_ + 1), :]), 0, hi_tile,
            win_k, lambda g_: (lambda kt: wvt_ref[kt, NSA_DK * g_:NSA_DK * (g_ + 1), :]), win_lo, hi_tile, 0)


def _overlap_t(nj_pad, nb_pad):
    n = np.arange(nb_pad)[None, :]
    j = np.arange(nj_pad)[:, None]
    lo = np.maximum(n * CMP_STRIDE, j * SLC_BLOCK)
    hi = np.minimum(n * CMP_STRIDE + CMP_BLOCK, (j + 1) * SLC_BLOCK)
    return jnp.asarray(np.maximum(hi - lo, 0).astype(np.float32) / CMP_STRIDE, BF16)


def _nsa_prompt(kv_new, win_new, qtz, gate_t, w1k, pek, w2k, *, tq):
    b, seq, _ = kv_new.shape
    nqt = seq // tq
    nq = NSA_R * tq
    nch = seq // CMP_STRIDE
    n_slc = seq // SLC_BLOCK
    n_tiles = seq // KEY_TILE
    ov = _overlap_t(n_slc, nch)
    kern = functools.partial(_nsa_prompt_kernel, seq=seq, tq=tq)
    return pl.pallas_call(
        kern,
        grid=(b, nqt),
        in_specs=[
            pl.BlockSpec((1, seq, LANES), lambda bi, qi: (bi, 0, 0)),
            pl.BlockSpec((1, seq, LANES), lambda bi, qi: (bi, 0, 1)),
            pl.BlockSpec((1, seq, LANES), lambda bi, qi: (bi, 0, 2)),
            pl.BlockSpec((1, seq, LANES), lambda bi, qi: (bi, 0, 3)),
            pl.BlockSpec((1, seq, LANES), lambda bi, qi: (bi, 0, 0)),
            pl.BlockSpec((1, seq, LANES), lambda bi, qi: (bi, 0, 1)),
            pl.BlockSpec((1, 1, NSA_G, LANES, nq), lambda bi, qi: (bi, qi, 0, 0, 0)),
            pl.BlockSpec((1, 1, NSA_G, SUBLANES, nq), lambda bi, qi: (bi, qi, 0, 0, 0)),
            pl.BlockSpec((n_slc, nch), lambda bi, qi: (0, 0)),
            pl.BlockSpec(w1k.shape, lambda bi, qi: (0, 0, 0, 0)),
            pl.BlockSpec(pek.shape, lambda bi, qi: (0, 0, 0, 0)),
            pl.BlockSpec(w2k.shape, lambda bi, qi: (0, 0, 0)),
        ],
        out_specs=pl.BlockSpec((1, 1, NSA_G, NSA_DK, nq), lambda bi, qi: (bi, qi, 0, 0, 0)),
        out_shape=jax.ShapeDtypeStruct((b, nqt, NSA_G, NSA_DK, nq), F32),
        scratch_shapes=[
            pltpu.VMEM((nch, LANES), BF16),
            pltpu.VMEM((LANES, nch), BF16),
            pltpu.VMEM((n_tiles, LANES, KEY_TILE), BF16),
            pltpu.VMEM((n_tiles, LANES, KEY_TILE), BF16),
            pltpu.VMEM((n_slc, nq), F32),
            pltpu.VMEM((n_slc, nq), F32),
        ],
        compiler_params=pltpu.CompilerParams(dimension_semantics=("arbitrary", "arbitrary"), vmem_limit_bytes=VMEM_LIMIT),
        name="nsa_prompt",
    )(kv_new, kv_new, kv_new, kv_new, win_new, win_new, qtz, gate_t, ov, w1k, pek, w2k)


def _prep_cmp_weights(cmp_pe, cmp_w1, cmp_w2):
    w1k = jnp.concatenate([cmp_w1[:, 0], cmp_w1[:, 1]], axis=-1).astype(BF16)
    pek = jnp.pad(jnp.swapaxes(cmp_pe, 1, 2), ((0, 0), (0, 0), (0, SUBLANES - 2), (0, 0))).astype(BF16)
    return w1k, pek, cmp_w2.astype(BF16)


def _q_to_lanes(q, tq_real, tq):
    b, t, _ = q.shape
    nqt = t // tq_real
    x = (q * (NSA_DK ** -0.5)).reshape(b, nqt, tq_real, NSA_G, NSA_R, NSA_DK)
    x = x.transpose(0, 1, 3, 5, 4, 2)
    x = jnp.pad(x, ((0, 0),) * 5 + ((0, tq - tq_real),)).reshape(b, nqt, NSA_G, NSA_DK, NSA_R * tq)
    z = jnp.zeros_like(x)
    return jnp.stack([jnp.concatenate([x[:, :, 0], z[:, :, 0]], axis=2),
                      jnp.concatenate([z[:, :, 1], x[:, :, 1]], axis=2)], axis=2).astype(BF16)


def _gate_to_lanes(gate, tq_real, tq):
    b, t, _ = gate.shape
    nqt = t // tq_real
    x = gate[..., :COL_GATE].reshape(b, nqt, tq_real, 3, NSA_G, NSA_R).transpose(0, 1, 4, 3, 5, 2)
    x = jnp.pad(x, ((0, 0),) * 5 + ((0, tq - tq_real),)).reshape(b, nqt, NSA_G, 3, NSA_R * tq)
    return jnp.pad(x, ((0, 0), (0, 0), (0, 0), (0, SUBLANES - 3), (0, 0)))


def _o_from_lanes(o_t, tq_real, tq):
    b, nqt = o_t.shape[:2]
    x = o_t.reshape(b, nqt, NSA_G, NSA_DK, NSA_R, tq)[..., :tq_real]
    return x.transpose(0, 1, 5, 2, 4, 3).reshape(b, nqt * tq_real, NSA_WIDTH)


def _nsa_decode_kernel(pt_ref, pool_ref, kvn_ref, wck_ref, wcv_ref, wn_ref, qtz_ref, gate_ref, ov_ref, w1_ref, pe_ref, w2_ref,
                       o_ref, ckb, cvb, skb, svb, sem, wkb, wvb, kc_ref, vct_ref, svt_ref, wvt_ref, sc_ref, sel_ref,
                       *, past, n_new, tq):
    nq = NSA_R * tq
    n_pages = past // PAGE_SIZE
    nch = (past + n_new) // CMP_STRIDE
    assert nch * CMP_STRIDE <= past
    n_slc = -(-(past + n_new) // SLC_BLOCK)
    s_pad = skb.shape[1]
    n_tiles = s_pad // KEY_TILE
    w_tiles = wkb.shape[0] // KEY_TILE
    bufs = (ckb, cvb, skb, svb)
    b = pl.program_id(0)
    nb = pl.num_programs(0)
    slot = b % 2

    def page_copies(bb, sl, p):
        page = pt_ref[bb * n_pages + p]
        rows = pl.ds(pl.multiple_of(p * PAGE_SIZE, PAGE_SIZE), PAGE_SIZE)
        return [pltpu.make_async_copy(pool_ref.at[page, :, pl.ds(LANES * c, LANES)], bufs[c].at[sl, rows, :], sem.at[sl])
                for c in range(4)]

    def start_fetch(bb, sl):
        def body(p, carry):
            for cp in page_copies(bb, sl, p):
                cp.start()
            return carry
        lax.fori_loop(0, n_pages, body, 0)

    def wait_fetch(bb, sl):
        def body(p, carry):
            for cp in page_copies(bb, sl, p):
                cp.wait()
            return carry
        lax.fori_loop(0, n_pages, body, 0)

    @pl.when(b == 0)
    def _first():
        start_fetch(0, 0)

    @pl.when(b + 1 < nb)
    def _next():
        start_fetch(b + 1, 1 - slot)

    zpad = jnp.zeros((s_pad - past, LANES), F32)
    skb[slot, past:s_pad, :] = zpad
    svb[slot, past:s_pad, :] = zpad
    skb[slot, past:past + n_new, :] = kvn_ref[0, :, 2 * LANES:3 * LANES]
    svb[slot, past:past + n_new, :] = kvn_ref[0, :, 3 * LANES:4 * LANES]
    nwin = wck_ref.shape[1]
    for buf, cache_ref, lane0 in ((wkb, wck_ref, 0), (wvb, wcv_ref, LANES)):
        buf[0:nwin, :] = cache_ref[0]
        buf[nwin:, :] = jnp.zeros((buf.shape[0] - nwin, LANES), F32)
        buf[nwin:nwin + n_new, :] = wn_ref[0, :, lane0:lane0 + LANES]
    for kt in range(w_tiles):
        wvt_ref[kt] = wvb[kt * KEY_TILE:(kt + 1) * KEY_TILE, :].T.astype(BF16)

    wait_fetch(b, slot)

    kc = _compress_pair(lambda s: ckb[slot, pl.ds(s, nch, stride=CMP_STRIDE), :], w1_ref, pe_ref, w2_ref, 0, nch)
    vc = _compress_pair(lambda s: cvb[slot, pl.ds(s, nch, stride=CMP_STRIDE), :], w1_ref, pe_ref, w2_ref, 1, nch)
    kc_ref[...] = kc.astype(BF16)
    vct_ref[...] = vc.T.astype(BF16)

    def vt_body(kt, carry):
        rows = pl.ds(pl.multiple_of(kt * KEY_TILE, KEY_TILE), KEY_TILE)
        svt_ref[kt] = svb[slot, rows, :].T.astype(BF16)
        return carry
    lax.fori_loop(0, n_tiles, vt_body, 0)

    qpos = past + lax.broadcasted_iota(jnp.int32, (1, nq), 1) % tq

    def slc_k(kt):
        return skb[slot, pl.ds(pl.multiple_of(kt * KEY_TILE, KEY_TILE), KEY_TILE), :].astype(BF16)

    def win_k(kt):
        return wkb[pl.ds(pl.multiple_of(kt * KEY_TILE, KEY_TILE), KEY_TILE), :].astype(BF16)

    for g in range(NSA_G):
        o_ref[0, 0, g] = _nsa_attend(
            g, qtz_ref[0, 0, g], qpos, gate_ref[0, 0, g], tq, nq, kc_ref, vct_ref, ov_ref, sc_ref, sel_ref,
            nch - 1, n_slc,
            slc_k, lambda g_: (lambda kt: svt_ref[kt, NSA_DK * g_:NSA_DK * (g_ + 1), :]), 0, n_tiles,
            win_k, lambda g_: (lambda kt: wvt_ref[kt, NSA_DK * g_:NSA_DK * (g_ + 1), :]), 0, w_tiles, past - nwin)


def _nsa_decode(pool, page_table, kv_new, win_cache, win_new, qtz, gate_t, w1k, pek, w2k, *, tq):
    b, n_pages = page_table.shape
    n_new = kv_new.shape[1]
    nwin = win_cache.shape[1]
    past = n_pages * PAGE_SIZE
    nq = NSA_R * tq
    nch = (past + n_new) // CMP_STRIDE
    n_slc = -(-(past + n_new) // SLC_BLOCK)
    nj_pad = -(-n_slc // 16) * 16
    s_pad = -(-(past + n_new) // KEY_TILE) * KEY_TILE
    w_pad = -(-(nwin + n_new) // KEY_TILE) * KEY_TILE
    assert nwin % KEY_TILE == 0 and n_new <= LANES
    ov = _overlap_t(nj_pad, nch)
    kern = functools.partial(_nsa_decode_kernel, past=past, n_new=n_new, tq=tq)
    grid_spec = pltpu.PrefetchScalarGridSpec(
        num_scalar_prefetch=1,
        grid=(b,),
        in_specs=[
            pl.BlockSpec(memory_space=pl.ANY),
            pl.BlockSpec((1, n_new, COL_KV), lambda bi, pt: (bi, 0, 0)),
            pl.BlockSpec((1, nwin, LANES), lambda bi, pt: (bi, 0, 0)),
            pl.BlockSpec((1, nwin, LANES), lambda bi, pt: (bi, 0, 1)),
            pl.BlockSpec((1, n_new, COL_WIN), lambda bi, pt: (bi, 0, 0)),
            pl.BlockSpec((1, 1, NSA_G, LANES, nq), lambda bi, pt: (bi, 0, 0, 0, 0)),
            pl.BlockSpec((1, 1, NSA_G, SUBLANES, nq), lambda bi, pt: (bi, 0, 0, 0, 0)),
            pl.BlockSpec((nj_pad, nch), lambda bi, pt: (0, 0)),
            pl.BlockSpec(w1k.shape, lambda bi, pt: (0, 0, 0, 0)),
            pl.BlockSpec(pek.shape, lambda bi, pt: (0, 0, 0, 0)),
            pl.BlockSpec(w2k.shape, lambda bi, pt: (0, 0, 0)),
        ],
        out_specs=pl.BlockSpec((1, 1, NSA_G, NSA_DK, nq), lambda bi, pt: (bi, 0, 0, 0, 0)),
        scratch_shapes=[
            pltpu.VMEM((2, past, LANES), F32),
            pltpu.VMEM((2, past, LANES), F32),
            pltpu.VMEM((2, s_pad, LANES), F32),
            pltpu.VMEM((2, s_pad, LANES), F32),
            pltpu.SemaphoreType.DMA((2,)),
            pltpu.VMEM((w_pad, LANES), F32),
            pltpu.VMEM((w_pad, LANES), F32),
            pltpu.VMEM((nch, LANES), BF16),
            pltpu.VMEM((LANES, nch), BF16),
            pltpu.VMEM((s_pad // KEY_TILE, LANES, KEY_TILE), BF16),
            pltpu.VMEM((w_pad // KEY_TILE, LANES, KEY_TILE), BF16),
            pltpu.VMEM((nj_pad, nq), F32),
            pltpu.VMEM((nj_pad, nq), F32),
        ],
    )
    return pl.pallas_call(
        kern,
        grid_spec=grid_spec,
        out_shape=jax.ShapeDtypeStruct((b, 1, NSA_G, NSA_DK, nq), F32),
        compiler_params=pltpu.CompilerParams(dimension_semantics=("arbitrary",), vmem_limit_bytes=VMEM_LIMIT),
        name="nsa_decode",
    )(page_table.reshape(-1), pool, kv_new, win_cache, win_cache, win_new, qtz, gate_t, ov, w1k, pek, w2k)


FF_TILE = 256


def _rmsnorm(x, w):
    return (x * lax.rsqrt(jnp.mean(x * x, axis=-1, keepdims=True) + EPS)) * w


def _gelu_tanh(c):
    return c * (0.5 * (1.0 + jnp.tanh(0.7978845608028654 * (c + 0.044715 * (c * c * c)))))


def _ffn_kernel(x_ref, ohg_ref, onsa_ref, past_ref, wout_ref, n2_ref, wg_ref, wv_ref, cw_ref, cb_ref, wd_ref, nf_ref,
                y_ref, conv_ref, carry_ref, *, tm):
    ti = pl.program_id(1)

    @pl.when(ti == 0)
    def _init():
        carry_ref[0:CONV_W - 1, :] = past_ref[0]

    x1 = (x_ref[0]
          + jnp.dot(ohg_ref[0].astype(BF16), wout_ref[0:HG_WIDTH, :], preferred_element_type=F32)
          + jnp.dot(onsa_ref[0].astype(BF16), wout_ref[HG_WIDTH:, :], preferred_element_type=F32))
    h = _rmsnorm(x1, n2_ref[...]).astype(BF16)
    row = lax.broadcasted_iota(jnp.int32, (tm, FF_TILE), 0)
    acc = jnp.zeros((tm, D_MODEL), F32)
    for j in range(D_FF // FF_TILE):
        cols = slice(j * FF_TILE, (j + 1) * FF_TILE)
        u = jnp.dot(h, wg_ref[:, cols], preferred_element_type=F32)
        val = jnp.dot(h, wv_ref[:, cols], preferred_element_type=F32)
        c0 = carry_ref[0:1, cols]
        c1 = carry_ref[1:2, cols]
        up1 = jnp.where(row == 0, c1, pltpu.roll(u, 1, axis=0))
        up2 = jnp.where(row == 0, c0, jnp.where(row == 1, c1, pltpu.roll(u, 2, axis=0)))
        c = cb_ref[:, cols] + up2 * cw_ref[0:1, cols] + up1 * cw_ref[1:2, cols] + u * cw_ref[2:3, cols]
        tail = u[tm - (CONV_W - 1):tm, :]
        carry_ref[0:CONV_W - 1, cols] = tail
        conv_ref[0, :, cols] = tail
        acc = acc + jnp.dot((_gelu_tanh(c) * val).astype(BF16), wd_ref[cols, :], preferred_element_type=F32)
    y_ref[0] = _rmsnorm(x1 + acc, nf_ref[...])


def _ffn(x, o_hg, o_nsa, conv_past, wts, *, tm):
    b, t, _ = x.shape
    assert t % tm == 0 and D_FF % FF_TILE == 0
    wout, n2, wg, wv, cw, cb, wd, nf = wts
    const2 = lambda bi, ti: (0, 0)
    one = pl.Buffered(1)
    return pl.pallas_call(
        functools.partial(_ffn_kernel, tm=tm),
        grid=(b, t // tm),
        in_specs=[
            pl.BlockSpec((1, tm, D_MODEL), lambda bi, ti: (bi, ti, 0)),
            pl.BlockSpec((1, tm, HG_WIDTH), lambda bi, ti: (bi, ti, 0)),
            pl.BlockSpec((1, tm, NSA_WIDTH), lambda bi, ti: (bi, ti, 0)),
            pl.BlockSpec((1, CONV_W - 1, D_FF), lambda bi, ti: (bi, 0, 0)),
            pl.BlockSpec(wout.shape, const2, pipeline_mode=one),
            pl.BlockSpec(n2.shape, const2),
            pl.BlockSpec(wg.shape, const2, pipeline_mode=one),
            pl.BlockSpec(wv.shape, const2, pipeline_mode=one),
            pl.BlockSpec(cw.shape, const2),
            pl.BlockSpec(cb.shape, const2),
            pl.BlockSpec(wd.shape, const2, pipeline_mode=one),
            pl.BlockSpec(nf.shape, const2),
        ],
        out_specs=[
            pl.BlockSpec((1, tm, D_MODEL), lambda bi, ti: (bi, ti, 0)),
            pl.BlockSpec((1, CONV_W - 1, D_FF), lambda bi, ti: (bi, 0, 0)),
        ],
        out_shape=[jax.ShapeDtypeStruct((b, t, D_MODEL), F32), jax.ShapeDtypeStruct((b, CONV_W - 1, D_FF), F32)],
        scratch_shapes=[pltpu.VMEM((SUBLANES, D_FF), F32)],
        compiler_params=pltpu.CompilerParams(dimension_semantics=("arbitrary", "arbitrary"), vmem_limit_bytes=VMEM_LIMIT),
        name="ffn",
    )(x, o_hg, o_nsa, conv_past, wout, n2, wg, wv, cw, cb, wd, nf)


def _ffn_steps_kernel(x_ref, om_ref, past_ref, wout_ref, n2_ref, wg_ref, wv_ref, cw_ref, cb_ref, wd_ref, nf_ref,
                      y_ref, conv_ref, *, n_steps, nb):
    x1 = x_ref[...] + jnp.dot(om_ref[...].astype(BF16), wout_ref[...], preferred_element_type=F32)
    h = _rmsnorm(x1, n2_ref[...]).astype(BF16)
    acc = jnp.zeros((n_steps * nb, D_MODEL), F32)
    for j in range(D_FF // FF_TILE):
        cols = slice(j * FF_TILE, (j + 1) * FF_TILE)
        u = jnp.dot(h, wg_ref[:, cols], preferred_element_type=F32)
        val = jnp.dot(h, wv_ref[:, cols], preferred_element_type=F32)
        taps = [past_ref[k, :, cols] for k in range(CONV_W - 1)] + [u[t * nb:(t + 1) * nb, :] for t in range(n_steps)]
        cs = []
        for t in range(n_steps):
            c = cb_ref[:, cols]
            for k in range(CONV_W):
                c = c + taps[t + k] * cw_ref[k:k + 1, cols]
            cs.append(c)
        for k in range(CONV_W - 1):
            conv_ref[k, :, cols] = taps[n_steps + k]
        c_all = jnp.concatenate(cs, axis=0)
        acc = acc + jnp.dot((_gelu_tanh(c_all) * val).astype(BF16), wd_ref[cols, :], preferred_element_type=F32)
    y_ref[...] = _rmsnorm(x1 + acc, nf_ref[...])


def _ffn_steps(x_t, omix_t, past_t, wts, *, n_steps, nb):
    return pl.pallas_call(
        functools.partial(_ffn_steps_kernel, n_steps=n_steps, nb=nb),
        out_shape=[jax.ShapeDtypeStruct((n_steps * nb, D_MODEL), F32), jax.ShapeDtypeStruct((CONV_W - 1, nb, D_FF), F32)],
        compiler_params=pltpu.CompilerParams(vmem_limit_bytes=VMEM_LIMIT),
        name="ffn_steps",
    )(x_t, omix_t, past_t, *wts)


def kernel(x_prompt, x_sample, cache_nsa_kv, cache_win_kv, state_hgrn, state_ffn_conv, page_table, norm1, w_in, hg_lb_logits, hg_norm, cmp_pe, cmp_w1, cmp_w2, w_out, norm2, w_gate, w_val, conv_w, conv_b, w_down, norm_f):
    B, T, _ = x_prompt.shape
    Bd, Td, _ = x_sample.shape
    depth = norm1.shape[0]
    assert depth == 1
    l = 0
    w_in_bf = jnp.pad(w_in[l], ((0, 0), (0, _C_GATE - D_IN))).astype(BF16)
    w1k, pek, w2k = _prep_cmp_weights(cmp_pe[l], cmp_w1[l], cmp_w2[l])
    ffn_w = (w_out[l].astype(BF16), norm2[l].reshape(1, D_MODEL), w_gate[l].astype(BF16), w_val[l].astype(BF16),
             conv_w[l], conv_b[l].reshape(1, D_FF), w_down[l].astype(BF16), norm_f.reshape(1, D_MODEL))

    z_hg, q_p, kv_p, win_p, gate_p = _in_proj(x_prompt.reshape(B * T, D_MODEL), norm1[l], w_in_bf)
    o_hg_p, s_p = _hgrn(z_hg.reshape(B, T, _C_HG), hg_lb_logits, hg_norm[l],
                        jnp.zeros((B, HG_HEADS, HG_DK, HG_DK), F32), chunk=HG_CHUNK, rows_per_step=512, t_valid=None)
    tq = 128
    kv_p = kv_p.reshape(B, T, COL_KV)
    win_p = win_p.reshape(B, T, COL_WIN)
    o_t = _nsa_prompt(kv_p, win_p, _q_to_lanes(q_p.reshape(B, T, NSA_WIDTH), tq, tq),
                      _gate_to_lanes(gate_p.reshape(B, T, LANES), tq, tq), w1k, pek, w2k, tq=tq)
    o_nsa_p = _o_from_lanes(o_t, tq, tq)
    y_p, conv_p = _ffn(x_prompt, o_hg_p, o_nsa_p, jnp.zeros((B, CONV_W - 1, D_FF), F32), ffn_w, tm=512)

    z_hg, q_s, kv_s, win_s, gate_s = _in_proj(x_sample.reshape(Bd * Td, D_MODEL), norm1[l], w_in_bf)
    t_pad = HG_SUB
    z_pad = jnp.pad(z_hg.reshape(Bd, Td, _C_HG), ((0, 0), (0, t_pad - Td), (0, 0)))
    o_hg_s, s_s = _hgrn(z_pad, hg_lb_logits, hg_norm[l], state_hgrn[l], chunk=t_pad, rows_per_step=t_pad, t_valid=Td)
    o_hg_s = o_hg_s[:, :Td]
    tqs = LANES // NSA_R
    kv_s = kv_s.reshape(Bd, Td, COL_KV)
    win_s = win_s.reshape(Bd, Td, COL_WIN)
    n_pool = cache_nsa_kv.shape[1]
    nwin = cache_win_kv.shape[2]
    o_t = _nsa_decode(cache_nsa_kv[l].reshape(n_pool, PAGE_SIZE, COL_KV), page_table, kv_s,
                      cache_win_kv[l].reshape(Bd, nwin, COL_WIN), win_s,
                      _q_to_lanes(q_s.reshape(Bd, Td, NSA_WIDTH), Td, tqs),
                      _gate_to_lanes(gate_s.reshape(Bd, Td, LANES), Td, tqs), w1k, pek, w2k, tq=tqs)
    o_nsa_s = _o_from_lanes(o_t, Td, tqs)
    omix_t = jnp.concatenate([o_hg_s, o_nsa_s], axis=-1).transpose(1, 0, 2).reshape(Td * Bd, D_MODEL)
    y_t, conv_t = _ffn_steps(x_sample.transpose(1, 0, 2).reshape(Td * Bd, D_MODEL), omix_t,
                             state_ffn_conv[l].transpose(1, 0, 2), ffn_w, n_steps=Td, nb=Bd)
    y_s = y_t.reshape(Td, Bd, D_MODEL).transpose(1, 0, 2)
    conv_s = conv_t.transpose(1, 0, 2)

    win_keep_p = min(WINDOW, T)
    win_all_s = jnp.concatenate([cache_win_kv[l].reshape(Bd, nwin, COL_WIN), win_s], axis=1)[:, -nwin:]
    return (y_p, y_s,
            kv_p.reshape(1, B, T, 4, NSA_G, NSA_DK), kv_s.reshape(1, Bd, Td, 4, NSA_G, NSA_DK),
            win_p[:, T - win_keep_p:].reshape(1, B, win_keep_p, 2, NSA_G, NSA_DK),
            win_all_s.reshape(1, Bd, nwin, 2, NSA_G, NSA_DK),
            s_p[None], s_s[None], conv_p[None], conv_s[None])
```

```python
import functools

import numpy as np
import jax
import jax.numpy as jnp
from jax import lax
from jax.experimental import pallas as pl
from jax.experimental.pallas import tpu as pltpu

F32 = jnp.float32
BF16 = jnp.bfloat16

D_MODEL = 1024
PAGE_SIZE = 128
HG_WIDTH = 512
HG_HEADS = 4
HG_DK = 128
HG_CHUNK = 64
NSA_WIDTH = 512
NSA_HEADS = 8
NSA_DK = 64
NSA_G = 2
NSA_R = 4
CMP_STRIDE = 16
CMP_BLOCK = 32
CMP_HIDDEN = 256
SLC_BLOCK = 64
SLC_TOPK = 16
WINDOW = 512
D_FF = 2816
CONV_W = 3
EPS = 1e-6
COL_KV = 512
COL_WIN = 256
COL_GATE = 24

LANES = 128
SUBLANES = 8
HG_SUB = 16
HG_UNROLL = 8
HG_MASKED_EXP = -1e30
VMEM_LIMIT = 56 * 1024 * 1024

_C_HG = 4 * HG_WIDTH
_C_Q = _C_HG + NSA_WIDTH
_C_KV = _C_Q + COL_KV
_C_WIN = _C_KV + COL_WIN
_C_GATE = _C_WIN + LANES
D_IN = _C_WIN + COL_GATE


def _sigmoid(x):
    return 1.0 / (1.0 + jnp.exp(-x))


def _split3_bf16(x):
    hi = x.astype(BF16)
    r1 = x - hi.astype(F32)
    mid = r1.astype(BF16)
    lo = (r1 - mid.astype(F32)).astype(BF16)
    return hi, mid, lo


def _in_proj_kernel(x_ref, g_ref, w_ref, hg_ref, q_ref, kv_ref, win_ref, gate_ref):
    x = x_ref[...]
    ms = jnp.mean(x * x, axis=-1, keepdims=True)
    h = ((x * lax.rsqrt(ms + EPS)) * g_ref[...]).astype(BF16)
    hg_ref[...] = jnp.dot(h, w_ref[:, 0:_C_HG], preferred_element_type=F32)
    q_ref[...] = jnp.dot(h, w_ref[:, _C_HG:_C_Q], preferred_element_type=F32)
    kv_ref[...] = jnp.dot(h, w_ref[:, _C_Q:_C_KV], preferred_element_type=F32)
    win_ref[...] = jnp.dot(h, w_ref[:, _C_KV:_C_WIN], preferred_element_type=F32)
    gate_ref[...] = jnp.dot(h, w_ref[:, _C_WIN:_C_GATE], preferred_element_type=F32)


def _in_proj(x2d, norm_w, w_in_bf):
    n = x2d.shape[0]
    tm = 512
    assert n % tm == 0
    widths = (_C_HG, NSA_WIDTH, COL_KV, COL_WIN, LANES)
    return pl.pallas_call(
        _in_proj_kernel,
        grid=(n // tm,),
        in_specs=[
            pl.BlockSpec((tm, D_MODEL), lambda i: (i, 0)),
            pl.BlockSpec((1, D_MODEL), lambda i: (0, 0)),
            pl.BlockSpec((D_MODEL, _C_GATE), lambda i: (0, 0)),
        ],
        out_specs=[pl.BlockSpec((tm, w), lambda i: (i, 0)) for w in widths],
        out_shape=[jax.ShapeDtypeStruct((n, w), F32) for w in widths],
        compiler_params=pltpu.CompilerParams(dimension_semantics=("arbitrary",), vmem_limit_bytes=VMEM_LIMIT),
        name="in_proj",
    )(x2d, norm_w.reshape(1, D_MODEL), w_in_bf)


def _hgrn_kernel(q_ref, f_ref, i_ref, g_ref, lb_ref, nw_ref, s0_ref, e_ref, o_ref, sfin_ref, st_ref,
                 *, chunk, rows_per_step, t_valid, layer):
    C = chunk
    nsb = C // HG_SUB
    ti = pl.program_id(2)

    @pl.when(ti == 0)
    def _init():
        st_ref[...] = s0_ref[0, 0].T

    lg = lb_ref[...]
    le = jnp.exp(lg - jnp.max(lg, axis=0, keepdims=True))
    lb = jnp.sum(le[0:layer + 1, :], axis=0, keepdims=True) / jnp.sum(le, axis=0, keepdims=True)
    one_m_lb = 1.0 - lb
    nw = nw_ref[...]
    row_i = lax.broadcasted_iota(jnp.int32, (C, C), 0)
    col_i = lax.broadcasted_iota(jnp.int32, (C, C), 1)
    tri_bf = jnp.where(col_i <= row_i, 1.0, 0.0).astype(BF16)
    diag_mask = ((row_i // HG_SUB) == (col_i // HG_SUB)) & (col_i <= row_i)
    off_mask = (col_i // HG_SUB) < (row_i // HG_SUB)
    trel = lax.broadcasted_iota(jnp.int32, (C, HG_DK), 0) % HG_SUB
    nt_dims = (((1,), (1,)), ((), ()))
    tn_dims = (((0,), (0,)), ((), ()))

    def do_chunk(c, carry):
        r0 = pl.multiple_of(c * C, C)
        rows = pl.ds(r0, C)
        fr = f_ref[0, rows, :]
        qr = q_ref[0, rows, :]
        v = i_ref[0, rows, :]
        gr = g_ref[0, rows, :]
        f = lb + one_m_lb * _sigmoid(fr)
        kk = one_m_lb * _sigmoid(-fr)
        logf = jnp.log2(f)
        if t_valid is not None:
            tok = ti * rows_per_step + r0 + lax.broadcasted_iota(jnp.int32, (C, HG_DK), 0)
            ok = tok < t_valid
            logf = jnp.where(ok, logf, 0.0)
            kk = jnp.where(ok, kk, 0.0)
        q = qr * _sigmoid(qr)
        hi, mid, lo = _split3_bf16(logf)
        a = (jnp.dot(tri_bf, hi, preferred_element_type=F32)
             + jnp.dot(tri_bf, mid, preferred_element_type=F32)
             + jnp.dot(tri_bf, lo, preferred_element_type=F32))
        st = st_ref[...]
        vb = v.astype(BF16)
        o = lax.dot_general((q * jnp.exp2(a)).astype(BF16), st.astype(BF16), nt_dims, preferred_element_type=F32)

        if nsb > 1:
            parts = [jnp.zeros((HG_SUB, C), F32)]
            for i in range(1, nsb):
                bi = a[HG_SUB * i - 1:HG_SUB * i, :]
                qi = q[HG_SUB * i:HG_SUB * (i + 1), :] * jnp.exp2(a[HG_SUB * i:HG_SUB * (i + 1), :] - bi)
                ki = kk * jnp.exp2(jnp.minimum(bi - a, 0.0))
                parts.append(lax.dot_general(qi.astype(BF16), ki.astype(BF16), nt_dims, preferred_element_type=F32))
            att = jnp.where(off_mask, jnp.concatenate(parts, axis=0), 0.0)
        else:
            att = jnp.zeros((C, C), F32)

        a3 = a.reshape(nsb, HG_SUB, HG_DK)
        k3 = kk.reshape(nsb, HG_SUB, HG_DK)
        zs = []
        for j in range(HG_SUB):
            aj = jnp.broadcast_to(a3[:, j:j + 1, :], (nsb, HG_SUB, HG_DK)).reshape(C, HG_DK)
            kj = jnp.broadcast_to(k3[:, j:j + 1, :], (nsb, HG_SUB, HG_DK)).reshape(C, HG_DK)
            expo = a - aj if j == 0 else jnp.where(trel >= j, a - aj, HG_MASKED_EXP)
            zs.append((q * kj * jnp.exp2(expo)).astype(BF16))
        zcat = jnp.concatenate(zs, axis=1)
        att_d = jnp.dot(zcat, e_ref[...], preferred_element_type=F32)
        att = att + jnp.where(diag_mask, att_d, 0.0)
        o = o + jnp.dot(att.astype(BF16), vb, preferred_element_type=F32)

        a_end = a[C - 1:C, :]
        kdec = kk * jnp.exp2(a_end - a)
        st_ref[...] = st * jnp.exp2(a_end) + lax.dot_general(vb, kdec.astype(BF16), tn_dims, preferred_element_type=F32)

        on = (o * lax.rsqrt(jnp.mean(o * o, axis=-1, keepdims=True) + EPS)) * nw
        o_ref[0, rows, :] = on * (gr * _sigmoid(gr))
        return carry

    n_chunks = rows_per_step // C
    lax.fori_loop(0, n_chunks, do_chunk, 0, unroll=min(n_chunks, HG_UNROLL))

    @pl.when(ti == pl.num_programs(2) - 1)
    def _fin():
        sfin_ref[0, 0] = st_ref[...].T


def _hgrn(z_hg, lb_logits, norm_w, s0, *, chunk, rows_per_step, t_valid, layer=0):
    b, t, _ = z_hg.shape
    n_lb = lb_logits.shape[0]
    assert t % rows_per_step == 0 and rows_per_step % chunk == 0 and chunk % HG_SUB == 0
    nt = t // rows_per_step
    sub = np.arange(chunk) % HG_SUB
    e_mat = jnp.asarray((np.repeat(np.arange(HG_SUB), HG_DK)[:, None] == sub[None, :]).astype(np.float32), BF16)

    def zspec(k):
        return pl.BlockSpec((1, rows_per_step, HG_DK), lambda bi, h, ti, k=k: (bi, ti, k * HG_HEADS + h))

    kern = functools.partial(_hgrn_kernel, chunk=chunk, rows_per_step=rows_per_step, t_valid=t_valid, layer=layer)
    return pl.pallas_call(
        kern,
        grid=(b, HG_HEADS, nt),
        in_specs=[
            zspec(0), zspec(1), zspec(2), zspec(3),
            pl.BlockSpec((n_lb, HG_DK), lambda bi, h, ti: (0, h)),
            pl.BlockSpec((1, HG_DK), lambda bi, h, ti: (0, h)),
            pl.BlockSpec((1, 1, HG_DK, HG_DK), lambda bi, h, ti: (bi, h, 0, 0)),
            pl.BlockSpec((HG_SUB * HG_DK, chunk), lambda bi, h, ti: (0, 0)),
        ],
        out_specs=[
            pl.BlockSpec((1, rows_per_step, HG_DK), lambda bi, h, ti: (bi, ti, h)),
            pl.BlockSpec((1, 1, HG_DK, HG_DK), lambda bi, h, ti: (bi, h, 0, 0)),
        ],
        out_shape=[jax.ShapeDtypeStruct((b, t, HG_WIDTH), F32), jax.ShapeDtypeStruct((b, HG_HEADS, HG_DK, HG_DK), F32)],
        scratch_shapes=[pltpu.VMEM((HG_DK, HG_DK), F32)],
        compiler_params=pltpu.CompilerParams(dimension_semantics=("arbitrary", "arbitrary", "arbitrary"),
                                             vmem_limit_bytes=VMEM_LIMIT),
        name="hgrn2",
    )(z_hg, z_hg, z_hg, z_hg, lb_logits, norm_w.reshape(1, HG_WIDTH), s0, e_mat)


KEY_TILE = 256
CMP_PACK = 4
DEC_TILES = 3
LOG2E = 1.4426950408889634
Q_SCALE = LOG2E * NSA_DK ** -0.5
NEG_INF = float("-inf")


def _compress_pair(load_xs, w1_ref, pe_ref, w2_ref, slot, nch):
    r = jnp.zeros((SUBLANES, 2 * CMP_HIDDEN), F32)
    for u in range(CMP_STRIDE // CMP_PACK):
        r = r + jnp.dot(pe_ref[slot, u], w1_ref[slot, u], preferred_element_type=F32)
    bias = r[0:1, 0:CMP_HIDDEN] + r[1:2, CMP_HIDDEN:2 * CMP_HIDDEN]
    outs = []
    for g in range(NSA_G):
        acc = jnp.zeros((nch, 2 * CMP_HIDDEN), F32)
        for u in range(CMP_STRIDE // CMP_PACK):
            xg = jnp.concatenate([load_xs(CMP_PACK * u + i)[:, NSA_DK * g:NSA_DK * (g + 1)] for i in range(CMP_PACK)],
                                 axis=1).astype(BF16)
            acc = acc + jnp.dot(xg, w1_ref[slot, u], preferred_element_type=F32)
        h = acc[:, 0:CMP_HIDDEN] + pltpu.roll(acc[:, CMP_HIDDEN:2 * CMP_HIDDEN], nch - 1, axis=0) + bias
        hs = (h * _sigmoid(h)).astype(BF16)
        outs.append(jnp.dot(hs, w2_ref[slot], preferred_element_type=F32))
    return jnp.concatenate(outs, axis=1)


def _rank_bias(score, n_slc):
    nj, w = score.shape
    jrow = lax.broadcasted_iota(jnp.int32, (SUBLANES, w), 0)
    slabs = [score[SUBLANES * v:SUBLANES * (v + 1), :] for v in range(nj // SUBLANES)]
    ranks = [jnp.zeros((SUBLANES, w), F32) for _ in slabs]
    for jp in range(n_slc):
        row = score[jp:jp + 1, :]
        for v, slab in enumerate(slabs):
            lo = SUBLANES * v
            if lo > jp:
                beats = row >= slab
            elif lo + SUBLANES - 1 <= jp:
                beats = row > slab
            else:
                beats = (row > slab) | ((row == slab) & (jrow > jp - lo))
            ranks[v] = ranks[v] + jnp.where(beats, 1.0, 0.0)
    rank = jnp.concatenate(ranks, axis=0)
    return jnp.where((rank < float(SLC_TOPK)) & (score > NEG_INF), 0.0, NEG_INF)


def _cmp_branch(g, qtz, qpos, tq, nq, kc_ref, vct_ref, ov_ref, n_blk):
    nb_pad = kc_ref.shape[0]
    nj_pad = ov_ref.shape[0]
    s = jnp.dot(kc_ref[...], qtz, preferred_element_type=F32)
    n_idx = lax.broadcasted_iota(jnp.int32, (nb_pad, nq), 0)
    cmask = (n_idx * CMP_STRIDE + (CMP_BLOCK - 1) <= qpos) & (n_idx < n_blk)
    s = jnp.where(cmask, s, NEG_INF)
    m = jnp.max(s, axis=0, keepdims=True)
    m = jnp.where(m == NEG_INF, 0.0, m)
    e = jnp.exp2(s - m)
    d = jnp.sum(e, axis=0, keepdims=True)
    p = e / jnp.where(d > 0.0, d, 1.0)
    o_cmp = jnp.dot(vct_ref[NSA_DK * g:NSA_DK * (g + 1), :], p.astype(BF16), preferred_element_type=F32)

    if tq % LANES == 0:
        ps = p[:, 0:tq]
        for r in range(1, NSA_R):
            ps = ps + p[:, r * tq:(r + 1) * tq]
        qpos_w = qpos[:, 0:tq]
    else:
        assert nq == LANES and NSA_R * tq == LANES
        ps = p
        for r in range(1, NSA_R):
            ps = ps + pltpu.roll(p, r * tq, axis=1)
        qpos_w = qpos
    w = ps.shape[1]
    hi, mid, lo = _split3_bf16(ps)
    ov = ov_ref[...]
    imp = (jnp.dot(ov, hi, preferred_element_type=F32) + jnp.dot(ov, mid, preferred_element_type=F32)
           + jnp.dot(ov, lo, preferred_element_type=F32))
    j_idx = lax.broadcasted_iota(jnp.int32, (nj_pad, w), 0)
    cur = qpos_w // SLC_BLOCK
    forced = (j_idx == 0) | (j_idx == cur) | (j_idx == cur - 1)
    score = jnp.where(forced, jnp.inf, jnp.where(j_idx <= cur, imp, NEG_INF))
    return o_cmp, score


def _select_blocks(scores, n_slc, tq, nq, selb_ref):
    w = scores[0].shape[1]
    bias = _rank_bias(jnp.concatenate(scores, axis=1), n_slc)
    for g in range(NSA_G):
        bg = bias[:, g * w:(g + 1) * w]
        selb_ref[g] = bg if w == nq else jnp.concatenate([bg] * (nq // w), axis=1)


def _block_rows(rows, nq):
    n = rows.shape[0]
    return jnp.broadcast_to(rows[:, None, :], (n, SLC_BLOCK, nq)).reshape(n * SLC_BLOCK, nq)


def _attn_step(carry, k, qtz, vts, biases):
    out = []
    for g in range(NSA_G):
        m, l, acc = carry[g]
        s = jnp.dot(k, qtz[g], preferred_element_type=F32) + biases[g]
        m_new = jnp.maximum(m, jnp.max(s, axis=0, keepdims=True))
        m_safe = jnp.where(m_new == NEG_INF, 0.0, m_new)
        alpha = jnp.exp2(m - m_safe)
        p = jnp.exp2(s - m_safe)
        l = l * alpha + jnp.sum(p, axis=0, keepdims=True)
        acc = acc * alpha + jnp.dot(vts[g], p.astype(BF16), preferred_element_type=F32)
        out.append((m_new, l, acc))
    return tuple(out)


def _attn_init(nq):
    return tuple((jnp.full((1, nq), NEG_INF, F32), jnp.zeros((1, nq), F32), jnp.zeros((NSA_DK, nq), F32))
                 for _ in range(NSA_G))


def _attn_finish(carry):
    return [acc / jnp.where(l > 0.0, l, 1.0) for (_, l, acc) in carry]


def _causal_bias(kpos, qpos):
    return jnp.where(kpos <= qpos, 0.0, NEG_INF)


def _window_bias(kpos, qpos):
    d = kpos - qpos
    return jnp.where((d <= 0) & (d > -WINDOW), 0.0, NEG_INF)


def _gated_sum(gate_raw, o_cmp, o_slc, o_win):
    gt = _sigmoid(gate_raw)
    return gt[0:1, :] * o_cmp + gt[1:2, :] * o_slc + gt[2:3, :] * o_win


def _nsa_prompt_kernel(ck_ref, cv_ref, sk_ref, sv_ref, wk_ref, wv_ref, q_ref, gate_ref, ov_ref, w1_ref, pe_ref, w2_ref,
                       o_ref, kc_ref, vct_ref, svt_ref, wvt_ref, selb_ref, *, seq, tq):
    nq = NSA_R * tq
    nch = seq // CMP_STRIDE
    n_tiles = seq // KEY_TILE
    n_slc = seq // SLC_BLOCK
    blocks_per_tile = KEY_TILE // SLC_BLOCK
    assert 2 * blocks_per_tile == SUBLANES
    qt = pl.program_id(1)

    @pl.when(qt == 0)
    def _prep():
        kc = _compress_pair(lambda s: ck_ref[0, pl.ds(s, nch, stride=CMP_STRIDE), :], w1_ref, pe_ref, w2_ref, 0, nch)
        vc = _compress_pair(lambda s: cv_ref[0, pl.ds(s, nch, stride=CMP_STRIDE), :], w1_ref, pe_ref, w2_ref, 1, nch)
        kc_ref[...] = kc.astype(BF16)
        vct_ref[...] = vc.T.astype(BF16)
        for kt in range(n_tiles):
            rows = slice(kt * KEY_TILE, (kt + 1) * KEY_TILE)
            svt_ref[kt] = sv_ref[0, rows, :].T.astype(BF16)
            wvt_ref[kt] = wv_ref[0, rows, :].T.astype(BF16)

    t0 = qt * tq
    qpos = t0 + lax.broadcasted_iota(jnp.int32, (1, nq), 1) % tq
    hi_tile = (t0 + tq - 1) // KEY_TILE + 1
    win_lo = jnp.maximum(t0 - (WINDOW - 1), 0) // KEY_TILE
    krow = lax.broadcasted_iota(jnp.int32, (KEY_TILE, nq), 0)

    gate_t = gate_ref[0].T
    zeros_half = jnp.zeros((NSA_DK, nq), F32)
    qtz, gates = [], []
    for g in range(NSA_G):
        qg_t = q_ref[0, :, g * NSA_R * NSA_DK:(g + 1) * NSA_R * NSA_DK].T
        q_t = jnp.concatenate([qg_t[r * NSA_DK:(r + 1) * NSA_DK, :] for r in range(NSA_R)], axis=1) * Q_SCALE
        qtz.append(jnp.concatenate([q_t, zeros_half] if g == 0 else [zeros_half, q_t], axis=0).astype(BF16))
        gates.append(jnp.concatenate(
            [jnp.concatenate([gate_t[k * NSA_HEADS + g * NSA_R + r:k * NSA_HEADS + g * NSA_R + r + 1, :]
                              for r in range(NSA_R)], axis=1) for k in range(3)], axis=0))

    o_cmp, scores = [], []
    for g in range(NSA_G):
        oc, sc = _cmp_branch(g, qtz[g], qpos, tq, nq, kc_ref, vct_ref, ov_ref, nch - 1)
        o_cmp.append(oc)
        scores.append(sc)
    _select_blocks(scores, n_slc, tq, nq, selb_ref)

    def slc_step(kt, carry, causal):
        rows = pl.ds(pl.multiple_of(kt * KEY_TILE, KEY_TILE), KEY_TILE)
        k = sk_ref[0, rows, :].astype(BF16)
        biases = []
        for g in range(NSA_G):
            rows8 = selb_ref[g, pl.ds(pl.multiple_of((kt // 2) * SUBLANES, SUBLANES), SUBLANES), :]
            rows4 = jnp.where(kt % 2 == 0, rows8[0:blocks_per_tile, :], rows8[blocks_per_tile:, :])
            bg = _block_rows(rows4, nq)
            biases.append(bg + _causal_bias(kt * KEY_TILE + krow, qpos) if causal else bg)
        vts = [svt_ref[kt, NSA_DK * g:NSA_DK * (g + 1), :] for g in range(NSA_G)]
        return _attn_step(carry, k, qtz, vts, biases)

    carry = lax.fori_loop(0, hi_tile - 1, lambda kt, c: slc_step(kt, c, False), _attn_init(nq))
    o_slc = _attn_finish(slc_step(hi_tile - 1, carry, True))

    def win_step(kt, carry):
        rows = pl.ds(pl.multiple_of(kt * KEY_TILE, KEY_TILE), KEY_TILE)
        k = wk_ref[0, rows, :].astype(BF16)
        wb = _window_bias(kt * KEY_TILE + krow, qpos)
        vts = [wvt_ref[kt, NSA_DK * g:NSA_DK * (g + 1), :] for g in range(NSA_G)]
        return _attn_step(carry, k, qtz, vts, [wb, wb])

    o_win = _attn_finish(lax.fori_loop(win_lo, hi_tile, win_step, _attn_init(nq)))

    for g in range(NSA_G):
        o_t = _gated_sum(gates[g], o_cmp[g], o_slc[g], o_win[g])
        stack = jnp.concatenate([o_t[:, r * tq:(r + 1) * tq] for r in range(NSA_R)], axis=0)
        o_ref[0, :, g * NSA_R * NSA_DK:(g + 1) * NSA_R * NSA_DK] = stack.T


def _overlap_t(nj_pad, nb_pad):
    n = np.arange(nb_pad)[None, :]
    j = np.arange(nj_pad)[:, None]
    lo = np.maximum(n * CMP_STRIDE, j * SLC_BLOCK)
    hi = np.minimum(n * CMP_STRIDE + CMP_BLOCK, (j + 1) * SLC_BLOCK)
    return jnp.asarray(np.maximum(hi - lo, 0).astype(np.float32) / CMP_STRIDE, BF16)


def _nsa_prompt(kv_new, win_new, q, gate, w1k, pek, w2k, *, tq):
    b, seq, _ = kv_new.shape
    assert tq == LANES and seq % KEY_TILE == 0
    nq = NSA_R * tq
    nch = seq // CMP_STRIDE
    n_slc = seq // SLC_BLOCK
    n_tiles = seq // KEY_TILE
    ov = _overlap_t(n_slc, nch)
    kern = functools.partial(_nsa_prompt_kernel, seq=seq, tq=tq)
    return pl.pallas_call(
        kern,
        grid=(b, seq // tq),
        in_specs=[
            pl.BlockSpec((1, seq, LANES), lambda bi, qi: (bi, 0, 0)),
            pl.BlockSpec((1, seq, LANES), lambda bi, qi: (bi, 0, 1)),
            pl.BlockSpec((1, seq, LANES), lambda bi, qi: (bi, 0, 2)),
            pl.BlockSpec((1, seq, LANES), lambda bi, qi: (bi, 0, 3)),
            pl.BlockSpec((1, seq, LANES), lambda bi, qi: (bi, 0, 0)),
            pl.BlockSpec((1, seq, LANES), lambda bi, qi: (bi, 0, 1)),
            pl.BlockSpec((1, tq, NSA_WIDTH), lambda bi, qi: (bi, qi, 0)),
            pl.BlockSpec((1, tq, LANES), lambda bi, qi: (bi, qi, 0)),
            pl.BlockSpec((n_slc, nch), lambda bi, qi: (0, 0)),
            pl.BlockSpec(w1k.shape, lambda bi, qi: (0, 0, 0, 0)),
            pl.BlockSpec(pek.shape, lambda bi, qi: (0, 0, 0, 0)),
            pl.BlockSpec(w2k.shape, lambda bi, qi: (0, 0, 0)),
        ],
        out_specs=pl.BlockSpec((1, tq, NSA_WIDTH), lambda bi, qi: (bi, qi, 0)),
        out_shape=jax.ShapeDtypeStruct((b, seq, NSA_WIDTH), F32),
        scratch_shapes=[
            pltpu.VMEM((nch, LANES), BF16),
            pltpu.VMEM((LANES, nch), BF16),
            pltpu.VMEM((n_tiles, LANES, KEY_TILE), BF16),
            pltpu.VMEM((n_tiles, LANES, KEY_TILE), BF16),
            pltpu.VMEM((NSA_G, n_slc, nq), F32),
        ],
        compiler_params=pltpu.CompilerParams(dimension_semantics=("arbitrary", "arbitrary"), vmem_limit_bytes=VMEM_LIMIT),
        name="nsa_prompt",
    )(kv_new, kv_new, kv_new, kv_new, win_new, win_new, q, gate, ov, w1k, pek, w2k)


def _prep_cmp_weights(cmp_pe, cmp_w1, cmp_w2):
    groups = CMP_STRIDE // CMP_PACK
    w1k = jnp.concatenate([cmp_w1[:, 0], cmp_w1[:, 1]], axis=-1)
    w1k = w1k.reshape(2, groups, CMP_PACK * NSA_DK, 2 * CMP_HIDDEN).astype(BF16)
    pek = cmp_pe.reshape(2, 2, groups, CMP_PACK * NSA_DK).transpose(0, 2, 1, 3)
    pek = jnp.pad(pek, ((0, 0), (0, 0), (0, SUBLANES - 2), (0, 0))).astype(BF16)
    return w1k, pek, cmp_w2.astype(BF16)


def _q_to_lanes(q, tq_real, tq):
    b, t, _ = q.shape
    nqt = t // tq_real
    x = (q * Q_SCALE).reshape(b, nqt, tq_real, NSA_G, NSA_R, NSA_DK)
    x = x.transpose(0, 1, 3, 5, 4, 2)
    x = jnp.pad(x, ((0, 0),) * 5 + ((0, tq - tq_real),)).reshape(b, nqt, NSA_G, NSA_DK, NSA_R * tq)
    z = jnp.zeros_like(x)
    return jnp.stack([jnp.concatenate([x[:, :, 0], z[:, :, 0]], axis=2),
                      jnp.concatenate([z[:, :, 1], x[:, :, 1]], axis=2)], axis=2).astype(BF16)


def _gate_to_lanes(gate, tq_real, tq):
    b, t, _ = gate.shape
    nqt = t // tq_real
    x = gate[..., :COL_GATE].reshape(b, nqt, tq_real, 3, NSA_G, NSA_R).transpose(0, 1, 4, 3, 5, 2)
    x = jnp.pad(x, ((0, 0),) * 5 + ((0, tq - tq_real),)).reshape(b, nqt, NSA_G, 3, NSA_R * tq)
    return jnp.pad(x, ((0, 0), (0, 0), (0, 0), (0, SUBLANES - 3), (0, 0)))


def _o_from_lanes(o_t, tq_real, tq):
    b, nqt = o_t.shape[:2]
    x = o_t.reshape(b, nqt, NSA_G, NSA_DK, NSA_R, tq)[..., :tq_real]
    return x.transpose(0, 1, 5, 2, 4, 3).reshape(b, nqt * tq_real, NSA_WIDTH)


def _nsa_decode_kernel(pt_ref, pool_ref, kvn_ref, wck_ref, wcv_ref, wn_ref, qtz_ref, gate_ref, ov_ref, w1_ref, pe_ref, w2_ref,
                       o_ref, ckb, cvb, skb, svb, sem, wkb, wvb, kc_ref, vct_ref, svt_ref, wvt_ref, selb_ref,
                       *, past, n_new, tq):
    nq = NSA_R * tq
    n_pages = past // PAGE_SIZE
    nch = (past + n_new) // CMP_STRIDE
    assert nch * CMP_STRIDE <= past
    n_slc = -(-(past + n_new) // SLC_BLOCK)
    s_pad = skb.shape[1]
    w_pad = wkb.shape[0]
    big = s_pad // DEC_TILES
    bufs = (ckb, cvb, skb, svb)
    b = pl.program_id(0)
    nb = pl.num_programs(0)
    slot = b % 2

    def page_copies(bb, sl, p):
        page = pt_ref[bb * n_pages + p]
        rows = pl.ds(pl.multiple_of(p * PAGE_SIZE, PAGE_SIZE), PAGE_SIZE)
        return [pltpu.make_async_copy(pool_ref.at[page, :, pl.ds(LANES * c, LANES)], bufs[c].at[sl, rows, :], sem.at[sl])
                for c in range(4)]

    def start_fetch(bb, sl):
        def body(p, carry):
            for cp in page_copies(bb, sl, p):
                cp.start()
            return carry
        lax.fori_loop(0, n_pages, body, 0)

    def wait_fetch(bb, sl):
        def body(p, carry):
            for cp in page_copies(bb, sl, p):
                cp.wait()
            return carry
        lax.fori_loop(0, n_pages, body, 0)

    @pl.when(b == 0)
    def _first():
        start_fetch(0, 0)

    @pl.when(b + 1 < nb)
    def _next():
        start_fetch(b + 1, 1 - slot)

    zpad = jnp.zeros((s_pad - past, LANES), F32)
    skb[slot, past:s_pad, :] = zpad
    svb[slot, past:s_pad, :] = zpad
    skb[slot, past:past + n_new, :] = kvn_ref[0, :, 2 * LANES:3 * LANES]
    svb[slot, past:past + n_new, :] = kvn_ref[0, :, 3 * LANES:4 * LANES]
    nwin = wck_ref.shape[1]
    for buf, cache_ref, lane0 in ((wkb, wck_ref, 0), (wvb, wcv_ref, LANES)):
        buf[0:nwin, :] = cache_ref[0]
        buf[nwin:, :] = jnp.zeros((w_pad - nwin, LANES), F32)
        buf[nwin:nwin + n_new, :] = wn_ref[0, :, lane0:lane0 + LANES]
    for kt in range(w_pad // KEY_TILE):
        cols = slice(kt * KEY_TILE, (kt + 1) * KEY_TILE)
        wvt_ref[:, cols] = wvb[cols, :].T.astype(BF16)

    wait_fetch(b, slot)

    kc = _compress_pair(lambda s: ckb[slot, pl.ds(s, nch, stride=CMP_STRIDE), :], w1_ref, pe_ref, w2_ref, 0, nch)
    vc = _compress_pair(lambda s: cvb[slot, pl.ds(s, nch, stride=CMP_STRIDE), :], w1_ref, pe_ref, w2_ref, 1, nch)
    kc_ref[...] = kc.astype(BF16)
    vct_ref[...] = vc.T.astype(BF16)
    for kt in range(s_pad // KEY_TILE):
        cols = slice(kt * KEY_TILE, (kt + 1) * KEY_TILE)
        svt_ref[:, cols] = svb[slot, cols, :].T.astype(BF16)

    qpos = past + lax.broadcasted_iota(jnp.int32, (1, nq), 1) % tq
    qtz = [qtz_ref[0, 0, g] for g in range(NSA_G)]
    o_cmp, scores = [], []
    for g in range(NSA_G):
        oc, sc = _cmp_branch(g, qtz[g], qpos, tq, nq, kc_ref, vct_ref, ov_ref, nch - 1)
        o_cmp.append(oc)
        scores.append(sc)
    _select_blocks(scores, n_slc, tq, nq, selb_ref)

    blocks_per_big = big // SLC_BLOCK
    carry = _attn_init(nq)
    for i in range(DEC_TILES):
        rows = slice(i * big, (i + 1) * big)
        k = skb[slot, rows, :].astype(BF16)
        biases = []
        for g in range(NSA_G):
            bg = _block_rows(selb_ref[g, i * blocks_per_big:(i + 1) * blocks_per_big, :], nq)
            if (i + 1) * big > past:
                bg = bg + _causal_bias(i * big + lax.broadcasted_iota(jnp.int32, (big, nq), 0), qpos)
            biases.append(bg)
        vts = [svt_ref[NSA_DK * g:NSA_DK * (g + 1), rows] for g in range(NSA_G)]
        carry = _attn_step(carry, k, qtz, vts, biases)
    o_slc = _attn_finish(carry)

    wb = _window_bias(past - nwin + lax.broadcasted_iota(jnp.int32, (w_pad, nq), 0), qpos)
    vts = [wvt_ref[NSA_DK * g:NSA_DK * (g + 1), :] for g in range(NSA_G)]
    o_win = _attn_finish(_attn_step(_attn_init(nq), wkb[...].astype(BF16), qtz, vts, [wb, wb]))

    for g in range(NSA_G):
        o_ref[0, 0, g] = _gated_sum(gate_ref[0, 0, g], o_cmp[g], o_slc[g], o_win[g])


def _nsa_decode(pool, page_table, kv_new, win_cache, win_new, qtz, gate_t, w1k, pek, w2k, *, tq):
    b, n_pages = page_table.shape
    n_new = kv_new.shape[1]
    nwin = win_cache.shape[1]
    past = n_pages * PAGE_SIZE
    nq = NSA_R * tq
    nch = (past + n_new) // CMP_STRIDE
    step_keys = DEC_TILES * KEY_TILE
    s_pad = -(-(past + n_new) // step_keys) * step_keys
    w_pad = -(-(nwin + n_new) // KEY_TILE) * KEY_TILE
    nj_pad = -(-(s_pad // SLC_BLOCK) // 16) * 16
    assert nwin % KEY_TILE == 0 and n_new <= LANES and (s_pad // DEC_TILES) % LANES == 0
    ov = _overlap_t(nj_pad, nch)
    kern = functools.partial(_nsa_decode_kernel, past=past, n_new=n_new, tq=tq)
    grid_spec = pltpu.PrefetchScalarGridSpec(
        num_scalar_prefetch=1,
        grid=(b,),
        in_specs=[
            pl.BlockSpec(memory_space=pl.ANY),
            pl.BlockSpec((1, n_new, COL_KV), lambda bi, pt: (bi, 0, 0)),
            pl.BlockSpec((1, nwin, LANES), lambda bi, pt: (bi, 0, 0)),
            pl.BlockSpec((1, nwin, LANES), lambda bi, pt: (bi, 0, 1)),
            pl.BlockSpec((1, n_new, COL_WIN), lambda bi, pt: (bi, 0, 0)),
            pl.BlockSpec((1, 1, NSA_G, LANES, nq), lambda bi, pt: (bi, 0, 0, 0, 0)),
            pl.BlockSpec((1, 1, NSA_G, SUBLANES, nq), lambda bi, pt: (bi, 0, 0, 0, 0)),
            pl.BlockSpec((nj_pad, nch), lambda bi, pt: (0, 0)),
            pl.BlockSpec(w1k.shape, lambda bi, pt: (0, 0, 0, 0)),
            pl.BlockSpec(pek.shape, lambda bi, pt: (0, 0, 0, 0)),
            pl.BlockSpec(w2k.shape, lambda bi, pt: (0, 0, 0)),
        ],
        out_specs=pl.BlockSpec((1, 1, NSA_G, NSA_DK, nq), lambda bi, pt: (bi, 0, 0, 0, 0)),
        scratch_shapes=[
            pltpu.VMEM((2, past, LANES), F32),
            pltpu.VMEM((2, past, LANES), F32),
            pltpu.VMEM((2, s_pad, LANES), F32),
            pltpu.VMEM((2, s_pad, LANES), F32),
            pltpu.SemaphoreType.DMA((2,)),
            pltpu.VMEM((w_pad, LANES), F32),
            pltpu.VMEM((w_pad, LANES), F32),
            pltpu.VMEM((nch, LANES), BF16),
            pltpu.VMEM((LANES, nch), BF16),
            pltpu.VMEM((LANES, s_pad), BF16),
            pltpu.VMEM((LANES, w_pad), BF16),
            pltpu.VMEM((NSA_G, nj_pad, nq), F32),
        ],
    )
    return pl.pallas_call(
        kern,
        grid_spec=grid_spec,
        out_shape=jax.ShapeDtypeStruct((b, 1, NSA_G, NSA_DK, nq), F32),
        compiler_params=pltpu.CompilerParams(dimension_semantics=("arbitrary",), vmem_limit_bytes=VMEM_LIMIT),
        name="nsa_decode",
    )(page_table.reshape(-1), pool, kv_new, win_cache, win_cache, win_new, qtz, gate_t, ov, w1k, pek, w2k)


FF_TILE = 256


def _rmsnorm(x, w):
    return (x * lax.rsqrt(jnp.mean(x * x, axis=-1, keepdims=True) + EPS)) * w


def _gelu_tanh(c):
    return c * (0.5 * (1.0 + jnp.tanh(0.7978845608028654 * (c + 0.044715 * (c * c * c)))))


def _ffn_kernel(x_ref, ohg_ref, onsa_ref, past_ref, wout_ref, n2_ref, wg_ref, wv_ref, cw_ref, cb_ref, wd_ref, nf_ref,
                y_ref, conv_ref, carry_ref, *, tm):
    ti = pl.program_id(1)

    @pl.when(ti == 0)
    def _init():
        carry_ref[0:CONV_W - 1, :] = past_ref[0]

    x1 = (x_ref[0]
          + jnp.dot(ohg_ref[0].astype(BF16), wout_ref[0:HG_WIDTH, :], preferred_element_type=F32)
          + jnp.dot(onsa_ref[0].astype(BF16), wout_ref[HG_WIDTH:, :], preferred_element_type=F32))
    h = _rmsnorm(x1, n2_ref[...]).astype(BF16)
    row = lax.broadcasted_iota(jnp.int32, (tm, FF_TILE), 0)
    acc = jnp.zeros((tm, D_MODEL), F32)
    for j in range(D_FF // FF_TILE):
        cols = slice(j * FF_TILE, (j + 1) * FF_TILE)
        u = jnp.dot(h, wg_ref[:, cols], preferred_element_type=F32)
        val = jnp.dot(h, wv_ref[:, cols], preferred_element_type=F32)
        c0 = carry_ref[0:1, cols]
        c1 = carry_ref[1:2, cols]
        up1 = jnp.where(row == 0, c1, pltpu.roll(u, 1, axis=0))
        up2 = jnp.where(row == 0, c0, jnp.where(row == 1, c1, pltpu.roll(u, 2, axis=0)))
        c = cb_ref[:, cols] + up2 * cw_ref[0:1, cols] + up1 * cw_ref[1:2, cols] + u * cw_ref[2:3, cols]
        tail = u[tm - (CONV_W - 1):tm, :]
        carry_ref[0:CONV_W - 1, cols] = tail
        conv_ref[0, :, cols] = tail
        acc = acc + jnp.dot((_gelu_tanh(c) * val).astype(BF16), wd_ref[cols, :], preferred_element_type=F32)
    y_ref[0] = _rmsnorm(x1 + acc, nf_ref[...])


def _ffn(x, o_hg, o_nsa, conv_past, wts, *, tm):
    b, t, _ = x.shape
    assert t % tm == 0 and D_FF % FF_TILE == 0
    wout, n2, wg, wv, cw, cb, wd, nf = wts
    const2 = lambda bi, ti: (0, 0)
    one = pl.Buffered(1)
    return pl.pallas_call(
        functools.partial(_ffn_kernel, tm=tm),
        grid=(b, t // tm),
        in_specs=[
            pl.BlockSpec((1, tm, D_MODEL), lambda bi, ti: (bi, ti, 0)),
            pl.BlockSpec((1, tm, HG_WIDTH), lambda bi, ti: (bi, ti, 0)),
            pl.BlockSpec((1, tm, NSA_WIDTH), lambda bi, ti: (bi, ti, 0)),
            pl.BlockSpec((1, CONV_W - 1, D_FF), lambda bi, ti: (bi, 0, 0)),
            pl.BlockSpec(wout.shape, const2, pipeline_mode=one),
            pl.BlockSpec(n2.shape, const2),
            pl.BlockSpec(wg.shape, const2, pipeline_mode=one),
            pl.BlockSpec(wv.shape, const2, pipeline_mode=one),
            pl.BlockSpec(cw.shape, const2),
            pl.BlockSpec(cb.shape, const2),
            pl.BlockSpec(wd.shape, const2, pipeline_mode=one),
            pl.BlockSpec(nf.shape, const2),
        ],
        out_specs=[
            pl.BlockSpec((1, tm, D_MODEL), lambda bi, ti: (bi, ti, 0)),
            pl.BlockSpec((1, CONV_W - 1, D_FF), lambda bi, ti: (bi, 0, 0)),
        ],
        out_shape=[jax.ShapeDtypeStruct((b, t, D_MODEL), F32), jax.ShapeDtypeStruct((b, CONV_W - 1, D_FF), F32)],
        scratch_shapes=[pltpu.VMEM((SUBLANES, D_FF), F32)],
        compiler_params=pltpu.CompilerParams(dimension_semantics=("arbitrary", "arbitrary"), vmem_limit_bytes=VMEM_LIMIT),
        name="ffn",
    )(x, o_hg, o_nsa, conv_past, wout, n2, wg, wv, cw, cb, wd, nf)


def _ffn_steps_kernel(x_ref, om_ref, past_ref, wout_ref, n2_ref, wg_ref, wv_ref, cw_ref, cb_ref, wd_ref, nf_ref,
                      y_ref, conv_ref, *, n_steps, nb):
    x1 = x_ref[...] + jnp.dot(om_ref[...].astype(BF16), wout_ref[...], preferred_element_type=F32)
    h = _rmsnorm(x1, n2_ref[...]).astype(BF16)
    acc = jnp.zeros((n_steps * nb, D_MODEL), F32)
    for j in range(D_FF // FF_TILE):
        cols = slice(j * FF_TILE, (j + 1) * FF_TILE)
        u = jnp.dot(h, wg_ref[:, cols], preferred_element_type=F32)
        val = jnp.dot(h, wv_ref[:, cols], preferred_element_type=F32)
        taps = [past_ref[k, :, cols] for k in range(CONV_W - 1)] + [u[t * nb:(t + 1) * nb, :] for t in range(n_steps)]
        cs = []
        for t in range(n_steps):
            c = cb_ref[:, cols]
            for k in range(CONV_W):
                c = c + taps[t + k] * cw_ref[k:k + 1, cols]
            cs.append(c)
        for k in range(CONV_W - 1):
            conv_ref[k, :, cols] = taps[n_steps + k]
        c_all = jnp.concatenate(cs, axis=0)
        acc = acc + jnp.dot((_gelu_tanh(c_all) * val).astype(BF16), wd_ref[cols, :], preferred_element_type=F32)
    y_ref[...] = _rmsnorm(x1 + acc, nf_ref[...])


def _ffn_steps(x_t, omix_t, past_t, wts, *, n_steps, nb):
    return pl.pallas_call(
        functools.partial(_ffn_steps_kernel, n_steps=n_steps, nb=nb),
        out_shape=[jax.ShapeDtypeStruct((n_steps * nb, D_MODEL), F32), jax.ShapeDtypeStruct((CONV_W - 1, nb, D_FF), F32)],
        compiler_params=pltpu.CompilerParams(vmem_limit_bytes=VMEM_LIMIT),
        name="ffn_steps",
    )(x_t, omix_t, past_t, *wts)


def kernel(x_prompt, x_sample, cache_nsa_kv, cache_win_kv, state_hgrn, state_ffn_conv, page_table, norm1, w_in, hg_lb_logits, hg_norm, cmp_pe, cmp_w1, cmp_w2, w_out, norm2, w_gate, w_val, conv_w, conv_b, w_down, norm_f):
    B, T, _ = x_prompt.shape
    Bd, Td, _ = x_sample.shape
    depth = norm1.shape[0]
    assert depth == 1
    l = 0
    w_in_bf = jnp.pad(w_in[l], ((0, 0), (0, _C_GATE - D_IN))).astype(BF16)
    w1k, pek, w2k = _prep_cmp_weights(cmp_pe[l], cmp_w1[l], cmp_w2[l])
    ffn_w = (w_out[l].astype(BF16), norm2[l].reshape(1, D_MODEL), w_gate[l].astype(BF16), w_val[l].astype(BF16),
             conv_w[l], conv_b[l].reshape(1, D_FF), w_down[l].astype(BF16), norm_f.reshape(1, D_MODEL))

    z_hg, q_p, kv_p, win_p, gate_p = _in_proj(x_prompt.reshape(B * T, D_MODEL), norm1[l], w_in_bf)
    o_hg_p, s_p = _hgrn(z_hg.reshape(B, T, _C_HG), hg_lb_logits, hg_norm[l],
                        jnp.zeros((B, HG_HEADS, HG_DK, HG_DK), F32), chunk=HG_CHUNK, rows_per_step=512, t_valid=None)
    kv_p = kv_p.reshape(B, T, COL_KV)
    win_p = win_p.reshape(B, T, COL_WIN)
    o_nsa_p = _nsa_prompt(kv_p, win_p, q_p.reshape(B, T, NSA_WIDTH), gate_p.reshape(B, T, LANES), w1k, pek, w2k, tq=LANES)
    y_p, conv_p = _ffn(x_prompt, o_hg_p, o_nsa_p, jnp.zeros((B, CONV_W - 1, D_FF), F32), ffn_w, tm=512)

    z_hg, q_s, kv_s, win_s, gate_s = _in_proj(x_sample.reshape(Bd * Td, D_MODEL), norm1[l], w_in_bf)
    t_pad = HG_SUB
    z_pad = jnp.pad(z_hg.reshape(Bd, Td, _C_HG), ((0, 0), (0, t_pad - Td), (0, 0)))
    o_hg_s, s_s = _hgrn(z_pad, hg_lb_logits, hg_norm[l], state_hgrn[l], chunk=t_pad, rows_per_step=t_pad, t_valid=Td)
    o_hg_s = o_hg_s[:, :Td]
    tqs = LANES // NSA_R
    kv_s = kv_s.reshape(Bd, Td, COL_KV)
    win_s = win_s.reshape(Bd, Td, COL_WIN)
    n_pool = cache_nsa_kv.shape[1]
    nwin = cache_win_kv.shape[2]
    o_t = _nsa_decode(cache_nsa_kv[l].reshape(n_pool, PAGE_SIZE, COL_KV), page_table, kv_s,
                      cache_win_kv[l].reshape(Bd, nwin, COL_WIN), win_s,
                      _q_to_lanes(q_s.reshape(Bd, Td, NSA_WIDTH), Td, tqs),
                      _gate_to_lanes(gate_s.reshape(Bd, Td, LANES), Td, tqs), w1k, pek, w2k, tq=tqs)
    o_nsa_s = _o_from_lanes(o_t, Td, tqs)
    omix_t = jnp.concatenate([o_hg_s, o_nsa_s], axis=-1).transpose(1, 0, 2).reshape(Td * Bd, D_MODEL)
    y_t, conv_t = _ffn_steps(x_sample.transpose(1, 0, 2).reshape(Td * Bd, D_MODEL), omix_t,
                             state_ffn_conv[l].transpose(1, 0, 2), ffn_w, n_steps=Td, nb=Bd)
    y_s = y_t.reshape(Td, Bd, D_MODEL).transpose(1, 0, 2)
    conv_s = conv_t.transpose(1, 0, 2)

    win_keep_p = min(WINDOW, T)
    win_all_s = jnp.concatenate([cache_win_kv[l].reshape(Bd, nwin, COL_WIN), win_s], axis=1)[:, -nwin:]
    return (y_p, y_s,
            kv_p.reshape(1, B, T, 4, NSA_G, NSA_DK), kv_s.reshape(1, Bd, Td, 4, NSA_G, NSA_DK),
            win_p[:, T - win_keep_p:].reshape(1, B, win_keep_p, 2, NSA_G, NSA_DK),
            win_all_s.reshape(1, Bd, nwin, 2, NSA_G, NSA_DK),
            s_p[None], s_s[None], conv_p[None], conv_s[None])
```

```python
import functools

import numpy as np
import jax
import jax.numpy as jnp
from jax import lax
from jax.experimental import pallas as pl
from jax.experimental.pallas import tpu as pltpu

F32 = jnp.float32
BF16 = jnp.bfloat16

D_MODEL = 1024
PAGE_SIZE = 128
HG_WIDTH = 512
HG_HEADS = 4
HG_DK = 128
HG_CHUNK = 64
NSA_WIDTH = 512
NSA_HEADS = 8
NSA_DK = 64
NSA_G = 2
NSA_R = 4
CMP_STRIDE = 16
CMP_BLOCK = 32
CMP_HIDDEN = 256
SLC_BLOCK = 64
SLC_TOPK = 16
WINDOW = 512
D_FF = 2816
CONV_W = 3
EPS = 1e-6
COL_KV = 512
COL_WIN = 256
COL_GATE = 24

LANES = 128
SUBLANES = 8
HG_SUB = 16
HG_UNROLL = 8
HG_MASKED_EXP = -1e30
VMEM_LIMIT = 56 * 1024 * 1024

_C_HG = 4 * HG_WIDTH
_C_Q = _C_HG + NSA_WIDTH
_C_KV = _C_Q + COL_KV
_C_WIN = _C_KV + COL_WIN
_C_GATE = _C_WIN + LANES
D_IN = _C_WIN + COL_GATE


def _sigmoid(x):
    return 1.0 / (1.0 + jnp.exp(-x))


def _split3_bf16(x):
    hi = x.astype(BF16)
    r1 = x - hi.astype(F32)
    mid = r1.astype(BF16)
    lo = (r1 - mid.astype(F32)).astype(BF16)
    return hi, mid, lo


def _in_proj_kernel(x_ref, g_ref, w_ref, hg_ref, q_ref, kv_ref, win_ref, gate_ref):
    x = x_ref[...]
    ms = jnp.mean(x * x, axis=-1, keepdims=True)
    h = ((x * lax.rsqrt(ms + EPS)) * g_ref[...]).astype(BF16)
    hg_ref[...] = jnp.dot(h, w_ref[:, 0:_C_HG], preferred_element_type=F32)
    q_ref[...] = jnp.dot(h, w_ref[:, _C_HG:_C_Q], preferred_element_type=F32)
    kv_ref[...] = jnp.dot(h, w_ref[:, _C_Q:_C_KV], preferred_element_type=F32)
    win_ref[...] = jnp.dot(h, w_ref[:, _C_KV:_C_WIN], preferred_element_type=F32)
    gate_ref[...] = jnp.dot(h, w_ref[:, _C_WIN:_C_GATE], preferred_element_type=F32)


def _in_proj(x2d, norm_w, w_in_bf):
    n = x2d.shape[0]
    tm = 512
    assert n % tm == 0
    widths = (_C_HG, NSA_WIDTH, COL_KV, COL_WIN, LANES)
    return pl.pallas_call(
        _in_proj_kernel,
        grid=(n // tm,),
        in_specs=[
            pl.BlockSpec((tm, D_MODEL), lambda i: (i, 0)),
            pl.BlockSpec((1, D_MODEL), lambda i: (0, 0)),
            pl.BlockSpec((D_MODEL, _C_GATE), lambda i: (0, 0)),
        ],
        out_specs=[pl.BlockSpec((tm, w), lambda i: (i, 0)) for w in widths],
        out_shape=[jax.ShapeDtypeStruct((n, w), F32) for w in widths],
        compiler_params=pltpu.CompilerParams(dimension_semantics=("arbitrary",), vmem_limit_bytes=VMEM_LIMIT),
        name="in_proj",
    )(x2d, norm_w.reshape(1, D_MODEL), w_in_bf)


def _hgrn_kernel(q_ref, f_ref, i_ref, g_ref, lb_ref, nw_ref, s0_ref, e_ref, o_ref, sfin_ref, st_ref,
                 *, chunk, rows_per_step, t_valid, layer):
    C = chunk
    nsb = C // HG_SUB
    ti = pl.program_id(2)

    @pl.when(ti == 0)
    def _init():
        st_ref[...] = s0_ref[0, 0].T

    lg = lb_ref[...]
    le = jnp.exp(lg - jnp.max(lg, axis=0, keepdims=True))
    lb = jnp.sum(le[0:layer + 1, :], axis=0, keepdims=True) / jnp.sum(le, axis=0, keepdims=True)
    one_m_lb = 1.0 - lb
    nw = nw_ref[...]
    row_i = lax.broadcasted_iota(jnp.int32, (C, C), 0)
    col_i = lax.broadcasted_iota(jnp.int32, (C, C), 1)
    tri_bf = jnp.where(col_i <= row_i, 1.0, 0.0).astype(BF16)
    diag_mask = ((row_i // HG_SUB) == (col_i // HG_SUB)) & (col_i <= row_i)
    off_mask = (col_i // HG_SUB) < (row_i // HG_SUB)
    trel = lax.broadcasted_iota(jnp.int32, (C, HG_DK), 0) % HG_SUB
    nt_dims = (((1,), (1,)), ((), ()))
    tn_dims = (((0,), (0,)), ((), ()))

    def do_chunk(c, carry):
        r0 = pl.multiple_of(c * C, C)
        rows = pl.ds(r0, C)
        fr = f_ref[0, rows, :]
        qr = q_ref[0, rows, :]
        v = i_ref[0, rows, :]
        gr = g_ref[0, rows, :]
        f = lb + one_m_lb * _sigmoid(fr)
        kk = one_m_lb * _sigmoid(-fr)
        logf = jnp.log2(f)
        if t_valid is not None:
            tok = ti * rows_per_step + r0 + lax.broadcasted_iota(jnp.int32, (C, HG_DK), 0)
            ok = tok < t_valid
            logf = jnp.where(ok, logf, 0.0)
            kk = jnp.where(ok, kk, 0.0)
        q = qr * _sigmoid(qr)
        hi, mid, lo = _split3_bf16(logf)
        a = (jnp.dot(tri_bf, hi, preferred_element_type=F32)
             + jnp.dot(tri_bf, mid, preferred_element_type=F32)
             + jnp.dot(tri_bf, lo, preferred_element_type=F32))
        st = st_ref[...]
        vb = v.astype(BF16)
        o = lax.dot_general((q * jnp.exp2(a)).astype(BF16), st.astype(BF16), nt_dims, preferred_element_type=F32)

        if nsb > 1:
            parts = [jnp.zeros((HG_SUB, C), F32)]
            for i in range(1, nsb):
                bi = a[HG_SUB * i - 1:HG_SUB * i, :]
                qi = q[HG_SUB * i:HG_SUB * (i + 1), :] * jnp.exp2(a[HG_SUB * i:HG_SUB * (i + 1), :] - bi)
                ki = kk * jnp.exp2(jnp.minimum(bi - a, 0.0))
                parts.append(lax.dot_general(qi.astype(BF16), ki.astype(BF16), nt_dims, preferred_element_type=F32))
            att = jnp.where(off_mask, jnp.concatenate(parts, axis=0), 0.0)
        else:
            att = jnp.zeros((C, C), F32)

        a3 = a.reshape(nsb, HG_SUB, HG_DK)
        k3 = kk.reshape(nsb, HG_SUB, HG_DK)
        zs = []
        for j in range(HG_SUB):
            aj = jnp.broadcast_to(a3[:, j:j + 1, :], (nsb, HG_SUB, HG_DK)).reshape(C, HG_DK)
            kj = jnp.broadcast_to(k3[:, j:j + 1, :], (nsb, HG_SUB, HG_DK)).reshape(C, HG_DK)
            expo = a - aj if j == 0 else jnp.where(trel >= j, a - aj, HG_MASKED_EXP)
            zs.append((q * kj * jnp.exp2(expo)).astype(BF16))
        zcat = jnp.concatenate(zs, axis=1)
        att_d = jnp.dot(zcat, e_ref[...], preferred_element_type=F32)
        att = att + jnp.where(diag_mask, att_d, 0.0)
        o = o + jnp.dot(att.astype(BF16), vb, preferred_element_type=F32)

        a_end = a[C - 1:C, :]
        kdec = kk * jnp.exp2(a_end - a)
        st_ref[...] = st * jnp.exp2(a_end) + lax.dot_general(vb, kdec.astype(BF16), tn_dims, preferred_element_type=F32)

        on = (o * lax.rsqrt(jnp.mean(o * o, axis=-1, keepdims=True) + EPS)) * nw
        o_ref[0, rows, :] = on * (gr * _sigmoid(gr))
        return carry

    n_chunks = rows_per_step // C
    lax.fori_loop(0, n_chunks, do_chunk, 0, unroll=min(n_chunks, HG_UNROLL))

    @pl.when(ti == pl.num_programs(2) - 1)
    def _fin():
        sfin_ref[0, 0] = st_ref[...].T


def _hgrn(z_hg, lb_logits, norm_w, s0, *, chunk, rows_per_step, t_valid, layer=0):
    b, t, _ = z_hg.shape
    n_lb = lb_logits.shape[0]
    assert t % rows_per_step == 0 and rows_per_step % chunk == 0 and chunk % HG_SUB == 0
    nt = t // rows_per_step
    sub = np.arange(chunk) % HG_SUB
    e_mat = jnp.asarray((np.repeat(np.arange(HG_SUB), HG_DK)[:, None] == sub[None, :]).astype(np.float32), BF16)

    def zspec(k):
        return pl.BlockSpec((1, rows_per_step, HG_DK), lambda bi, h, ti, k=k: (bi, ti, k * HG_HEADS + h))

    kern = functools.partial(_hgrn_kernel, chunk=chunk, rows_per_step=rows_per_step, t_valid=t_valid, layer=layer)
    return pl.pallas_call(
        kern,
        grid=(b, HG_HEADS, nt),
        in_specs=[
            zspec(0), zspec(1), zspec(2), zspec(3),
            pl.BlockSpec((n_lb, HG_DK), lambda bi, h, ti: (0, h)),
            pl.BlockSpec((1, HG_DK), lambda bi, h, ti: (0, h)),
            pl.BlockSpec((1, 1, HG_DK, HG_DK), lambda bi, h, ti: (bi, h, 0, 0)),
            pl.BlockSpec((HG_SUB * HG_DK, chunk), lambda bi, h, ti: (0, 0)),
        ],
        out_specs=[
            pl.BlockSpec((1, rows_per_step, HG_DK), lambda bi, h, ti: (bi, ti, h)),
            pl.BlockSpec((1, 1, HG_DK, HG_DK), lambda bi, h, ti: (bi, h, 0, 0)),
        ],
        out_shape=[jax.ShapeDtypeStruct((b, t, HG_WIDTH), F32), jax.ShapeDtypeStruct((b, HG_HEADS, HG_DK, HG_DK), F32)],
        scratch_shapes=[pltpu.VMEM((HG_DK, HG_DK), F32)],
        compiler_params=pltpu.CompilerParams(dimension_semantics=("arbitrary", "arbitrary", "arbitrary"),
                                             vmem_limit_bytes=VMEM_LIMIT),
        name="hgrn2",
    )(z_hg, z_hg, z_hg, z_hg, lb_logits, norm_w.reshape(1, HG_WIDTH), s0, e_mat)


KEY_TILE = 256
CMP_PACK = 4
DEC_TILES = 3
LOG2E = 1.4426950408889634
Q_SCALE = LOG2E * NSA_DK ** -0.5
NEG_INF = float("-inf")


def _compress_pair(load_xs, w1_ref, pe_ref, w2_ref, slot, nch):
    r = jnp.zeros((SUBLANES, 2 * CMP_HIDDEN), F32)
    for u in range(CMP_STRIDE // CMP_PACK):
        r = r + jnp.dot(pe_ref[slot, u], w1_ref[slot, u], preferred_element_type=F32)
    bias = r[0:1, 0:CMP_HIDDEN] + r[1:2, CMP_HIDDEN:2 * CMP_HIDDEN]
    outs = []
    for g in range(NSA_G):
        acc = jnp.zeros((nch, 2 * CMP_HIDDEN), F32)
        for u in range(CMP_STRIDE // CMP_PACK):
            xg = jnp.concatenate([load_xs(CMP_PACK * u + i)[:, NSA_DK * g:NSA_DK * (g + 1)] for i in range(CMP_PACK)],
                                 axis=1).astype(BF16)
            acc = acc + jnp.dot(xg, w1_ref[slot, u], preferred_element_type=F32)
        h = acc[:, 0:CMP_HIDDEN] + pltpu.roll(acc[:, CMP_HIDDEN:2 * CMP_HIDDEN], nch - 1, axis=0) + bias
        hs = (h * _sigmoid(h)).astype(BF16)
        outs.append(jnp.dot(hs, w2_ref[slot], preferred_element_type=F32))
    return jnp.concatenate(outs, axis=1)


def _rank_bias(score, n_slc):
    nj, w = score.shape
    jrow = lax.broadcasted_iota(jnp.int32, (SUBLANES, w), 0)
    slabs = [score[SUBLANES * v:SUBLANES * (v + 1), :] for v in range(nj // SUBLANES)]
    ranks = [jnp.zeros((SUBLANES, w), F32) for _ in slabs]
    for jp in range(n_slc):
        row = score[jp:jp + 1, :]
        for v, slab in enumerate(slabs):
            lo = SUBLANES * v
            if lo > jp:
                beats = row >= slab
            elif lo + SUBLANES - 1 <= jp:
                beats = row > slab
            else:
                beats = (row > slab) | ((row == slab) & (jrow > jp - lo))
            ranks[v] = ranks[v] + jnp.where(beats, 1.0, 0.0)
    rank = jnp.concatenate(ranks, axis=0)
    return jnp.where((rank < float(SLC_TOPK)) & (score > NEG_INF), 0.0, NEG_INF)


def _cmp_branch(g, qtz, qpos, tq, nq, kc_ref, vct_ref, ov_ref, n_blk):
    nb_pad = kc_ref.shape[0]
    nj_pad = ov_ref.shape[0]
    s = jnp.dot(kc_ref[...], qtz, preferred_element_type=F32)
    n_idx = lax.broadcasted_iota(jnp.int32, (nb_pad, nq), 0)
    cmask = (n_idx * CMP_STRIDE + (CMP_BLOCK - 1) <= qpos) & (n_idx < n_blk)
    s = jnp.where(cmask, s, NEG_INF)
    m = jnp.max(s, axis=0, keepdims=True)
    m = jnp.where(m == NEG_INF, 0.0, m)
    e = jnp.exp2(s - m)
    d = jnp.sum(e, axis=0, keepdims=True)
    p = e / jnp.where(d > 0.0, d, 1.0)
    o_cmp = jnp.dot(vct_ref[NSA_DK * g:NSA_DK * (g + 1), :], p.astype(BF16), preferred_element_type=F32)

    if tq % LANES == 0:
        ps = p[:, 0:tq]
        for r in range(1, NSA_R):
            ps = ps + p[:, r * tq:(r + 1) * tq]
        qpos_w = qpos[:, 0:tq]
    else:
        assert nq == LANES and NSA_R * tq == LANES
        ps = p
        for r in range(1, NSA_R):
            ps = ps + pltpu.roll(p, r * tq, axis=1)
        qpos_w = qpos
    w = ps.shape[1]
    hi, mid, lo = _split3_bf16(ps)
    ov = ov_ref[...]
    imp = (jnp.dot(ov, hi, preferred_element_type=F32) + jnp.dot(ov, mid, preferred_element_type=F32)
           + jnp.dot(ov, lo, preferred_element_type=F32))
    j_idx = lax.broadcasted_iota(jnp.int32, (nj_pad, w), 0)
    cur = qpos_w // SLC_BLOCK
    forced = (j_idx == 0) | (j_idx == cur) | (j_idx == cur - 1)
    score = jnp.where(forced, jnp.inf, jnp.where(j_idx <= cur, imp, NEG_INF))
    return o_cmp, score


def _select_blocks(scores, n_slc, tq, nq, selb_ref):
    w = scores[0].shape[1]
    bias = _rank_bias(jnp.concatenate(scores, axis=1), n_slc)
    for g in range(NSA_G):
        bg = bias[:, g * w:(g + 1) * w]
        selb_ref[g] = bg if w == nq else jnp.concatenate([bg] * (nq // w), axis=1)


def _block_rows(rows, nq):
    n = rows.shape[0]
    return jnp.broadcast_to(rows[:, None, :], (n, SLC_BLOCK, nq)).reshape(n * SLC_BLOCK, nq)


def _attn_step(carry, k, qtz, vts, biases):
    out = []
    for g in range(NSA_G):
        m, l, acc = carry[g]
        s = jnp.dot(k, qtz[g], preferred_element_type=F32) + biases[g]
        m_new = jnp.maximum(m, jnp.max(s, axis=0, keepdims=True))
        m_safe = jnp.where(m_new == NEG_INF, 0.0, m_new)
        alpha = jnp.exp2(m - m_safe)
        p = jnp.exp2(s - m_safe)
        l = l * alpha + jnp.sum(p, axis=0, keepdims=True)
        acc = acc * alpha + jnp.dot(vts[g], p.astype(BF16), preferred_element_type=F32)
        out.append((m_new, l, acc))
    return tuple(out)


def _attn_init(nq):
    return tuple((jnp.full((1, nq), NEG_INF, F32), jnp.zeros((1, nq), F32), jnp.zeros((NSA_DK, nq), F32))
                 for _ in range(NSA_G))


def _attn_finish(carry):
    return [acc / jnp.where(l > 0.0, l, 1.0) for (_, l, acc) in carry]


def _causal_bias(kpos, qpos):
    return jnp.where(kpos <= qpos, 0.0, NEG_INF)


def _window_bias(kpos, qpos):
    d = kpos - qpos
    return jnp.where((d <= 0) & (d > -WINDOW), 0.0, NEG_INF)


def _gated_sum(gate_raw, o_cmp, o_slc, o_win):
    gt = _sigmoid(gate_raw)
    return gt[0:1, :] * o_cmp + gt[1:2, :] * o_slc + gt[2:3, :] * o_win


def _nsa_prompt_kernel(ck_ref, cv_ref, sk_ref, sv_ref, wk_ref, wv_ref, q_ref, gate_ref, ov_ref, w1_ref, pe_ref, w2_ref,
                       o_ref, kc_ref, vct_ref, svt_ref, wvt_ref, selb_ref, *, seq, tq):
    nq = NSA_R * tq
    nch = seq // CMP_STRIDE
    n_tiles = seq // KEY_TILE
    n_slc = seq // SLC_BLOCK
    blocks_per_tile = KEY_TILE // SLC_BLOCK
    assert 2 * blocks_per_tile == SUBLANES
    qt = pl.program_id(1)

    @pl.when(qt == 0)
    def _prep():
        kc = _compress_pair(lambda s: ck_ref[0, pl.ds(s, nch, stride=CMP_STRIDE), :], w1_ref, pe_ref, w2_ref, 0, nch)
        vc = _compress_pair(lambda s: cv_ref[0, pl.ds(s, nch, stride=CMP_STRIDE), :], w1_ref, pe_ref, w2_ref, 1, nch)
        kc_ref[...] = kc.astype(BF16)
        vct_ref[...] = vc.T.astype(BF16)
        for kt in range(n_tiles):
            rows = slice(kt * KEY_TILE, (kt + 1) * KEY_TILE)
            svt_ref[kt] = sv_ref[0, rows, :].T.astype(BF16)
            wvt_ref[kt] = wv_ref[0, rows, :].T.astype(BF16)

    t0 = qt * tq
    qpos = t0 + lax.broadcasted_iota(jnp.int32, (1, nq), 1) % tq
    hi_tile = (t0 + tq - 1) // KEY_TILE + 1
    win_lo = jnp.maximum(t0 - (WINDOW - 1), 0) // KEY_TILE
    krow = lax.broadcasted_iota(jnp.int32, (KEY_TILE, nq), 0)

    gate_t = gate_ref[0].T
    zeros_half = jnp.zeros((NSA_DK, nq), F32)
    qtz, gates = [], []
    for g in range(NSA_G):
        qg_t = q_ref[0, :, g * NSA_R * NSA_DK:(g + 1) * NSA_R * NSA_DK].T
        q_t = jnp.concatenate([qg_t[r * NSA_DK:(r + 1) * NSA_DK, :] for r in range(NSA_R)], axis=1) * Q_SCALE
        qtz.append(jnp.concatenate([q_t, zeros_half] if g == 0 else [zeros_half, q_t], axis=0).astype(BF16))
        gates.append(jnp.concatenate(
            [jnp.concatenate([gate_t[k * NSA_HEADS + g * NSA_R + r:k * NSA_HEADS + g * NSA_R + r + 1, :]
                              for r in range(NSA_R)], axis=1) for k in range(3)], axis=0))

    o_cmp, scores = [], []
    for g in range(NSA_G):
        oc, sc = _cmp_branch(g, qtz[g], qpos, tq, nq, kc_ref, vct_ref, ov_ref, nch - 1)
        o_cmp.append(oc)
        scores.append(sc)
    _select_blocks(scores, n_slc, tq, nq, selb_ref)

    def slc_step(kt, carry, causal):
        rows = pl.ds(pl.multiple_of(kt * KEY_TILE, KEY_TILE), KEY_TILE)
        k = sk_ref[0, rows, :].astype(BF16)
        biases = []
        for g in range(NSA_G):
            rows8 = selb_ref[g, pl.ds(pl.multiple_of((kt // 2) * SUBLANES, SUBLANES), SUBLANES), :]
            rows4 = jnp.where(kt % 2 == 0, rows8[0:blocks_per_tile, :], rows8[blocks_per_tile:, :])
            bg = _block_rows(rows4, nq)
            biases.append(bg + _causal_bias(kt * KEY_TILE + krow, qpos) if causal else bg)
        vts = [svt_ref[kt, NSA_DK * g:NSA_DK * (g + 1), :] for g in range(NSA_G)]
        return _attn_step(carry, k, qtz, vts, biases)

    carry = lax.fori_loop(0, hi_tile - 1, lambda kt, c: slc_step(kt, c, False), _attn_init(nq))
    o_slc = _attn_finish(slc_step(hi_tile - 1, carry, True))

    def win_step(kt, carry):
        rows = pl.ds(pl.multiple_of(kt * KEY_TILE, KEY_TILE), KEY_TILE)
        k = wk_ref[0, rows, :].astype(BF16)
        wb = _window_bias(kt * KEY_TILE + krow, qpos)
        vts = [wvt_ref[kt, NSA_DK * g:NSA_DK * (g + 1), :] for g in range(NSA_G)]
        return _attn_step(carry, k, qtz, vts, [wb, wb])

    o_win = _attn_finish(lax.fori_loop(win_lo, hi_tile, win_step, _attn_init(nq)))

    for g in range(NSA_G):
        o_t = _gated_sum(gates[g], o_cmp[g], o_slc[g], o_win[g])
        stack = jnp.concatenate([o_t[:, r * tq:(r + 1) * tq] for r in range(NSA_R)], axis=0)
        o_ref[0, :, g * NSA_R * NSA_DK:(g + 1) * NSA_R * NSA_DK] = stack.T


def _overlap_t(nj_pad, nb_pad):
    n = np.arange(nb_pad)[None, :]
    j = np.arange(nj_pad)[:, None]
    lo = np.maximum(n * CMP_STRIDE, j * SLC_BLOCK)
    hi = np.minimum(n * CMP_STRIDE + CMP_BLOCK, (j + 1) * SLC_BLOCK)
    return jnp.asarray(np.maximum(hi - lo, 0).astype(np.float32) / CMP_STRIDE, BF16)


def _nsa_prompt(kv_new, win_new, q, gate, w1k, pek, w2k, *, tq):
    b, seq, _ = kv_new.shape
    assert tq % LANES == 0 and tq <= KEY_TILE and seq % KEY_TILE == 0
    nq = NSA_R * tq
    nch = seq // CMP_STRIDE
    n_slc = seq // SLC_BLOCK
    n_tiles = seq // KEY_TILE
    ov = _overlap_t(n_slc, nch)
    kern = functools.partial(_nsa_prompt_kernel, seq=seq, tq=tq)
    return pl.pallas_call(
        kern,
        grid=(b, seq // tq),
        in_specs=[
            pl.BlockSpec((1, seq, LANES), lambda bi, qi: (bi, 0, 0)),
            pl.BlockSpec((1, seq, LANES), lambda bi, qi: (bi, 0, 1)),
            pl.BlockSpec((1, seq, LANES), lambda bi, qi: (bi, 0, 2)),
            pl.BlockSpec((1, seq, LANES), lambda bi, qi: (bi, 0, 3)),
            pl.BlockSpec((1, seq, LANES), lambda bi, qi: (bi, 0, 0)),
            pl.BlockSpec((1, seq, LANES), lambda bi, qi: (bi, 0, 1)),
            pl.BlockSpec((1, tq, NSA_WIDTH), lambda bi, qi: (bi, qi, 0)),
            pl.BlockSpec((1, tq, LANES), lambda bi, qi: (bi, qi, 0)),
            pl.BlockSpec((n_slc, nch), lambda bi, qi: (0, 0)),
            pl.BlockSpec(w1k.shape, lambda bi, qi: (0, 0, 0, 0)),
            pl.BlockSpec(pek.shape, lambda bi, qi: (0, 0, 0, 0)),
            pl.BlockSpec(w2k.shape, lambda bi, qi: (0, 0, 0)),
        ],
        out_specs=pl.BlockSpec((1, tq, NSA_WIDTH), lambda bi, qi: (bi, qi, 0)),
        out_shape=jax.ShapeDtypeStruct((b, seq, NSA_WIDTH), F32),
        scratch_shapes=[
            pltpu.VMEM((nch, LANES), BF16),
            pltpu.VMEM((LANES, nch), BF16),
            pltpu.VMEM((n_tiles, LANES, KEY_TILE), BF16),
            pltpu.VMEM((n_tiles, LANES, KEY_TILE), BF16),
            pltpu.VMEM((NSA_G, n_slc, nq), F32),
        ],
        compiler_params=pltpu.CompilerParams(dimension_semantics=("arbitrary", "arbitrary"), vmem_limit_bytes=VMEM_LIMIT),
        name="nsa_prompt",
    )(kv_new, kv_new, kv_new, kv_new, win_new, win_new, q, gate, ov, w1k, pek, w2k)


def _prep_cmp_weights(cmp_pe, cmp_w1, cmp_w2):
    groups = CMP_STRIDE // CMP_PACK
    w1k = jnp.concatenate([cmp_w1[:, 0], cmp_w1[:, 1]], axis=-1)
    w1k = w1k.reshape(2, groups, CMP_PACK * NSA_DK, 2 * CMP_HIDDEN).astype(BF16)
    pek = cmp_pe.reshape(2, 2, groups, CMP_PACK * NSA_DK).transpose(0, 2, 1, 3)
    pek = jnp.pad(pek, ((0, 0), (0, 0), (0, SUBLANES - 2), (0, 0))).astype(BF16)
    return w1k, pek, cmp_w2.astype(BF16)


def _q_to_lanes(q, tq_real, tq):
    b, t, _ = q.shape
    nqt = t // tq_real
    x = (q * Q_SCALE).reshape(b, nqt, tq_real, NSA_G, NSA_R, NSA_DK)
    x = x.transpose(0, 1, 3, 5, 4, 2)
    x = jnp.pad(x, ((0, 0),) * 5 + ((0, tq - tq_real),)).reshape(b, nqt, NSA_G, NSA_DK, NSA_R * tq)
    z = jnp.zeros_like(x)
    return jnp.stack([jnp.concatenate([x[:, :, 0], z[:, :, 0]], axis=2),
                      jnp.concatenate([z[:, :, 1], x[:, :, 1]], axis=2)], axis=2).astype(BF16)


def _gate_to_lanes(gate, tq_real, tq):
    b, t, _ = gate.shape
    nqt = t // tq_real
    x = gate[..., :COL_GATE].reshape(b, nqt, tq_real, 3, NSA_G, NSA_R).transpose(0, 1, 4, 3, 5, 2)
    x = jnp.pad(x, ((0, 0),) * 5 + ((0, tq - tq_real),)).reshape(b, nqt, NSA_G, 3, NSA_R * tq)
    return jnp.pad(x, ((0, 0), (0, 0), (0, 0), (0, SUBLANES - 3), (0, 0)))


def _o_from_lanes(o_t, tq_real, tq):
    b, nqt = o_t.shape[:2]
    x = o_t.reshape(b, nqt, NSA_G, NSA_DK, NSA_R, tq)[..., :tq_real]
    return x.transpose(0, 1, 5, 2, 4, 3).reshape(b, nqt * tq_real, NSA_WIDTH)


def _nsa_decode_kernel(pt_ref, pool_ref, kvn_ref, wck_ref, wcv_ref, wn_ref, qtz_ref, gate_ref, ov_ref, w1_ref, pe_ref, w2_ref,
                       o_ref, stg_c, stg_s, sem, ckb, cvb, skb, svt_ref, tmp, wkb, wvb, kc_ref, vct_ref, wvt_ref, selb_ref,
                       *, past, n_new, tq):
    nq = NSA_R * tq
    n_pages = past // PAGE_SIZE
    nch = (past + n_new) // CMP_STRIDE
    assert nch * CMP_STRIDE <= past
    n_slc = -(-(past + n_new) // SLC_BLOCK)
    s_pad = skb.shape[0]
    w_pad = wkb.shape[0]
    big = s_pad // DEC_TILES
    b = pl.program_id(0)
    nb = pl.num_programs(0)

    def page_copy(bb, p, which):
        page = pt_ref[bb * n_pages + p]
        stg = (stg_c, stg_s)[which]
        return pltpu.make_async_copy(pool_ref.at[page, pl.ds(2 * which, 2)], stg.at[p], sem.at[which])

    def start_fetch(bb, which):
        def body(p, carry):
            page_copy(bb, p, which).start()
            return carry
        lax.fori_loop(0, n_pages, body, 0)

    def wait_fetch(bb, which):
        def body(p, carry):
            page_copy(bb, p, which).wait()
            return carry
        lax.fori_loop(0, n_pages, body, 0)

    @pl.when(b == 0)
    def _first():
        start_fetch(0, 0)
        start_fetch(0, 1)

    nwin = wck_ref.shape[1]
    for buf, cache_ref, lane0 in ((wkb, wck_ref, 0), (wvb, wcv_ref, LANES)):
        buf[0:nwin, :] = cache_ref[0]
        buf[nwin:, :] = jnp.zeros((w_pad - nwin, LANES), F32)
        buf[nwin:nwin + n_new, :] = wn_ref[0, :, lane0:lane0 + LANES]
    for kt in range(w_pad // KEY_TILE):
        cols = slice(kt * KEY_TILE, (kt + 1) * KEY_TILE)
        wvt_ref[:, cols] = wvb[cols, :].T.astype(BF16)

    wait_fetch(b, 0)
    for p in range(n_pages):
        rows = slice(p * PAGE_SIZE, (p + 1) * PAGE_SIZE)
        ckb[rows, :] = stg_c[p, 0].T
        cvb[rows, :] = stg_c[p, 1].T

    @pl.when(b + 1 < nb)
    def _next_cmp():
        start_fetch(b + 1, 0)

    wait_fetch(b, 1)
    for p in range(n_pages):
        rows = slice(p * PAGE_SIZE, (p + 1) * PAGE_SIZE)
        skb[rows, :] = stg_s[p, 0].T.astype(BF16)
        svt_ref[:, rows] = stg_s[p, 1].astype(BF16)

    @pl.when(b + 1 < nb)
    def _next_slc():
        start_fetch(b + 1, 1)

    tmp[...] = jnp.zeros(tmp.shape, F32)
    tmp[0:n_new, :] = kvn_ref[0, :, 2 * LANES:3 * LANES]
    skb[past:past + PAGE_SIZE, :] = tmp[...].astype(BF16)
    tmp[0:n_new, :] = kvn_ref[0, :, 3 * LANES:4 * LANES]
    svt_ref[:, past:past + PAGE_SIZE] = tmp[...].T.astype(BF16)
    if s_pad > past + PAGE_SIZE:
        skb[past + PAGE_SIZE:, :] = jnp.zeros((s_pad - past - PAGE_SIZE, LANES), BF16)
        svt_ref[:, past + PAGE_SIZE:] = jnp.zeros((LANES, s_pad - past - PAGE_SIZE), BF16)

    kc = _compress_pair(lambda s: ckb[pl.ds(s, nch, stride=CMP_STRIDE), :], w1_ref, pe_ref, w2_ref, 0, nch)
    vc = _compress_pair(lambda s: cvb[pl.ds(s, nch, stride=CMP_STRIDE), :], w1_ref, pe_ref, w2_ref, 1, nch)
    kc_ref[...] = kc.astype(BF16)
    vct_ref[...] = vc.T.astype(BF16)

    qpos = past + lax.broadcasted_iota(jnp.int32, (1, nq), 1) % tq
    qtz = [qtz_ref[0, 0, g] for g in range(NSA_G)]
    o_cmp, scores = [], []
    for g in range(NSA_G):
        oc, sc = _cmp_branch(g, qtz[g], qpos, tq, nq, kc_ref, vct_ref, ov_ref, nch - 1)
        o_cmp.append(oc)
        scores.append(sc)
    _select_blocks(scores, n_slc, tq, nq, selb_ref)

    blocks_per_big = big // SLC_BLOCK
    carry = _attn_init(nq)
    for i in range(DEC_TILES):
        rows = slice(i * big, (i + 1) * big)
        k = skb[rows, :]
        biases = []
        for g in range(NSA_G):
            bg = _block_rows(selb_ref[g, i * blocks_per_big:(i + 1) * blocks_per_big, :], nq)
            if (i + 1) * big > past:
                bg = bg + _causal_bias(i * big + lax.broadcasted_iota(jnp.int32, (big, nq), 0), qpos)
            biases.append(bg)
        vts = [svt_ref[NSA_DK * g:NSA_DK * (g + 1), rows] for g in range(NSA_G)]
        carry = _attn_step(carry, k, qtz, vts, biases)
    o_slc = _attn_finish(carry)

    wb = _window_bias(past - nwin + lax.broadcasted_iota(jnp.int32, (w_pad, nq), 0), qpos)
    vts = [wvt_ref[NSA_DK * g:NSA_DK * (g + 1), :] for g in range(NSA_G)]
    o_win = _attn_finish(_attn_step(_attn_init(nq), wkb[...].astype(BF16), qtz, vts, [wb, wb]))

    for g in range(NSA_G):
        o_ref[0, 0, g] = _gated_sum(gate_ref[0, 0, g], o_cmp[g], o_slc[g], o_win[g])


def _nsa_decode(pool, page_table, kv_new, win_cache, win_new, qtz, gate_t, w1k, pek, w2k, *, tq):
    b, n_pages = page_table.shape
    n_new = kv_new.shape[1]
    nwin = win_cache.shape[1]
    past = n_pages * PAGE_SIZE
    nq = NSA_R * tq
    nch = (past + n_new) // CMP_STRIDE
    step_keys = DEC_TILES * KEY_TILE
    s_pad = -(-(past + n_new) // step_keys) * step_keys
    w_pad = -(-(nwin + n_new) // KEY_TILE) * KEY_TILE
    nj_pad = -(-(s_pad // SLC_BLOCK) // 16) * 16
    assert nwin % KEY_TILE == 0 and n_new <= LANES and (s_pad // DEC_TILES) % LANES == 0
    ov = _overlap_t(nj_pad, nch)
    kern = functools.partial(_nsa_decode_kernel, past=past, n_new=n_new, tq=tq)
    grid_spec = pltpu.PrefetchScalarGridSpec(
        num_scalar_prefetch=1,
        grid=(b,),
        in_specs=[
            pl.BlockSpec(memory_space=pl.ANY),
            pl.BlockSpec((1, n_new, COL_KV), lambda bi, pt: (bi, 0, 0)),
            pl.BlockSpec((1, nwin, LANES), lambda bi, pt: (bi, 0, 0)),
            pl.BlockSpec((1, nwin, LANES), lambda bi, pt: (bi, 0, 1)),
            pl.BlockSpec((1, n_new, COL_WIN), lambda bi, pt: (bi, 0, 0)),
            pl.BlockSpec((1, 1, NSA_G, LANES, nq), lambda bi, pt: (bi, 0, 0, 0, 0)),
            pl.BlockSpec((1, 1, NSA_G, SUBLANES, nq), lambda bi, pt: (bi, 0, 0, 0, 0)),
            pl.BlockSpec((nj_pad, nch), lambda bi, pt: (0, 0)),
            pl.BlockSpec(w1k.shape, lambda bi, pt: (0, 0, 0, 0)),
            pl.BlockSpec(pek.shape, lambda bi, pt: (0, 0, 0, 0)),
            pl.BlockSpec(w2k.shape, lambda bi, pt: (0, 0, 0)),
        ],
        out_specs=pl.BlockSpec((1, 1, NSA_G, NSA_DK, nq), lambda bi, pt: (bi, 0, 0, 0, 0)),
        scratch_shapes=[
            pltpu.VMEM((n_pages, 2, LANES, PAGE_SIZE), F32),
            pltpu.VMEM((n_pages, 2, LANES, PAGE_SIZE), F32),
            pltpu.SemaphoreType.DMA((2,)),
            pltpu.VMEM((past, LANES), F32),
            pltpu.VMEM((past, LANES), F32),
            pltpu.VMEM((s_pad, LANES), BF16),
            pltpu.VMEM((LANES, s_pad), BF16),
            pltpu.VMEM((PAGE_SIZE, LANES), F32),
            pltpu.VMEM((w_pad, LANES), F32),
            pltpu.VMEM((w_pad, LANES), F32),
            pltpu.VMEM((nch, LANES), BF16),
            pltpu.VMEM((LANES, nch), BF16),
            pltpu.VMEM((LANES, w_pad), BF16),
            pltpu.VMEM((NSA_G, nj_pad, nq), F32),
        ],
    )
    return pl.pallas_call(
        kern,
        grid_spec=grid_spec,
        out_shape=jax.ShapeDtypeStruct((b, 1, NSA_G, NSA_DK, nq), F32),
        compiler_params=pltpu.CompilerParams(dimension_semantics=("arbitrary",), vmem_limit_bytes=VMEM_LIMIT),
        name="nsa_decode",
    )(page_table.reshape(-1), pool, kv_new, win_cache, win_cache, win_new, qtz, gate_t, ov, w1k, pek, w2k)


FF_TILE = 256


def _rmsnorm(x, w):
    return (x * lax.rsqrt(jnp.mean(x * x, axis=-1, keepdims=True) + EPS)) * w


def _gelu_tanh(c):
    return c * (0.5 * (1.0 + jnp.tanh(0.7978845608028654 * (c + 0.044715 * (c * c * c)))))


def _ffn_kernel(x_ref, ohg_ref, onsa_ref, past_ref, wout_ref, n2_ref, wg_ref, wv_ref, cw_ref, cb_ref, wd_ref, nf_ref,
                y_ref, conv_ref, carry_ref, *, tm):
    ti = pl.program_id(1)

    @pl.when(ti == 0)
    def _init():
        carry_ref[0:CONV_W - 1, :] = past_ref[0]

    x1 = (x_ref[0]
          + jnp.dot(ohg_ref[0].astype(BF16), wout_ref[0:HG_WIDTH, :], preferred_element_type=F32)
          + jnp.dot(onsa_ref[0].astype(BF16), wout_ref[HG_WIDTH:, :], preferred_element_type=F32))
    h = _rmsnorm(x1, n2_ref[...]).astype(BF16)
    row = lax.broadcasted_iota(jnp.int32, (tm, FF_TILE), 0)
    acc = jnp.zeros((tm, D_MODEL), F32)
    for j in range(D_FF // FF_TILE):
        cols = slice(j * FF_TILE, (j + 1) * FF_TILE)
        u = jnp.dot(h, wg_ref[:, cols], preferred_element_type=F32)
        val = jnp.dot(h, wv_ref[:, cols], preferred_element_type=F32)
        c0 = carry_ref[0:1, cols]
        c1 = carry_ref[1:2, cols]
        up1 = jnp.where(row == 0, c1, pltpu.roll(u, 1, axis=0))
        up2 = jnp.where(row == 0, c0, jnp.where(row == 1, c1, pltpu.roll(u, 2, axis=0)))
        c = cb_ref[:, cols] + up2 * cw_ref[0:1, cols] + up1 * cw_ref[1:2, cols] + u * cw_ref[2:3, cols]
        tail = u[tm - (CONV_W - 1):tm, :]
        carry_ref[0:CONV_W - 1, cols] = tail
        conv_ref[0, :, cols] = tail
        acc = acc + jnp.dot((_gelu_tanh(c) * val).astype(BF16), wd_ref[cols, :], preferred_element_type=F32)
    y_ref[0] = _rmsnorm(x1 + acc, nf_ref[...])


def _ffn(x, o_hg, o_nsa, conv_past, wts, *, tm):
    b, t, _ = x.shape
    assert t % tm == 0 and D_FF % FF_TILE == 0
    wout, n2, wg, wv, cw, cb, wd, nf = wts
    const2 = lambda bi, ti: (0, 0)
    one = pl.Buffered(1)
    return pl.pallas_call(
        functools.partial(_ffn_kernel, tm=tm),
        grid=(b, t // tm),
        in_specs=[
            pl.BlockSpec((1, tm, D_MODEL), lambda bi, ti: (bi, ti, 0)),
            pl.BlockSpec((1, tm, HG_WIDTH), lambda bi, ti: (bi, ti, 0)),
            pl.BlockSpec((1, tm, NSA_WIDTH), lambda bi, ti: (bi, ti, 0)),
            pl.BlockSpec((1, CONV_W - 1, D_FF), lambda bi, ti: (bi, 0, 0)),
            pl.BlockSpec(wout.shape, const2, pipeline_mode=one),
            pl.BlockSpec(n2.shape, const2),
            pl.BlockSpec(wg.shape, const2, pipeline_mode=one),
            pl.BlockSpec(wv.shape, const2, pipeline_mode=one),
            pl.BlockSpec(cw.shape, const2),
            pl.BlockSpec(cb.shape, const2),
            pl.BlockSpec(wd.shape, const2, pipeline_mode=one),
            pl.BlockSpec(nf.shape, const2),
        ],
        out_specs=[
            pl.BlockSpec((1, tm, D_MODEL), lambda bi, ti: (bi, ti, 0)),
            pl.BlockSpec((1, CONV_W - 1, D_FF), lambda bi, ti: (bi, 0, 0)),
        ],
        out_shape=[jax.ShapeDtypeStruct((b, t, D_MODEL), F32), jax.ShapeDtypeStruct((b, CONV_W - 1, D_FF), F32)],
        scratch_shapes=[pltpu.VMEM((SUBLANES, D_FF), F32)],
        compiler_params=pltpu.CompilerParams(dimension_semantics=("arbitrary", "arbitrary"), vmem_limit_bytes=VMEM_LIMIT),
        name="ffn",
    )(x, o_hg, o_nsa, conv_past, wout, n2, wg, wv, cw, cb, wd, nf)


def _ffn_steps_kernel(x_ref, om_ref, past_ref, wout_ref, n2_ref, wg_ref, wv_ref, cw_ref, cb_ref, wd_ref, nf_ref,
                      y_ref, conv_ref, *, n_steps, nb):
    x1 = x_ref[...] + jnp.dot(om_ref[...].astype(BF16), wout_ref[...], preferred_element_type=F32)
    h = _rmsnorm(x1, n2_ref[...]).astype(BF16)
    acc = jnp.zeros((n_steps * nb, D_MODEL), F32)
    for j in range(D_FF // FF_TILE):
        cols = slice(j * FF_TILE, (j + 1) * FF_TILE)
        u = jnp.dot(h, wg_ref[:, cols], preferred_element_type=F32)
        val = jnp.dot(h, wv_ref[:, cols], preferred_element_type=F32)
        taps = [past_ref[k, :, cols] for k in range(CONV_W - 1)] + [u[t * nb:(t + 1) * nb, :] for t in range(n_steps)]
        cs = []
        for t in range(n_steps):
            c = cb_ref[:, cols]
            for k in range(CONV_W):
                c = c + taps[t + k] * cw_ref[k:k + 1, cols]
            cs.append(c)
        for k in range(CONV_W - 1):
            conv_ref[k, :, cols] = taps[n_steps + k]
        c_all = jnp.concatenate(cs, axis=0)
        acc = acc + jnp.dot((_gelu_tanh(c_all) * val).astype(BF16), wd_ref[cols, :], preferred_element_type=F32)
    y_ref[...] = _rmsnorm(x1 + acc, nf_ref[...])


def _ffn_steps(x_t, omix_t, past_t, wts, *, n_steps, nb):
    return pl.pallas_call(
        functools.partial(_ffn_steps_kernel, n_steps=n_steps, nb=nb),
        out_shape=[jax.ShapeDtypeStruct((n_steps * nb, D_MODEL), F32), jax.ShapeDtypeStruct((CONV_W - 1, nb, D_FF), F32)],
        compiler_params=pltpu.CompilerParams(vmem_limit_bytes=VMEM_LIMIT),
        name="ffn_steps",
    )(x_t, omix_t, past_t, *wts)


def kernel(x_prompt, x_sample, cache_nsa_kv, cache_win_kv, state_hgrn, state_ffn_conv, page_table, norm1, w_in, hg_lb_logits, hg_norm, cmp_pe, cmp_w1, cmp_w2, w_out, norm2, w_gate, w_val, conv_w, conv_b, w_down, norm_f):
    B, T, _ = x_prompt.shape
    Bd, Td, _ = x_sample.shape
    depth = norm1.shape[0]
    assert depth == 1
    l = 0
    w_in_bf = jnp.pad(w_in[l], ((0, 0), (0, _C_GATE - D_IN))).astype(BF16)
    w1k, pek, w2k = _prep_cmp_weights(cmp_pe[l], cmp_w1[l], cmp_w2[l])
    ffn_w = (w_out[l].astype(BF16), norm2[l].reshape(1, D_MODEL), w_gate[l].astype(BF16), w_val[l].astype(BF16),
             conv_w[l], conv_b[l].reshape(1, D_FF), w_down[l].astype(BF16), norm_f.reshape(1, D_MODEL))

    z_hg, q_p, kv_p, win_p, gate_p = _in_proj(x_prompt.reshape(B * T, D_MODEL), norm1[l], w_in_bf)
    o_hg_p, s_p = _hgrn(z_hg.reshape(B, T, _C_HG), hg_lb_logits, hg_norm[l],
                        jnp.zeros((B, HG_HEADS, HG_DK, HG_DK), F32), chunk=HG_CHUNK, rows_per_step=512, t_valid=None)
    kv_p = kv_p.reshape(B, T, COL_KV)
    win_p = win_p.reshape(B, T, COL_WIN)
    o_nsa_p = _nsa_prompt(kv_p, win_p, q_p.reshape(B, T, NSA_WIDTH), gate_p.reshape(B, T, LANES), w1k, pek, w2k,
                          tq=KEY_TILE)
    y_p, conv_p = _ffn(x_prompt, o_hg_p, o_nsa_p, jnp.zeros((B, CONV_W - 1, D_FF), F32), ffn_w, tm=512)

    z_hg, q_s, kv_s, win_s, gate_s = _in_proj(x_sample.reshape(Bd * Td, D_MODEL), norm1[l], w_in_bf)
    t_pad = HG_SUB
    z_pad = jnp.pad(z_hg.reshape(Bd, Td, _C_HG), ((0, 0), (0, t_pad - Td), (0, 0)))
    o_hg_s, s_s = _hgrn(z_pad, hg_lb_logits, hg_norm[l], state_hgrn[l], chunk=t_pad, rows_per_step=t_pad, t_valid=Td)
    o_hg_s = o_hg_s[:, :Td]
    tqs = LANES // NSA_R
    kv_s = kv_s.reshape(Bd, Td, COL_KV)
    win_s = win_s.reshape(Bd, Td, COL_WIN)
    n_pool = cache_nsa_kv.shape[1]
    nwin = cache_win_kv.shape[2]
    pool_t = jnp.transpose(cache_nsa_kv[l], (0, 2, 3, 4, 1)).reshape(n_pool, 4, NSA_G * NSA_DK, PAGE_SIZE)
    o_t = _nsa_decode(pool_t, page_table, kv_s,
                      cache_win_kv[l].reshape(Bd, nwin, COL_WIN), win_s,
                      _q_to_lanes(q_s.reshape(Bd, Td, NSA_WIDTH), Td, tqs),
                      _gate_to_lanes(gate_s.reshape(Bd, Td, LANES), Td, tqs), w1k, pek, w2k, tq=tqs)
    o_nsa_s = _o_from_lanes(o_t, Td, tqs)
    omix_t = jnp.concatenate([o_hg_s, o_nsa_s], axis=-1).transpose(1, 0, 2).reshape(Td * Bd, D_MODEL)
    y_t, conv_t = _ffn_steps(x_sample.transpose(1, 0, 2).reshape(Td * Bd, D_MODEL), omix_t,
                             state_ffn_conv[l].transpose(1, 0, 2), ffn_w, n_steps=Td, nb=Bd)
    y_s = y_t.reshape(Td, Bd, D_MODEL).transpose(1, 0, 2)
    conv_s = conv_t.transpose(1, 0, 2)

    win_keep_p = min(WINDOW, T)
    win_all_s = jnp.concatenate([cache_win_kv[l].reshape(Bd, nwin, COL_WIN), win_s], axis=1)[:, -nwin:]
    return (y_p, y_s,
            kv_p.reshape(1, B, T, 4, NSA_G, NSA_DK), kv_s.reshape(1, Bd, Td, 4, NSA_G, NSA_DK),
            win_p[:, T - win_keep_p:].reshape(1, B, win_keep_p, 2, NSA_G, NSA_DK),
            win_all_s.reshape(1, Bd, nwin, 2, NSA_G, NSA_DK),
            s_p[None], s_s[None], conv_p[None], conv_s[None])
```

```python
import functools

import numpy as np
import jax
import jax.numpy as jnp
from jax import lax
from jax.experimental import pallas as pl
from jax.experimental.pallas import tpu as pltpu

F32 = jnp.float32
BF16 = jnp.bfloat16

D_MODEL = 1024
PAGE_SIZE = 128
HG_WIDTH = 512
HG_HEADS = 4
HG_DK = 128
HG_CHUNK = 64
NSA_WIDTH = 512
NSA_HEADS = 8
NSA_DK = 64
NSA_G = 2
NSA_R = 4
CMP_STRIDE = 16
CMP_BLOCK = 32
CMP_HIDDEN = 256
SLC_BLOCK = 64
SLC_TOPK = 16
WINDOW = 512
D_FF = 2816
CONV_W = 3
EPS = 1e-6
COL_KV = 512
COL_WIN = 256
COL_GATE = 24

LANES = 128
SUBLANES = 8
HG_SUB = 16
HG_UNROLL = 8
HG_MASKED_EXP = -1e30
VMEM_LIMIT = 56 * 1024 * 1024

_C_HG = 4 * HG_WIDTH
_C_Q = _C_HG + NSA_WIDTH
_C_KV = _C_Q + COL_KV
_C_WIN = _C_KV + COL_WIN
_C_GATE = _C_WIN + LANES
D_IN = _C_WIN + COL_GATE


def _sigmoid(x):
    return 1.0 / (1.0 + jnp.exp(-x))


def _split3_bf16(x):
    hi = x.astype(BF16)
    r1 = x - hi.astype(F32)
    mid = r1.astype(BF16)
    lo = (r1 - mid.astype(F32)).astype(BF16)
    return hi, mid, lo


def _in_proj_kernel(x_ref, g_ref, w_ref, hg_ref, q_ref, kv_ref, win_ref, gate_ref):
    x = x_ref[...]
    ms = jnp.mean(x * x, axis=-1, keepdims=True)
    h = ((x * lax.rsqrt(ms + EPS)) * g_ref[...]).astype(BF16)
    hg_ref[...] = jnp.dot(h, w_ref[:, 0:_C_HG], preferred_element_type=F32)
    q_ref[...] = jnp.dot(h, w_ref[:, _C_HG:_C_Q], preferred_element_type=F32)
    kv_ref[...] = jnp.dot(h, w_ref[:, _C_Q:_C_KV], preferred_element_type=F32)
    win_ref[...] = jnp.dot(h, w_ref[:, _C_KV:_C_WIN], preferred_element_type=F32)
    gate_ref[...] = jnp.dot(h, w_ref[:, _C_WIN:_C_GATE], preferred_element_type=F32)


def _in_proj(x2d, norm_w, w_in_bf):
    n = x2d.shape[0]
    tm = 512
    assert n % tm == 0
    widths = (_C_HG, NSA_WIDTH, COL_KV, COL_WIN, LANES)
    return pl.pallas_call(
        _in_proj_kernel,
        grid=(n // tm,),
        in_specs=[
            pl.BlockSpec((tm, D_MODEL), lambda i: (i, 0)),
            pl.BlockSpec((1, D_MODEL), lambda i: (0, 0)),
            pl.BlockSpec((D_MODEL, _C_GATE), lambda i: (0, 0)),
        ],
        out_specs=[pl.BlockSpec((tm, w), lambda i: (i, 0)) for w in widths],
        out_shape=[jax.ShapeDtypeStruct((n, w), F32) for w in widths],
        compiler_params=pltpu.CompilerParams(dimension_semantics=("arbitrary",), vmem_limit_bytes=VMEM_LIMIT),
        name="in_proj",
    )(x2d, norm_w.reshape(1, D_MODEL), w_in_bf)


def _hgrn_kernel(q_ref, f_ref, i_ref, g_ref, lb_ref, nw_ref, s0_ref, e_ref, o_ref, sfin_ref, st_ref,
                 *, chunk, rows_per_step, t_valid, layer):
    C = chunk
    nsb = C // HG_SUB
    ti = pl.program_id(2)

    hps = st_ref.shape[0]

    @pl.when(ti == 0)
    def _init():
        for hh in range(hps):
            st_ref[hh] = s0_ref[0, hh].T

    lg = lb_ref[...]
    le = jnp.exp(lg - jnp.max(lg, axis=0, keepdims=True))
    lb_all = jnp.sum(le[0:layer + 1, :], axis=0, keepdims=True) / jnp.sum(le, axis=0, keepdims=True)
    nw_all = nw_ref[...]
    row_i = lax.broadcasted_iota(jnp.int32, (C, C), 0)
    col_i = lax.broadcasted_iota(jnp.int32, (C, C), 1)
    tri_bf = jnp.where(col_i <= row_i, 1.0, 0.0).astype(BF16)
    diag_mask = ((row_i // HG_SUB) == (col_i // HG_SUB)) & (col_i <= row_i)
    off_mask = (col_i // HG_SUB) < (row_i // HG_SUB)
    trel = lax.broadcasted_iota(jnp.int32, (C, HG_DK), 0) % HG_SUB
    nt_dims = (((1,), (1,)), ((), ()))
    tn_dims = (((0,), (0,)), ((), ()))

    def do_chunk(hh, c):
        lanes = slice(hh * HG_DK, (hh + 1) * HG_DK)
        lb = lb_all[:, lanes]
        one_m_lb = 1.0 - lb
        nw = nw_all[:, lanes]
        r0 = pl.multiple_of(c * C, C)
        rows = pl.ds(r0, C)
        fr = f_ref[0, rows, lanes]
        qr = q_ref[0, rows, lanes]
        v = i_ref[0, rows, lanes]
        gr = g_ref[0, rows, lanes]
        f = lb + one_m_lb * _sigmoid(fr)
        kk = one_m_lb * _sigmoid(-fr)
        logf = jnp.log2(f)
        if t_valid is not None:
            tok = ti * rows_per_step + r0 + lax.broadcasted_iota(jnp.int32, (C, HG_DK), 0)
            ok = tok < t_valid
            logf = jnp.where(ok, logf, 0.0)
            kk = jnp.where(ok, kk, 0.0)
        q = qr * _sigmoid(qr)
        hi, mid, lo = _split3_bf16(logf)
        a = (jnp.dot(tri_bf, hi, preferred_element_type=F32)
             + jnp.dot(tri_bf, mid, preferred_element_type=F32)
             + jnp.dot(tri_bf, lo, preferred_element_type=F32))
        st = st_ref[hh]
        vb = v.astype(BF16)
        o = lax.dot_general((q * jnp.exp2(a)).astype(BF16), st.astype(BF16), nt_dims, preferred_element_type=F32)

        if nsb > 1:
            parts = [jnp.zeros((HG_SUB, C), F32)]
            for i in range(1, nsb):
                bi = a[HG_SUB * i - 1:HG_SUB * i, :]
                qi = q[HG_SUB * i:HG_SUB * (i + 1), :] * jnp.exp2(a[HG_SUB * i:HG_SUB * (i + 1), :] - bi)
                ki = kk * jnp.exp2(jnp.minimum(bi - a, 0.0))
                parts.append(lax.dot_general(qi.astype(BF16), ki.astype(BF16), nt_dims, preferred_element_type=F32))
            att = jnp.where(off_mask, jnp.concatenate(parts, axis=0), 0.0)
        else:
            att = jnp.zeros((C, C), F32)

        a3 = a.reshape(nsb, HG_SUB, HG_DK)
        k3 = kk.reshape(nsb, HG_SUB, HG_DK)
        zs = []
        for j in range(HG_SUB):
            aj = jnp.broadcast_to(a3[:, j:j + 1, :], (nsb, HG_SUB, HG_DK)).reshape(C, HG_DK)
            kj = jnp.broadcast_to(k3[:, j:j + 1, :], (nsb, HG_SUB, HG_DK)).reshape(C, HG_DK)
            expo = a - aj if j == 0 else jnp.where(trel >= j, a - aj, HG_MASKED_EXP)
            zs.append((q * kj * jnp.exp2(expo)).astype(BF16))
        zcat = jnp.concatenate(zs, axis=1)
        att_d = jnp.dot(zcat, e_ref[...], preferred_element_type=F32)
        att = att + jnp.where(diag_mask, att_d, 0.0)
        o = o + jnp.dot(att.astype(BF16), vb, preferred_element_type=F32)

        a_end = a[C - 1:C, :]
        kdec = kk * jnp.exp2(a_end - a)
        st_ref[hh] = st * jnp.exp2(a_end) + lax.dot_general(vb, kdec.astype(BF16), tn_dims, preferred_element_type=F32)

        on = (o * lax.rsqrt(jnp.mean(o * o, axis=-1, keepdims=True) + EPS)) * nw
        o_ref[0, rows, lanes] = on * (gr * _sigmoid(gr))

    n_chunks = rows_per_step // C
    if n_chunks == 1:
        for hh in range(hps):
            do_chunk(hh, 0)
    else:
        for hh in range(hps):
            def chunk_body(c, carry, hh=hh):
                do_chunk(hh, c)
                return carry
            lax.fori_loop(0, n_chunks, chunk_body, 0, unroll=min(n_chunks, HG_UNROLL))

    @pl.when(ti == pl.num_programs(2) - 1)
    def _fin():
        for hh in range(hps):
            sfin_ref[0, hh] = st_ref[hh].T


def _hgrn(z_hg, lb_logits, norm_w, s0, *, chunk, rows_per_step, t_valid, heads_per_step=1, layer=0):
    b, t, _ = z_hg.shape
    n_lb = lb_logits.shape[0]
    hps = heads_per_step
    assert t % rows_per_step == 0 and rows_per_step % chunk == 0 and chunk % HG_SUB == 0 and HG_HEADS % hps == 0
    nt = t // rows_per_step
    hblocks = HG_HEADS // hps
    width = hps * HG_DK
    sub = np.arange(chunk) % HG_SUB
    e_mat = jnp.asarray((np.repeat(np.arange(HG_SUB), HG_DK)[:, None] == sub[None, :]).astype(np.float32), BF16)

    def zspec(k):
        return pl.BlockSpec((1, rows_per_step, width), lambda bi, h, ti, k=k: (bi, ti, k * hblocks + h))

    kern = functools.partial(_hgrn_kernel, chunk=chunk, rows_per_step=rows_per_step, t_valid=t_valid, layer=layer)
    return pl.pallas_call(
        kern,
        grid=(b, hblocks, nt),
        in_specs=[
            zspec(0), zspec(1), zspec(2), zspec(3),
            pl.BlockSpec((n_lb, width), lambda bi, h, ti: (0, h)),
            pl.BlockSpec((1, width), lambda bi, h, ti: (0, h)),
            pl.BlockSpec((1, hps, HG_DK, HG_DK), lambda bi, h, ti: (bi, h, 0, 0)),
            pl.BlockSpec((HG_SUB * HG_DK, chunk), lambda bi, h, ti: (0, 0)),
        ],
        out_specs=[
            pl.BlockSpec((1, rows_per_step, width), lambda bi, h, ti: (bi, ti, h)),
            pl.BlockSpec((1, hps, HG_DK, HG_DK), lambda bi, h, ti: (bi, h, 0, 0)),
        ],
        out_shape=[jax.ShapeDtypeStruct((b, t, HG_WIDTH), F32), jax.ShapeDtypeStruct((b, HG_HEADS, HG_DK, HG_DK), F32)],
        scratch_shapes=[pltpu.VMEM((hps, HG_DK, HG_DK), F32)],
        compiler_params=pltpu.CompilerParams(dimension_semantics=("arbitrary", "arbitrary", "arbitrary"),
                                             vmem_limit_bytes=VMEM_LIMIT),
        name="hgrn2",
    )(z_hg, z_hg, z_hg, z_hg, lb_logits, norm_w.reshape(1, HG_WIDTH), s0, e_mat)


KEY_TILE = 256
CMP_PACK = 4
DEC_TILES = 3
LOG2E = 1.4426950408889634
Q_SCALE = LOG2E * NSA_DK ** -0.5
NEG_INF = float("-inf")


def _compress_pair(load_xs, w1_ref, pe_ref, w2_ref, slot, nch):
    r = jnp.zeros((SUBLANES, 2 * CMP_HIDDEN), F32)
    for u in range(CMP_STRIDE // CMP_PACK):
        r = r + jnp.dot(pe_ref[slot, u], w1_ref[slot, u], preferred_element_type=F32)
    bias = r[0:1, 0:CMP_HIDDEN] + r[1:2, CMP_HIDDEN:2 * CMP_HIDDEN]
    outs = []
    for g in range(NSA_G):
        acc = jnp.zeros((nch, 2 * CMP_HIDDEN), F32)
        for u in range(CMP_STRIDE // CMP_PACK):
            xg = jnp.concatenate([load_xs(CMP_PACK * u + i)[:, NSA_DK * g:NSA_DK * (g + 1)] for i in range(CMP_PACK)],
                                 axis=1).astype(BF16)
            acc = acc + jnp.dot(xg, w1_ref[slot, u], preferred_element_type=F32)
        h = acc[:, 0:CMP_HIDDEN] + pltpu.roll(acc[:, CMP_HIDDEN:2 * CMP_HIDDEN], nch - 1, axis=0) + bias
        hs = (h * _sigmoid(h)).astype(BF16)
        outs.append(jnp.dot(hs, w2_ref[slot], preferred_element_type=F32))
    return jnp.concatenate(outs, axis=1)


def _rank_bias(score, n_slc):
    nj, w = score.shape
    jrow = lax.broadcasted_iota(jnp.int32, (SUBLANES, w), 0)
    slabs = [score[SUBLANES * v:SUBLANES * (v + 1), :] for v in range(nj // SUBLANES)]
    ranks = [jnp.zeros((SUBLANES, w), F32) for _ in slabs]
    for jp in range(n_slc):
        row = score[jp:jp + 1, :]
        for v, slab in enumerate(slabs):
            lo = SUBLANES * v
            if lo > jp:
                beats = row >= slab
            elif lo + SUBLANES - 1 <= jp:
                beats = row > slab
            else:
                beats = (row > slab) | ((row == slab) & (jrow > jp - lo))
            ranks[v] = ranks[v] + jnp.where(beats, 1.0, 0.0)
    rank = jnp.concatenate(ranks, axis=0)
    return jnp.where((rank < float(SLC_TOPK)) & (score > NEG_INF), 0.0, NEG_INF)


def _cmp_branch(g, qtz, qpos, tq, nq, kc_ref, vct_ref, ov_ref, n_blk):
    nb_pad = kc_ref.shape[0]
    nj_pad = ov_ref.shape[0]
    s = jnp.dot(kc_ref[...], qtz, preferred_element_type=F32)
    n_idx = lax.broadcasted_iota(jnp.int32, (nb_pad, nq), 0)
    cmask = (n_idx * CMP_STRIDE + (CMP_BLOCK - 1) <= qpos) & (n_idx < n_blk)
    s = jnp.where(cmask, s, NEG_INF)
    m = jnp.max(s, axis=0, keepdims=True)
    m = jnp.where(m == NEG_INF, 0.0, m)
    e = jnp.exp2(s - m)
    d = jnp.sum(e, axis=0, keepdims=True)
    p = e / jnp.where(d > 0.0, d, 1.0)
    o_cmp = jnp.dot(vct_ref[NSA_DK * g:NSA_DK * (g + 1), :], p.astype(BF16), preferred_element_type=F32)

    if tq % LANES == 0:
        ps = p[:, 0:tq]
        for r in range(1, NSA_R):
            ps = ps + p[:, r * tq:(r + 1) * tq]
        qpos_w = qpos[:, 0:tq]
    else:
        assert nq == LANES and NSA_R * tq == LANES
        ps = p
        for r in range(1, NSA_R):
            ps = ps + pltpu.roll(p, r * tq, axis=1)
        qpos_w = qpos
    w = ps.shape[1]
    hi, mid, lo = _split3_bf16(ps)
    ov = ov_ref[...]
    imp = (jnp.dot(ov, hi, preferred_element_type=F32) + jnp.dot(ov, mid, preferred_element_type=F32)
           + jnp.dot(ov, lo, preferred_element_type=F32))
    j_idx = lax.broadcasted_iota(jnp.int32, (nj_pad, w), 0)
    cur = qpos_w // SLC_BLOCK
    forced = (j_idx == 0) | (j_idx == cur) | (j_idx == cur - 1)
    score = jnp.where(forced, jnp.inf, jnp.where(j_idx <= cur, imp, NEG_INF))
    return o_cmp, score


def _select_blocks(scores, n_slc, tq, nq, selb_ref):
    w = scores[0].shape[1]
    if w == nq:
        assert NSA_G == 2 and nq == LANES
        lane = lax.broadcasted_iota(jnp.int32, scores[0].shape, 1)
        low = lane < LANES // 2
        bias = _rank_bias(jnp.where(low, scores[0], scores[1]), n_slc)
        swapped = pltpu.roll(bias, LANES // 2, axis=1)
        selb_ref[0] = jnp.where(low, bias, swapped)
        selb_ref[1] = jnp.where(low, swapped, bias)
        return
    bias = _rank_bias(jnp.concatenate(scores, axis=1), n_slc)
    for g in range(NSA_G):
        selb_ref[g] = jnp.concatenate([bias[:, g * w:(g + 1) * w]] * (nq // w), axis=1)


def _block_rows(rows, nq):
    n = rows.shape[0]
    return jnp.broadcast_to(rows[:, None, :], (n, SLC_BLOCK, nq)).reshape(n * SLC_BLOCK, nq)


def _attn_step(carry, k, qtz, vts, biases):
    out = []
    for g in range(NSA_G):
        m, l, acc = carry[g]
        s = jnp.dot(k, qtz[g], preferred_element_type=F32) + biases[g]
        m_new = jnp.maximum(m, jnp.max(s, axis=0, keepdims=True))
        m_safe = jnp.where(m_new == NEG_INF, 0.0, m_new)
        alpha = jnp.exp2(m - m_safe)
        p = jnp.exp2(s - m_safe)
        l = l * alpha + jnp.sum(p, axis=0, keepdims=True)
        acc = acc * alpha + jnp.dot(vts[g], p.astype(BF16), preferred_element_type=F32)
        out.append((m_new, l, acc))
    return tuple(out)


def _attn_init(nq):
    return tuple((jnp.full((1, nq), NEG_INF, F32), jnp.zeros((1, nq), F32), jnp.zeros((NSA_DK, nq), F32))
                 for _ in range(NSA_G))


def _attn_finish(carry):
    return [acc / jnp.where(l > 0.0, l, 1.0) for (_, l, acc) in carry]


def _causal_bias(kpos, qpos):
    return jnp.where(kpos <= qpos, 0.0, NEG_INF)


def _window_bias(kpos, qpos):
    d = kpos - qpos
    return jnp.where((d <= 0) & (d > -WINDOW), 0.0, NEG_INF)


def _gated_sum(gate_raw, o_cmp, o_slc, o_win):
    gt = _sigmoid(gate_raw)
    return gt[0:1, :] * o_cmp + gt[1:2, :] * o_slc + gt[2:3, :] * o_win


def _nsa_prompt_kernel(ck_ref, cv_ref, sk_ref, sv_ref, wk_ref, wv_ref, q_ref, gate_ref, ov_ref, w1_ref, pe_ref, w2_ref,
                       o_ref, kc_ref, vct_ref, svt_ref, wvt_ref, selb_ref, *, seq, tq):
    nq = NSA_R * tq
    nch = seq // CMP_STRIDE
    n_tiles = seq // KEY_TILE
    n_slc = seq // SLC_BLOCK
    blocks_per_tile = KEY_TILE // SLC_BLOCK
    assert 2 * blocks_per_tile == SUBLANES
    qt = pl.program_id(1)

    @pl.when(qt == 0)
    def _prep():
        kc = _compress_pair(lambda s: ck_ref[0, pl.ds(s, nch, stride=CMP_STRIDE), :], w1_ref, pe_ref, w2_ref, 0, nch)
        vc = _compress_pair(lambda s: cv_ref[0, pl.ds(s, nch, stride=CMP_STRIDE), :], w1_ref, pe_ref, w2_ref, 1, nch)
        kc_ref[...] = kc.astype(BF16)
        vct_ref[...] = vc.T.astype(BF16)
        for kt in range(n_tiles):
            rows = slice(kt * KEY_TILE, (kt + 1) * KEY_TILE)
            svt_ref[kt] = sv_ref[0, rows, :].T.astype(BF16)
            wvt_ref[kt] = wv_ref[0, rows, :].T.astype(BF16)

    t0 = qt * tq
    qpos = t0 + lax.broadcasted_iota(jnp.int32, (1, nq), 1) % tq
    hi_tile = (t0 + tq - 1) // KEY_TILE + 1
    win_lo = jnp.maximum(t0 - (WINDOW - 1), 0) // KEY_TILE
    krow = lax.broadcasted_iota(jnp.int32, (KEY_TILE, nq), 0)

    gate_t = gate_ref[0].T
    zeros_half = jnp.zeros((NSA_DK, nq), F32)
    qtz, gates = [], []
    for g in range(NSA_G):
        qg_t = q_ref[0, :, g * NSA_R * NSA_DK:(g + 1) * NSA_R * NSA_DK].T
        q_t = jnp.concatenate([qg_t[r * NSA_DK:(r + 1) * NSA_DK, :] for r in range(NSA_R)], axis=1) * Q_SCALE
        qtz.append(jnp.concatenate([q_t, zeros_half] if g == 0 else [zeros_half, q_t], axis=0).astype(BF16))
        gates.append(jnp.concatenate(
            [jnp.concatenate([gate_t[k * NSA_HEADS + g * NSA_R + r:k * NSA_HEADS + g * NSA_R + r + 1, :]
                              for r in range(NSA_R)], axis=1) for k in range(3)], axis=0))

    o_cmp, scores = [], []
    for g in range(NSA_G):
        oc, sc = _cmp_branch(g, qtz[g], qpos, tq, nq, kc_ref, vct_ref, ov_ref, nch - 1)
        o_cmp.append(oc)
        scores.append(sc)
    _select_blocks(scores, n_slc, tq, nq, selb_ref)

    def slc_step(kt, carry, causal):
        rows = pl.ds(pl.multiple_of(kt * KEY_TILE, KEY_TILE), KEY_TILE)
        k = sk_ref[0, rows, :].astype(BF16)
        biases = []
        for g in range(NSA_G):
            rows8 = selb_ref[g, pl.ds(pl.multiple_of((kt // 2) * SUBLANES, SUBLANES), SUBLANES), :]
            rows4 = jnp.where(kt % 2 == 0, rows8[0:blocks_per_tile, :], rows8[blocks_per_tile:, :])
            bg = _block_rows(rows4, nq)
            biases.append(bg + _causal_bias(kt * KEY_TILE + krow, qpos) if causal else bg)
        vts = [svt_ref[kt, NSA_DK * g:NSA_DK * (g + 1), :] for g in range(NSA_G)]
        return _attn_step(carry, k, qtz, vts, biases)

    def win_step(kt, carry):
        rows = pl.ds(pl.multiple_of(kt * KEY_TILE, KEY_TILE), KEY_TILE)
        k = wk_ref[0, rows, :].astype(BF16)
        wb = _window_bias(kt * KEY_TILE + krow, qpos)
        vts = [wvt_ref[kt, NSA_DK * g:NSA_DK * (g + 1), :] for g in range(NSA_G)]
        return _attn_step(carry, k, qtz, vts, [wb, wb])

    carry = lax.fori_loop(0, hi_tile - 1, lambda kt, c: slc_step(kt, c, False), _attn_init(nq))
    o_slc = _attn_finish(slc_step(hi_tile - 1, carry, True))
    o_win = _attn_finish(lax.fori_loop(win_lo, hi_tile, win_step, _attn_init(nq)))

    for g in range(NSA_G):
        o_t = _gated_sum(gates[g], o_cmp[g], o_slc[g], o_win[g])
        stack = jnp.concatenate([o_t[:, r * tq:(r + 1) * tq] for r in range(NSA_R)], axis=0)
        o_ref[0, :, g * NSA_R * NSA_DK:(g + 1) * NSA_R * NSA_DK] = stack.T


def _overlap_t(nj_pad, nb_pad):
    n = np.arange(nb_pad)[None, :]
    j = np.arange(nj_pad)[:, None]
    lo = np.maximum(n * CMP_STRIDE, j * SLC_BLOCK)
    hi = np.minimum(n * CMP_STRIDE + CMP_BLOCK, (j + 1) * SLC_BLOCK)
    return jnp.asarray(np.maximum(hi - lo, 0).astype(np.float32) / CMP_STRIDE, BF16)


def _nsa_prompt(kv_new, win_new, q, gate, w1k, pek, w2k, *, tq):
    b, seq, _ = kv_new.shape
    assert tq % LANES == 0 and tq <= KEY_TILE and seq % KEY_TILE == 0
    nq = NSA_R * tq
    nch = seq // CMP_STRIDE
    n_slc = seq // SLC_BLOCK
    n_tiles = seq // KEY_TILE
    ov = _overlap_t(n_slc, nch)
    kern = functools.partial(_nsa_prompt_kernel, seq=seq, tq=tq)
    return pl.pallas_call(
        kern,
        grid=(b, seq // tq),
        in_specs=[
            pl.BlockSpec((1, seq, LANES), lambda bi, qi: (bi, 0, 0)),
            pl.BlockSpec((1, seq, LANES), lambda bi, qi: (bi, 0, 1)),
            pl.BlockSpec((1, seq, LANES), lambda bi, qi: (bi, 0, 2)),
            pl.BlockSpec((1, seq, LANES), lambda bi, qi: (bi, 0, 3)),
            pl.BlockSpec((1, seq, LANES), lambda bi, qi: (bi, 0, 0)),
            pl.BlockSpec((1, seq, LANES), lambda bi, qi: (bi, 0, 1)),
            pl.BlockSpec((1, tq, NSA_WIDTH), lambda bi, qi: (bi, qi, 0)),
            pl.BlockSpec((1, tq, LANES), lambda bi, qi: (bi, qi, 0)),
            pl.BlockSpec((n_slc, nch), lambda bi, qi: (0, 0)),
            pl.BlockSpec(w1k.shape, lambda bi, qi: (0, 0, 0, 0)),
            pl.BlockSpec(pek.shape, lambda bi, qi: (0, 0, 0, 0)),
            pl.BlockSpec(w2k.shape, lambda bi, qi: (0, 0, 0)),
        ],
        out_specs=pl.BlockSpec((1, tq, NSA_WIDTH), lambda bi, qi: (bi, qi, 0)),
        out_shape=jax.ShapeDtypeStruct((b, seq, NSA_WIDTH), F32),
        scratch_shapes=[
            pltpu.VMEM((nch, LANES), BF16),
            pltpu.VMEM((LANES, nch), BF16),
            pltpu.VMEM((n_tiles, LANES, KEY_TILE), BF16),
            pltpu.VMEM((n_tiles, LANES, KEY_TILE), BF16),
            pltpu.VMEM((NSA_G, n_slc, nq), F32),
        ],
        compiler_params=pltpu.CompilerParams(dimension_semantics=("arbitrary", "arbitrary"), vmem_limit_bytes=VMEM_LIMIT),
        name="nsa_prompt",
    )(kv_new, kv_new, kv_new, kv_new, win_new, win_new, q, gate, ov, w1k, pek, w2k)


def _prep_cmp_weights(cmp_pe, cmp_w1, cmp_w2):
    groups = CMP_STRIDE // CMP_PACK
    w1k = jnp.concatenate([cmp_w1[:, 0], cmp_w1[:, 1]], axis=-1)
    w1k = w1k.reshape(2, groups, CMP_PACK * NSA_DK, 2 * CMP_HIDDEN).astype(BF16)
    pek = cmp_pe.reshape(2, 2, groups, CMP_PACK * NSA_DK).transpose(0, 2, 1, 3)
    pek = jnp.pad(pek, ((0, 0), (0, 0), (0, SUBLANES - 2), (0, 0))).astype(BF16)
    return w1k, pek, cmp_w2.astype(BF16)


def _q_to_lanes(q, tq_real, tq):
    b, t, _ = q.shape
    nqt = t // tq_real
    x = (q * Q_SCALE).reshape(b, nqt, tq_real, NSA_G, NSA_R, NSA_DK)
    x = x.transpose(0, 1, 3, 5, 4, 2)
    x = jnp.pad(x, ((0, 0),) * 5 + ((0, tq - tq_real),)).reshape(b, nqt, NSA_G, NSA_DK, NSA_R * tq)
    z = jnp.zeros_like(x)
    return jnp.stack([jnp.concatenate([x[:, :, 0], z[:, :, 0]], axis=2),
                      jnp.concatenate([z[:, :, 1], x[:, :, 1]], axis=2)], axis=2).astype(BF16)


def _gate_to_lanes(gate, tq_real, tq):
    b, t, _ = gate.shape
    nqt = t // tq_real
    x = gate[..., :COL_GATE].reshape(b, nqt, tq_real, 3, NSA_G, NSA_R).transpose(0, 1, 4, 3, 5, 2)
    x = jnp.pad(x, ((0, 0),) * 5 + ((0, tq - tq_real),)).reshape(b, nqt, NSA_G, 3, NSA_R * tq)
    return jnp.pad(x, ((0, 0), (0, 0), (0, 0), (0, SUBLANES - 3), (0, 0)))


def _o_from_lanes(o_t, tq_real, tq):
    b, nqt = o_t.shape[:2]
    x = o_t.reshape(b, nqt, NSA_G, NSA_DK, NSA_R, tq)[..., :tq_real]
    return x.transpose(0, 1, 5, 2, 4, 3).reshape(b, nqt * tq_real, NSA_WIDTH)


def _nsa_decode_kernel(pt_ref, pool_ref, kvn_ref, wck_ref, wcv_ref, wn_ref, qtz_ref, gate_ref, ov_ref, w1_ref, pe_ref, w2_ref,
                       o_ref, stg_c, stg_s, sem, ckb, cvb, skb, svt_ref, tmp, wkb, wvb, kc_ref, vct_ref, wvt_ref, selb_ref,
                       *, past, n_new, tq):
    nq = NSA_R * tq
    n_pages = past // PAGE_SIZE
    nch = (past + n_new) // CMP_STRIDE
    assert nch * CMP_STRIDE <= past
    n_slc = -(-(past + n_new) // SLC_BLOCK)
    s_pad = skb.shape[0]
    w_pad = wkb.shape[0]
    big = s_pad // DEC_TILES
    b = pl.program_id(0)
    nb = pl.num_programs(0)

    def page_copy(bb, p, which):
        page = pt_ref[bb * n_pages + p]
        stg = (stg_c, stg_s)[which]
        return pltpu.make_async_copy(pool_ref.at[page, pl.ds(2 * which, 2)], stg.at[p], sem.at[which])

    def start_fetch(bb, which):
        def body(p, carry):
            page_copy(bb, p, which).start()
            return carry
        lax.fori_loop(0, n_pages, body, 0)

    def wait_fetch(bb, which):
        def body(p, carry):
            page_copy(bb, p, which).wait()
            return carry
        lax.fori_loop(0, n_pages, body, 0)

    @pl.when(b == 0)
    def _first():
        start_fetch(0, 0)
        start_fetch(0, 1)

    nwin = wck_ref.shape[1]
    for buf, cache_ref, lane0 in ((wkb, wck_ref, 0), (wvb, wcv_ref, LANES)):
        buf[0:nwin, :] = cache_ref[0]
        buf[nwin:, :] = jnp.zeros((w_pad - nwin, LANES), F32)
        buf[nwin:nwin + n_new, :] = wn_ref[0, :, lane0:lane0 + LANES]
    for kt in range(w_pad // KEY_TILE):
        cols = slice(kt * KEY_TILE, (kt + 1) * KEY_TILE)
        wvt_ref[:, cols] = wvb[cols, :].T.astype(BF16)

    wait_fetch(b, 0)
    for p in range(n_pages):
        rows = slice(p * PAGE_SIZE, (p + 1) * PAGE_SIZE)
        ckb[rows, :] = stg_c[p, 0].T
        cvb[rows, :] = stg_c[p, 1].T

    @pl.when(b + 1 < nb)
    def _next_cmp():
        start_fetch(b + 1, 0)

    wait_fetch(b, 1)
    for p in range(n_pages):
        rows = slice(p * PAGE_SIZE, (p + 1) * PAGE_SIZE)
        skb[rows, :] = stg_s[p, 0].T.astype(BF16)
        svt_ref[:, rows] = stg_s[p, 1].astype(BF16)

    @pl.when(b + 1 < nb)
    def _next_slc():
        start_fetch(b + 1, 1)

    kc = _compress_pair(lambda s: ckb[pl.ds(s, nch, stride=CMP_STRIDE), :], w1_ref, pe_ref, w2_ref, 0, nch)
    vc = _compress_pair(lambda s: cvb[pl.ds(s, nch, stride=CMP_STRIDE), :], w1_ref, pe_ref, w2_ref, 1, nch)
    kc_ref[...] = kc.astype(BF16)
    vct_ref[...] = vc.T.astype(BF16)
    tmp[...] = jnp.zeros(tmp.shape, F32)
    tmp[0:n_new, :] = kvn_ref[0, :, 2 * LANES:3 * LANES]
    skb[past:past + PAGE_SIZE, :] = tmp[...].astype(BF16)
    tmp[0:n_new, :] = kvn_ref[0, :, 3 * LANES:4 * LANES]
    svt_ref[:, past:past + PAGE_SIZE] = tmp[...].T.astype(BF16)
    if s_pad > past + PAGE_SIZE:
        skb[past + PAGE_SIZE:, :] = jnp.zeros((s_pad - past - PAGE_SIZE, LANES), BF16)
        svt_ref[:, past + PAGE_SIZE:] = jnp.zeros((LANES, s_pad - past - PAGE_SIZE), BF16)

    qpos = past + lax.broadcasted_iota(jnp.int32, (1, nq), 1) % tq
    qtz = [qtz_ref[0, 0, g] for g in range(NSA_G)]
    o_cmp, scores = [], []
    for g in range(NSA_G):
        oc, sc = _cmp_branch(g, qtz[g], qpos, tq, nq, kc_ref, vct_ref, ov_ref, nch - 1)
        o_cmp.append(oc)
        scores.append(sc)
    _select_blocks(scores, n_slc, tq, nq, selb_ref)

    blocks_per_big = big // SLC_BLOCK
    carry = _attn_init(nq)
    for i in range(DEC_TILES):
        rows = slice(i * big, (i + 1) * big)
        k = skb[rows, :]
        biases = []
        for g in range(NSA_G):
            bg = _block_rows(selb_ref[g, i * blocks_per_big:(i + 1) * blocks_per_big, :], nq)
            if (i + 1) * big > past:
                bg = bg + _causal_bias(i * big + lax.broadcasted_iota(jnp.int32, (big, nq), 0), qpos)
            biases.append(bg)
        vts = [svt_ref[NSA_DK * g:NSA_DK * (g + 1), rows] for g in range(NSA_G)]
        carry = _attn_step(carry, k, qtz, vts, biases)
    o_slc = _attn_finish(carry)

    wb = _window_bias(past - nwin + lax.broadcasted_iota(jnp.int32, (w_pad, nq), 0), qpos)
    vts = [wvt_ref[NSA_DK * g:NSA_DK * (g + 1), :] for g in range(NSA_G)]
    o_win = _attn_finish(_attn_step(_attn_init(nq), wkb[...].astype(BF16), qtz, vts, [wb, wb]))

    for g in range(NSA_G):
        o_ref[0, 0, g] = _gated_sum(gate_ref[0, 0, g], o_cmp[g], o_slc[g], o_win[g])


def _nsa_decode(pool, page_table, kv_new, win_cache, win_new, qtz, gate_t, w1k, pek, w2k, *, tq):
    b, n_pages = page_table.shape
    n_new = kv_new.shape[1]
    nwin = win_cache.shape[1]
    past = n_pages * PAGE_SIZE
    nq = NSA_R * tq
    nch = (past + n_new) // CMP_STRIDE
    step_keys = DEC_TILES * KEY_TILE
    s_pad = -(-(past + n_new) // step_keys) * step_keys
    w_pad = -(-(nwin + n_new) // KEY_TILE) * KEY_TILE
    nj_pad = -(-(s_pad // SLC_BLOCK) // 16) * 16
    assert nwin % KEY_TILE == 0 and n_new <= LANES and (s_pad // DEC_TILES) % LANES == 0
    ov = _overlap_t(nj_pad, nch)
    kern = functools.partial(_nsa_decode_kernel, past=past, n_new=n_new, tq=tq)
    grid_spec = pltpu.PrefetchScalarGridSpec(
        num_scalar_prefetch=1,
        grid=(b,),
        in_specs=[
            pl.BlockSpec(memory_space=pl.ANY),
            pl.BlockSpec((1, n_new, COL_KV), lambda bi, pt: (bi, 0, 0)),
            pl.BlockSpec((1, nwin, LANES), lambda bi, pt: (bi, 0, 0)),
            pl.BlockSpec((1, nwin, LANES), lambda bi, pt: (bi, 0, 1)),
            pl.BlockSpec((1, n_new, COL_WIN), lambda bi, pt: (bi, 0, 0)),
            pl.BlockSpec((1, 1, NSA_G, LANES, nq), lambda bi, pt: (bi, 0, 0, 0, 0)),
            pl.BlockSpec((1, 1, NSA_G, SUBLANES, nq), lambda bi, pt: (bi, 0, 0, 0, 0)),
            pl.BlockSpec((nj_pad, nch), lambda bi, pt: (0, 0)),
            pl.BlockSpec(w1k.shape, lambda bi, pt: (0, 0, 0, 0)),
            pl.BlockSpec(pek.shape, lambda bi, pt: (0, 0, 0, 0)),
            pl.BlockSpec(w2k.shape, lambda bi, pt: (0, 0, 0)),
        ],
        out_specs=pl.BlockSpec((1, 1, NSA_G, NSA_DK, nq), lambda bi, pt: (bi, 0, 0, 0, 0)),
        scratch_shapes=[
            pltpu.VMEM((n_pages, 2, LANES, PAGE_SIZE), F32),
            pltpu.VMEM((n_pages, 2, LANES, PAGE_SIZE), F32),
            pltpu.SemaphoreType.DMA((2,)),
            pltpu.VMEM((past, LANES), F32),
            pltpu.VMEM((past, LANES), F32),
            pltpu.VMEM((s_pad, LANES), BF16),
            pltpu.VMEM((LANES, s_pad), BF16),
            pltpu.VMEM((PAGE_SIZE, LANES), F32),
            pltpu.VMEM((w_pad, LANES), F32),
            pltpu.VMEM((w_pad, LANES), F32),
            pltpu.VMEM((nch, LANES), BF16),
            pltpu.VMEM((LANES, nch), BF16),
            pltpu.VMEM((LANES, w_pad), BF16),
            pltpu.VMEM((NSA_G, nj_pad, nq), F32),
        ],
    )
    return pl.pallas_call(
        kern,
        grid_spec=grid_spec,
        out_shape=jax.ShapeDtypeStruct((b, 1, NSA_G, NSA_DK, nq), F32),
        compiler_params=pltpu.CompilerParams(dimension_semantics=("arbitrary",), vmem_limit_bytes=VMEM_LIMIT),
        name="nsa_decode",
    )(page_table.reshape(-1), pool, kv_new, win_cache, win_cache, win_new, qtz, gate_t, ov, w1k, pek, w2k)


FF_TILE = 256


def _rmsnorm(x, w):
    return (x * lax.rsqrt(jnp.mean(x * x, axis=-1, keepdims=True) + EPS)) * w


def _gelu_tanh(c):
    return c * (0.5 * (1.0 + jnp.tanh(0.7978845608028654 * (c + 0.044715 * (c * c * c)))))


def _ffn_kernel(x_ref, ohg_ref, onsa_ref, past_ref, wout_ref, n2_ref, wg_ref, wv_ref, cw_ref, cb_ref, wd_ref, nf_ref,
                y_ref, conv_ref, carry_ref, act_ref, *, tm):
    ti = pl.program_id(1)

    @pl.when(ti == 0)
    def _init():
        carry_ref[0:CONV_W - 1, :] = past_ref[0]

    x1 = (x_ref[0]
          + jnp.dot(ohg_ref[0].astype(BF16), wout_ref[0:HG_WIDTH, :], preferred_element_type=F32)
          + jnp.dot(onsa_ref[0].astype(BF16), wout_ref[HG_WIDTH:, :], preferred_element_type=F32))
    h = _rmsnorm(x1, n2_ref[...]).astype(BF16)
    row = lax.broadcasted_iota(jnp.int32, (tm, FF_TILE), 0)
    for j in range(D_FF // FF_TILE):
        cols = slice(j * FF_TILE, (j + 1) * FF_TILE)
        u = jnp.dot(h, wg_ref[:, cols], preferred_element_type=F32)
        val = jnp.dot(h, wv_ref[:, cols], preferred_element_type=F32)
        c0 = carry_ref[0:1, cols]
        c1 = carry_ref[1:2, cols]
        up1 = jnp.where(row == 0, c1, pltpu.roll(u, 1, axis=0))
        up2 = jnp.where(row == 0, c0, jnp.where(row == 1, c1, pltpu.roll(u, 2, axis=0)))
        c = cb_ref[:, cols] + up2 * cw_ref[0:1, cols] + up1 * cw_ref[1:2, cols] + u * cw_ref[2:3, cols]
        tail = u[tm - (CONV_W - 1):tm, :]
        carry_ref[0:CONV_W - 1, cols] = tail
        conv_ref[0, :, cols] = tail
        act_ref[:, cols] = (_gelu_tanh(c) * val).astype(BF16)
    y = jnp.dot(act_ref[...], wd_ref[...], preferred_element_type=F32)
    y_ref[0] = _rmsnorm(x1 + y, nf_ref[...])


def _ffn(x, o_hg, o_nsa, conv_past, wts, *, tm):
    b, t, _ = x.shape
    assert t % tm == 0 and D_FF % FF_TILE == 0
    wout, n2, wg, wv, cw, cb, wd, nf = wts
    const2 = lambda bi, ti: (0, 0)
    one = pl.Buffered(1)
    return pl.pallas_call(
        functools.partial(_ffn_kernel, tm=tm),
        grid=(b, t // tm),
        in_specs=[
            pl.BlockSpec((1, tm, D_MODEL), lambda bi, ti: (bi, ti, 0)),
            pl.BlockSpec((1, tm, HG_WIDTH), lambda bi, ti: (bi, ti, 0)),
            pl.BlockSpec((1, tm, NSA_WIDTH), lambda bi, ti: (bi, ti, 0)),
            pl.BlockSpec((1, CONV_W - 1, D_FF), lambda bi, ti: (bi, 0, 0)),
            pl.BlockSpec(wout.shape, const2, pipeline_mode=one),
            pl.BlockSpec(n2.shape, const2),
            pl.BlockSpec(wg.shape, const2, pipeline_mode=one),
            pl.BlockSpec(wv.shape, const2, pipeline_mode=one),
            pl.BlockSpec(cw.shape, const2),
            pl.BlockSpec(cb.shape, const2),
            pl.BlockSpec(wd.shape, const2, pipeline_mode=one),
            pl.BlockSpec(nf.shape, const2),
        ],
        out_specs=[
            pl.BlockSpec((1, tm, D_MODEL), lambda bi, ti: (bi, ti, 0)),
            pl.BlockSpec((1, CONV_W - 1, D_FF), lambda bi, ti: (bi, 0, 0)),
        ],
        out_shape=[jax.ShapeDtypeStruct((b, t, D_MODEL), F32), jax.ShapeDtypeStruct((b, CONV_W - 1, D_FF), F32)],
        scratch_shapes=[pltpu.VMEM((SUBLANES, D_FF), F32), pltpu.VMEM((tm, D_FF), BF16)],
        compiler_params=pltpu.CompilerParams(dimension_semantics=("arbitrary", "arbitrary"), vmem_limit_bytes=VMEM_LIMIT),
        name="ffn",
    )(x, o_hg, o_nsa, conv_past, wout, n2, wg, wv, cw, cb, wd, nf)


def _ffn_steps_kernel(x_ref, om_ref, past_ref, wout_ref, n2_ref, wg_ref, wv_ref, cw_ref, cb_ref, wd_ref, nf_ref,
                      y_ref, conv_ref, *, n_steps, nb):
    x1 = x_ref[...] + jnp.dot(om_ref[...].astype(BF16), wout_ref[...], preferred_element_type=F32)
    h = _rmsnorm(x1, n2_ref[...]).astype(BF16)
    acc = jnp.zeros((n_steps * nb, D_MODEL), F32)
    for j in range(D_FF // FF_TILE):
        cols = slice(j * FF_TILE, (j + 1) * FF_TILE)
        u = jnp.dot(h, wg_ref[:, cols], preferred_element_type=F32)
        val = jnp.dot(h, wv_ref[:, cols], preferred_element_type=F32)
        taps = [past_ref[k, :, cols] for k in range(CONV_W - 1)] + [u[t * nb:(t + 1) * nb, :] for t in range(n_steps)]
        cs = []
        for t in range(n_steps):
            c = cb_ref[:, cols]
            for k in range(CONV_W):
                c = c + taps[t + k] * cw_ref[k:k + 1, cols]
            cs.append(c)
        for k in range(CONV_W - 1):
            conv_ref[k, :, cols] = taps[n_steps + k]
        c_all = jnp.concatenate(cs, axis=0)
        acc = acc + jnp.dot((_gelu_tanh(c_all) * val).astype(BF16), wd_ref[cols, :], preferred_element_type=F32)
    y_ref[...] = _rmsnorm(x1 + acc, nf_ref[...])


def _ffn_steps(x_t, omix_t, past_t, wts, *, n_steps, nb):
    return pl.pallas_call(
        functools.partial(_ffn_steps_kernel, n_steps=n_steps, nb=nb),
        out_shape=[jax.ShapeDtypeStruct((n_steps * nb, D_MODEL), F32), jax.ShapeDtypeStruct((CONV_W - 1, nb, D_FF), F32)],
        compiler_params=pltpu.CompilerParams(vmem_limit_bytes=VMEM_LIMIT),
        name="ffn_steps",
    )(x_t, omix_t, past_t, *wts)


def kernel(x_prompt, x_sample, cache_nsa_kv, cache_win_kv, state_hgrn, state_ffn_conv, page_table, norm1, w_in, hg_lb_logits, hg_norm, cmp_pe, cmp_w1, cmp_w2, w_out, norm2, w_gate, w_val, conv_w, conv_b, w_down, norm_f):
    B, T, _ = x_prompt.shape
    Bd, Td, _ = x_sample.shape
    depth = norm1.shape[0]
    assert depth == 1
    l = 0
    w_in_bf = jnp.pad(w_in[l], ((0, 0), (0, _C_GATE - D_IN))).astype(BF16)
    w1k, pek, w2k = _prep_cmp_weights(cmp_pe[l], cmp_w1[l], cmp_w2[l])
    ffn_w = (w_out[l].astype(BF16), norm2[l].reshape(1, D_MODEL), w_gate[l].astype(BF16), w_val[l].astype(BF16),
             conv_w[l], conv_b[l].reshape(1, D_FF), w_down[l].astype(BF16), norm_f.reshape(1, D_MODEL))

    z_hg, q_p, kv_p, win_p, gate_p = _in_proj(x_prompt.reshape(B * T, D_MODEL), norm1[l], w_in_bf)
    o_hg_p, s_p = _hgrn(z_hg.reshape(B, T, _C_HG), hg_lb_logits, hg_norm[l],
                        jnp.zeros((B, HG_HEADS, HG_DK, HG_DK), F32), chunk=HG_CHUNK, rows_per_step=512, t_valid=None,
                        heads_per_step=HG_HEADS)
    kv_p = kv_p.reshape(B, T, COL_KV)
    win_p = win_p.reshape(B, T, COL_WIN)
    o_nsa_p = _nsa_prompt(kv_p, win_p, q_p.reshape(B, T, NSA_WIDTH), gate_p.reshape(B, T, LANES), w1k, pek, w2k,
                          tq=KEY_TILE)
    y_p, conv_p = _ffn(x_prompt, o_hg_p, o_nsa_p, jnp.zeros((B, CONV_W - 1, D_FF), F32), ffn_w, tm=512)

    z_hg, q_s, kv_s, win_s, gate_s = _in_proj(x_sample.reshape(Bd * Td, D_MODEL), norm1[l], w_in_bf)
    t_pad = HG_SUB
    z_pad = jnp.pad(z_hg.reshape(Bd, Td, _C_HG), ((0, 0), (0, t_pad - Td), (0, 0)))
    o_hg_s, s_s = _hgrn(z_pad, hg_lb_logits, hg_norm[l], state_hgrn[l], chunk=t_pad, rows_per_step=t_pad, t_valid=Td,
                        heads_per_step=HG_HEADS)
    o_hg_s = o_hg_s[:, :Td]
    tqs = LANES // NSA_R
    kv_s = kv_s.reshape(Bd, Td, COL_KV)
    win_s = win_s.reshape(Bd, Td, COL_WIN)
    n_pool = cache_nsa_kv.shape[1]
    nwin = cache_win_kv.shape[2]
    pool_t = jnp.transpose(cache_nsa_kv[l], (0, 2, 3, 4, 1)).reshape(n_pool, 4, NSA_G * NSA_DK, PAGE_SIZE)
    o_t = _nsa_decode(pool_t, page_table, kv_s,
                      cache_win_kv[l].reshape(Bd, nwin, COL_WIN), win_s,
                      _q_to_lanes(q_s.reshape(Bd, Td, NSA_WIDTH), Td, tqs),
                      _gate_to_lanes(gate_s.reshape(Bd, Td, LANES), Td, tqs), w1k, pek, w2k, tq=tqs)
    o_nsa_s = _o_from_lanes(o_t, Td, tqs)
    omix_t = jnp.concatenate([o_hg_s, o_nsa_s], axis=-1).transpose(1, 0, 2).reshape(Td * Bd, D_MODEL)
    y_t, conv_t = _ffn_steps(x_sample.transpose(1, 0, 2).reshape(Td * Bd, D_MODEL), omix_t,
                             state_ffn_conv[l].transpose(1, 0, 2), ffn_w, n_steps=Td, nb=Bd)
    y_s = y_t.reshape(Td, Bd, D_MODEL).transpose(1, 0, 2)
    conv_s = conv_t.transpose(1, 0, 2)

    win_keep_p = min(WINDOW, T)
    win_all_s = jnp.concatenate([cache_win_kv[l].reshape(Bd, nwin, COL_WIN), win_s], axis=1)[:, -nwin:]
    return (y_p, y_s,
            kv_p.reshape(1, B, T, 4, NSA_G, NSA_DK), kv_s.reshape(1, Bd, Td, 4, NSA_G, NSA_DK),
            win_p[:, T - win_keep_p:].reshape(1, B, win_keep_p, 2, NSA_G, NSA_DK),
            win_all_s.reshape(1, Bd, nwin, 2, NSA_G, NSA_DK),
            s_p[None], s_s[None], conv_p[None], conv_s[None])
```

```python
import functools

import numpy as np
import jax
import jax.numpy as jnp
from jax import lax
from jax.experimental import pallas as pl
from jax.experimental.pallas import tpu as pltpu

F32 = jnp.float32
BF16 = jnp.bfloat16

D_MODEL = 1024
PAGE_SIZE = 128
HG_WIDTH = 512
HG_HEADS = 4
HG_DK = 128
HG_CHUNK = 64
NSA_WIDTH = 512
NSA_HEADS = 8
NSA_DK = 64
NSA_G = 2
NSA_R = 4
CMP_STRIDE = 16
CMP_BLOCK = 32
CMP_HIDDEN = 256
SLC_BLOCK = 64
SLC_TOPK = 16
WINDOW = 512
D_FF = 2816
CONV_W = 3
EPS = 1e-6
COL_KV = 512
COL_WIN = 256
COL_GATE = 24

LANES = 128
SUBLANES = 8
HG_SUB = 16
HG_UNROLL = 8
HG_MASKED_EXP = -1e30
VMEM_LIMIT = 56 * 1024 * 1024

_C_HG = 4 * HG_WIDTH
_C_Q = _C_HG + NSA_WIDTH
_C_KV = _C_Q + COL_KV
_C_WIN = _C_KV + COL_WIN
_C_GATE = _C_WIN + LANES
D_IN = _C_WIN + COL_GATE


def _sigmoid(x):
    return 1.0 / (1.0 + jnp.exp(-x))


def _split3_bf16(x):
    hi = x.astype(BF16)
    r1 = x - hi.astype(F32)
    mid = r1.astype(BF16)
    lo = (r1 - mid.astype(F32)).astype(BF16)
    return hi, mid, lo


def _in_proj_kernel(x_ref, g_ref, w_ref, hg_ref, q_ref, kv_ref, win_ref, gate_ref, *kvt_ref):
    x = x_ref[...]
    ms = jnp.mean(x * x, axis=-1, keepdims=True)
    h = ((x * lax.rsqrt(ms + EPS)) * g_ref[...]).astype(BF16)
    hg_ref[...] = jnp.dot(h, w_ref[:, 0:_C_HG], preferred_element_type=F32)
    q_ref[...] = jnp.dot(h, w_ref[:, _C_HG:_C_Q], preferred_element_type=F32)
    kv = jnp.dot(h, w_ref[:, _C_Q:_C_KV], preferred_element_type=F32)
    kv_ref[...] = kv
    win_ref[...] = jnp.dot(h, w_ref[:, _C_KV:_C_WIN], preferred_element_type=F32)
    gate_ref[...] = jnp.dot(h, w_ref[:, _C_WIN:_C_GATE], preferred_element_type=F32)
    if kvt_ref:
        kvt_ref[0][0] = kv.T


def _in_proj(x2d, norm_w, w_in_bf, *, seq_for_kv_t=None):
    n = x2d.shape[0]
    tm = 512
    assert n % tm == 0
    widths = (_C_HG, NSA_WIDTH, COL_KV, COL_WIN, LANES)
    out_specs = [pl.BlockSpec((tm, w), lambda i: (i, 0)) for w in widths]
    out_shape = [jax.ShapeDtypeStruct((n, w), F32) for w in widths]
    if seq_for_kv_t is not None:
        tiles = seq_for_kv_t // tm
        assert seq_for_kv_t % tm == 0
        out_specs.append(pl.BlockSpec((1, COL_KV, tm), lambda i: (i // tiles, 0, i % tiles)))
        out_shape.append(jax.ShapeDtypeStruct((n // seq_for_kv_t, COL_KV, seq_for_kv_t), F32))
    return pl.pallas_call(
        _in_proj_kernel,
        grid=(n // tm,),
        in_specs=[
            pl.BlockSpec((tm, D_MODEL), lambda i: (i, 0)),
            pl.BlockSpec((1, D_MODEL), lambda i: (0, 0)),
            pl.BlockSpec((D_MODEL, _C_GATE), lambda i: (0, 0)),
        ],
        out_specs=out_specs,
        out_shape=out_shape,
        compiler_params=pltpu.CompilerParams(dimension_semantics=("arbitrary",), vmem_limit_bytes=VMEM_LIMIT),
        name="in_proj",
    )(x2d, norm_w.reshape(1, D_MODEL), w_in_bf)


def _hgrn_kernel(q_ref, f_ref, i_ref, g_ref, lb_ref, nw_ref, s0_ref, e_ref, o_ref, sfin_ref, st_ref,
                 *, chunk, rows_per_step, t_valid, layer):
    C = chunk
    nsb = C // HG_SUB
    ti = pl.program_id(2)

    hps = st_ref.shape[0]

    @pl.when(ti == 0)
    def _init():
        for hh in range(hps):
            st_ref[hh] = s0_ref[0, hh].T

    lg = lb_ref[...]
    le = jnp.exp(lg - jnp.max(lg, axis=0, keepdims=True))
    lb_all = jnp.sum(le[0:layer + 1, :], axis=0, keepdims=True) / jnp.sum(le, axis=0, keepdims=True)
    nw_all = nw_ref[...]
    row_i = lax.broadcasted_iota(jnp.int32, (C, C), 0)
    col_i = lax.broadcasted_iota(jnp.int32, (C, C), 1)
    tri_bf = jnp.where(col_i <= row_i, 1.0, 0.0).astype(BF16)
    diag_mask = ((row_i // HG_SUB) == (col_i // HG_SUB)) & (col_i <= row_i)
    off_mask = (col_i // HG_SUB) < (row_i // HG_SUB)
    trel = lax.broadcasted_iota(jnp.int32, (C, HG_DK), 0) % HG_SUB
    nt_dims = (((1,), (1,)), ((), ()))
    tn_dims = (((0,), (0,)), ((), ()))

    def do_chunk(hh, c):
        lanes = slice(hh * HG_DK, (hh + 1) * HG_DK)
        lb = lb_all[:, lanes]
        one_m_lb = 1.0 - lb
        nw = nw_all[:, lanes]
        r0 = pl.multiple_of(c * C, C)
        rows = pl.ds(r0, C)
        fr = f_ref[0, rows, lanes]
        qr = q_ref[0, rows, lanes]
        v = i_ref[0, rows, lanes]
        gr = g_ref[0, rows, lanes]
        f = lb + one_m_lb * _sigmoid(fr)
        kk = one_m_lb * _sigmoid(-fr)
        logf = jnp.log2(f)
        if t_valid is not None:
            tok = ti * rows_per_step + r0 + lax.broadcasted_iota(jnp.int32, (C, HG_DK), 0)
            ok = tok < t_valid
            logf = jnp.where(ok, logf, 0.0)
            kk = jnp.where(ok, kk, 0.0)
        q = qr * _sigmoid(qr)
        hi, mid, lo = _split3_bf16(logf)
        a = (jnp.dot(tri_bf, hi, preferred_element_type=F32)
             + jnp.dot(tri_bf, mid, preferred_element_type=F32)
             + jnp.dot(tri_bf, lo, preferred_element_type=F32))
        st = st_ref[hh]
        vb = v.astype(BF16)
        o = lax.dot_general((q * jnp.exp2(a)).astype(BF16), st.astype(BF16), nt_dims, preferred_element_type=F32)

        if nsb > 1:
            parts = [jnp.zeros((HG_SUB, C), F32)]
            for i in range(1, nsb):
                bi = a[HG_SUB * i - 1:HG_SUB * i, :]
                qi = q[HG_SUB * i:HG_SUB * (i + 1), :] * jnp.exp2(a[HG_SUB * i:HG_SUB * (i + 1), :] - bi)
                ki = kk * jnp.exp2(jnp.minimum(bi - a, 0.0))
                parts.append(lax.dot_general(qi.astype(BF16), ki.astype(BF16), nt_dims, preferred_element_type=F32))
            att = jnp.where(off_mask, jnp.concatenate(parts, axis=0), 0.0)
        else:
            att = jnp.zeros((C, C), F32)

        a3 = a.reshape(nsb, HG_SUB, HG_DK)
        k3 = kk.reshape(nsb, HG_SUB, HG_DK)
        zs = []
        for j in range(HG_SUB):
            aj = jnp.broadcast_to(a3[:, j:j + 1, :], (nsb, HG_SUB, HG_DK)).reshape(C, HG_DK)
            kj = jnp.broadcast_to(k3[:, j:j + 1, :], (nsb, HG_SUB, HG_DK)).reshape(C, HG_DK)
            expo = a - aj if j == 0 else jnp.where(trel >= j, a - aj, HG_MASKED_EXP)
            zs.append((q * kj * jnp.exp2(expo)).astype(BF16))
        zcat = jnp.concatenate(zs, axis=1)
        att_d = jnp.dot(zcat, e_ref[...], preferred_element_type=F32)
        att = att + jnp.where(diag_mask, att_d, 0.0)
        o = o + jnp.dot(att.astype(BF16), vb, preferred_element_type=F32)

        a_end = a[C - 1:C, :]
        kdec = kk * jnp.exp2(a_end - a)
        st_ref[hh] = st * jnp.exp2(a_end) + lax.dot_general(vb, kdec.astype(BF16), tn_dims, preferred_element_type=F32)

        on = (o * lax.rsqrt(jnp.mean(o * o, axis=-1, keepdims=True) + EPS)) * nw
        o_ref[0, rows, lanes] = on * (gr * _sigmoid(gr))

    n_chunks = rows_per_step // C
    if n_chunks == 1:
        for hh in range(hps):
            do_chunk(hh, 0)
    else:
        for hh in range(hps):
            def chunk_body(c, carry, hh=hh):
                do_chunk(hh, c)
                return carry
            lax.fori_loop(0, n_chunks, chunk_body, 0, unroll=min(n_chunks, HG_UNROLL))

    @pl.when(ti == pl.num_programs(2) - 1)
    def _fin():
        for hh in range(hps):
            sfin_ref[0, hh] = st_ref[hh].T


def _hgrn(z_hg, lb_logits, norm_w, s0, *, chunk, rows_per_step, t_valid, heads_per_step=1, layer=0):
    b, t, _ = z_hg.shape
    n_lb = lb_logits.shape[0]
    hps = heads_per_step
    assert t % rows_per_step == 0 and rows_per_step % chunk == 0 and chunk % HG_SUB == 0 and HG_HEADS % hps == 0
    nt = t // rows_per_step
    hblocks = HG_HEADS // hps
    width = hps * HG_DK
    sub = np.arange(chunk) % HG_SUB
    e_mat = jnp.asarray((np.repeat(np.arange(HG_SUB), HG_DK)[:, None] == sub[None, :]).astype(np.float32), BF16)

    def zspec(k):
        return pl.BlockSpec((1, rows_per_step, width), lambda bi, h, ti, k=k: (bi, ti, k * hblocks + h))

    kern = functools.partial(_hgrn_kernel, chunk=chunk, rows_per_step=rows_per_step, t_valid=t_valid, layer=layer)
    return pl.pallas_call(
        kern,
        grid=(b, hblocks, nt),
        in_specs=[
            zspec(0), zspec(1), zspec(2), zspec(3),
            pl.BlockSpec((n_lb, width), lambda bi, h, ti: (0, h)),
            pl.BlockSpec((1, width), lambda bi, h, ti: (0, h)),
            pl.BlockSpec((1, hps, HG_DK, HG_DK), lambda bi, h, ti: (bi, h, 0, 0)),
            pl.BlockSpec((HG_SUB * HG_DK, chunk), lambda bi, h, ti: (0, 0)),
        ],
        out_specs=[
            pl.BlockSpec((1, rows_per_step, width), lambda bi, h, ti: (bi, ti, h)),
            pl.BlockSpec((1, hps, HG_DK, HG_DK), lambda bi, h, ti: (bi, h, 0, 0)),
        ],
        out_shape=[jax.ShapeDtypeStruct((b, t, HG_WIDTH), F32), jax.ShapeDtypeStruct((b, HG_HEADS, HG_DK, HG_DK), F32)],
        scratch_shapes=[pltpu.VMEM((hps, HG_DK, HG_DK), F32)],
        compiler_params=pltpu.CompilerParams(dimension_semantics=("arbitrary", "arbitrary", "arbitrary"),
                                             vmem_limit_bytes=VMEM_LIMIT),
        name="hgrn2",
    )(z_hg, z_hg, z_hg, z_hg, lb_logits, norm_w.reshape(1, HG_WIDTH), s0, e_mat)


KEY_TILE = 256
CMP_PACK = 4
DEC_TILES = 3
LOG2E = 1.4426950408889634
Q_SCALE = LOG2E * NSA_DK ** -0.5
NEG_INF = float("-inf")


def _compress_pair(load_xs, w1_ref, pe_ref, w2_ref, slot, nch):
    r = jnp.zeros((SUBLANES, 2 * CMP_HIDDEN), F32)
    for u in range(CMP_STRIDE // CMP_PACK):
        r = r + jnp.dot(pe_ref[slot, u], w1_ref[slot, u], preferred_element_type=F32)
    bias = r[0:1, 0:CMP_HIDDEN] + r[1:2, CMP_HIDDEN:2 * CMP_HIDDEN]
    outs = []
    for g in range(NSA_G):
        acc = jnp.zeros((nch, 2 * CMP_HIDDEN), F32)
        for u in range(CMP_STRIDE // CMP_PACK):
            xg = jnp.concatenate([load_xs(CMP_PACK * u + i)[:, NSA_DK * g:NSA_DK * (g + 1)] for i in range(CMP_PACK)],
                                 axis=1).astype(BF16)
            acc = acc + jnp.dot(xg, w1_ref[slot, u], preferred_element_type=F32)
        h = acc[:, 0:CMP_HIDDEN] + pltpu.roll(acc[:, CMP_HIDDEN:2 * CMP_HIDDEN], nch - 1, axis=0) + bias
        hs = (h * _sigmoid(h)).astype(BF16)
        outs.append(jnp.dot(hs, w2_ref[slot], preferred_element_type=F32))
    return jnp.concatenate(outs, axis=1)


def _rank_bias(score, n_slc):
    nj, w = score.shape
    jrow = lax.broadcasted_iota(jnp.int32, (SUBLANES, w), 0)
    slabs = [score[SUBLANES * v:SUBLANES * (v + 1), :] for v in range(nj // SUBLANES)]
    ranks = [jnp.zeros((SUBLANES, w), F32) for _ in slabs]
    for jp in range(n_slc):
        row = score[jp:jp + 1, :]
        for v, slab in enumerate(slabs):
            lo = SUBLANES * v
            if lo > jp:
                beats = row >= slab
            elif lo + SUBLANES - 1 <= jp:
                beats = row > slab
            else:
                beats = (row > slab) | ((row == slab) & (jrow > jp - lo))
            ranks[v] = ranks[v] + jnp.where(beats, 1.0, 0.0)
    rank = jnp.concatenate(ranks, axis=0)
    return jnp.where((rank < float(SLC_TOPK)) & (score > NEG_INF), 0.0, NEG_INF)


def _cmp_branch(g, qtz, qpos, tq, nq, kc_ref, vct_ref, ov_ref, n_blk):
    nb_pad = kc_ref.shape[0]
    nj_pad = ov_ref.shape[0]
    s = jnp.dot(kc_ref[...], qtz, preferred_element_type=F32)
    n_idx = lax.broadcasted_iota(jnp.int32, (nb_pad, nq), 0)
    cmask = (n_idx * CMP_STRIDE + (CMP_BLOCK - 1) <= qpos) & (n_idx < n_blk)
    s = jnp.where(cmask, s, NEG_INF)
    m = jnp.max(s, axis=0, keepdims=True)
    m = jnp.where(m == NEG_INF, 0.0, m)
    e = jnp.exp2(s - m)
    d = jnp.sum(e, axis=0, keepdims=True)
    p = e / jnp.where(d > 0.0, d, 1.0)
    o_cmp = jnp.dot(vct_ref[NSA_DK * g:NSA_DK * (g + 1), :], p.astype(BF16), preferred_element_type=F32)

    if tq % LANES == 0:
        ps = p[:, 0:tq]
        for r in range(1, NSA_R):
            ps = ps + p[:, r * tq:(r + 1) * tq]
        qpos_w = qpos[:, 0:tq]
    else:
        assert nq == LANES and NSA_R * tq == LANES
        ps = p
        for r in range(1, NSA_R):
            ps = ps + pltpu.roll(p, r * tq, axis=1)
        qpos_w = qpos
    w = ps.shape[1]
    hi, mid, lo = _split3_bf16(ps)
    ov = ov_ref[...]
    imp = (jnp.dot(ov, hi, preferred_element_type=F32) + jnp.dot(ov, mid, preferred_element_type=F32)
           + jnp.dot(ov, lo, preferred_element_type=F32))
    j_idx = lax.broadcasted_iota(jnp.int32, (nj_pad, w), 0)
    cur = qpos_w // SLC_BLOCK
    forced = (j_idx == 0) | (j_idx == cur) | (j_idx == cur - 1)
    score = jnp.where(forced, jnp.inf, jnp.where(j_idx <= cur, imp, NEG_INF))
    return o_cmp, score


def _select_blocks(scores, n_slc, tq, nq, selb_ref):
    w = scores[0].shape[1]
    if w == nq:
        assert NSA_G == 2 and nq == LANES
        lane = lax.broadcasted_iota(jnp.int32, scores[0].shape, 1)
        low = lane < LANES // 2
        bias = _rank_bias(jnp.where(low, scores[0], scores[1]), n_slc)
        swapped = pltpu.roll(bias, LANES // 2, axis=1)
        selb_ref[0] = jnp.where(low, bias, swapped)
        selb_ref[1] = jnp.where(low, swapped, bias)
        return
    bias = _rank_bias(jnp.concatenate(scores, axis=1), n_slc)
    for g in range(NSA_G):
        selb_ref[g] = jnp.concatenate([bias[:, g * w:(g + 1) * w]] * (nq // w), axis=1)


def _block_rows(rows, nq):
    n = rows.shape[0]
    return jnp.broadcast_to(rows[:, None, :], (n, SLC_BLOCK, nq)).reshape(n * SLC_BLOCK, nq)


def _attn_step(carry, k, qtz, vts, biases):
    out = []
    for g in range(NSA_G):
        m, l, acc = carry[g]
        s = jnp.dot(k, qtz[g], preferred_element_type=F32) + biases[g]
        m_new = jnp.maximum(m, jnp.max(s, axis=0, keepdims=True))
        m_safe = jnp.where(m_new == NEG_INF, 0.0, m_new)
        alpha = jnp.exp2(m - m_safe)
        p = jnp.exp2(s - m_safe)
        l = l * alpha + jnp.sum(p, axis=0, keepdims=True)
        acc = acc * alpha + jnp.dot(vts[g], p.astype(BF16), preferred_element_type=F32)
        out.append((m_new, l, acc))
    return tuple(out)


def _attn_init(nq):
    return tuple((jnp.full((1, nq), NEG_INF, F32), jnp.zeros((1, nq), F32), jnp.zeros((NSA_DK, nq), F32))
                 for _ in range(NSA_G))


def _attn_finish(carry):
    return [acc / jnp.where(l > 0.0, l, 1.0) for (_, l, acc) in carry]


def _causal_bias(kpos, qpos):
    return jnp.where(kpos <= qpos, 0.0, NEG_INF)


def _window_bias(kpos, qpos):
    d = kpos - qpos
    return jnp.where((d <= 0) & (d > -WINDOW), 0.0, NEG_INF)


def _gated_sum(gate_raw, o_cmp, o_slc, o_win):
    gt = _sigmoid(gate_raw)
    return gt[0:1, :] * o_cmp + gt[1:2, :] * o_slc + gt[2:3, :] * o_win


def _nsa_prompt_kernel(ck_ref, cv_ref, sk_ref, sv_ref, wk_ref, wv_ref, q_ref, gate_ref, ov_ref, w1_ref, pe_ref, w2_ref,
                       o_ref, kc_ref, vct_ref, svt_ref, wvt_ref, selb_ref, *, seq, tq):
    nq = NSA_R * tq
    nch = seq // CMP_STRIDE
    n_tiles = seq // KEY_TILE
    n_slc = seq // SLC_BLOCK
    blocks_per_tile = KEY_TILE // SLC_BLOCK
    assert 2 * blocks_per_tile == SUBLANES
    qt = pl.program_id(1)

    @pl.when(qt == 0)
    def _prep():
        kc = _compress_pair(lambda s: ck_ref[0, pl.ds(s, nch, stride=CMP_STRIDE), :], w1_ref, pe_ref, w2_ref, 0, nch)
        vc = _compress_pair(lambda s: cv_ref[0, pl.ds(s, nch, stride=CMP_STRIDE), :], w1_ref, pe_ref, w2_ref, 1, nch)
        kc_ref[...] = kc.astype(BF16)
        vct_ref[...] = vc.T.astype(BF16)
        for kt in range(n_tiles):
            rows = slice(kt * KEY_TILE, (kt + 1) * KEY_TILE)
            svt_ref[kt] = sv_ref[0, rows, :].T.astype(BF16)
            wvt_ref[kt] = wv_ref[0, rows, :].T.astype(BF16)

    t0 = qt * tq
    qpos = t0 + lax.broadcasted_iota(jnp.int32, (1, nq), 1) % tq
    hi_tile = (t0 + tq - 1) // KEY_TILE + 1
    win_lo = jnp.maximum(t0 - (WINDOW - 1), 0) // KEY_TILE
    krow = lax.broadcasted_iota(jnp.int32, (KEY_TILE, nq), 0)

    gate_t = gate_ref[0].T
    zeros_half = jnp.zeros((NSA_DK, nq), F32)
    qtz, gates = [], []
    for g in range(NSA_G):
        qg_t = q_ref[0, :, g * NSA_R * NSA_DK:(g + 1) * NSA_R * NSA_DK].T
        q_t = jnp.concatenate([qg_t[r * NSA_DK:(r + 1) * NSA_DK, :] for r in range(NSA_R)], axis=1) * Q_SCALE
        qtz.append(jnp.concatenate([q_t, zeros_half] if g == 0 else [zeros_half, q_t], axis=0).astype(BF16))
        gates.append(jnp.concatenate(
            [jnp.concatenate([gate_t[k * NSA_HEADS + g * NSA_R + r:k * NSA_HEADS + g * NSA_R + r + 1, :]
                              for r in range(NSA_R)], axis=1) for k in range(3)], axis=0))

    o_cmp, scores = [], []
    for g in range(NSA_G):
        oc, sc = _cmp_branch(g, qtz[g], qpos, tq, nq, kc_ref, vct_ref, ov_ref, nch - 1)
        o_cmp.append(oc)
        scores.append(sc)
    _select_blocks(scores, n_slc, tq, nq, selb_ref)

    def slc_step(kt, carry, causal):
        rows = pl.ds(pl.multiple_of(kt * KEY_TILE, KEY_TILE), KEY_TILE)
        k = sk_ref[0, rows, :].astype(BF16)
        biases = []
        for g in range(NSA_G):
            rows8 = selb_ref[g, pl.ds(pl.multiple_of((kt // 2) * SUBLANES, SUBLANES), SUBLANES), :]
            rows4 = jnp.where(kt % 2 == 0, rows8[0:blocks_per_tile, :], rows8[blocks_per_tile:, :])
            bg = _block_rows(rows4, nq)
            biases.append(bg + _causal_bias(kt * KEY_TILE + krow, qpos) if causal else bg)
        vts = [svt_ref[kt, NSA_DK * g:NSA_DK * (g + 1), :] for g in range(NSA_G)]
        return _attn_step(carry, k, qtz, vts, biases)

    def win_step(kt, carry):
        rows = pl.ds(pl.multiple_of(kt * KEY_TILE, KEY_TILE), KEY_TILE)
        k = wk_ref[0, rows, :].astype(BF16)
        wb = _window_bias(kt * KEY_TILE + krow, qpos)
        vts = [wvt_ref[kt, NSA_DK * g:NSA_DK * (g + 1), :] for g in range(NSA_G)]
        return _attn_step(carry, k, qtz, vts, [wb, wb])

    carry = lax.fori_loop(0, hi_tile - 1, lambda kt, c: slc_step(kt, c, False), _attn_init(nq))
    o_slc = _attn_finish(slc_step(hi_tile - 1, carry, True))
    o_win = _attn_finish(lax.fori_loop(win_lo, hi_tile, win_step, _attn_init(nq)))

    for g in range(NSA_G):
        o_t = _gated_sum(gates[g], o_cmp[g], o_slc[g], o_win[g])
        stack = jnp.concatenate([o_t[:, r * tq:(r + 1) * tq] for r in range(NSA_R)], axis=0)
        o_ref[0, :, g * NSA_R * NSA_DK:(g + 1) * NSA_R * NSA_DK] = stack.T


def _overlap_t(nj_pad, nb_pad):
    n = np.arange(nb_pad)[None, :]
    j = np.arange(nj_pad)[:, None]
    lo = np.maximum(n * CMP_STRIDE, j * SLC_BLOCK)
    hi = np.minimum(n * CMP_STRIDE + CMP_BLOCK, (j + 1) * SLC_BLOCK)
    return jnp.asarray(np.maximum(hi - lo, 0).astype(np.float32) / CMP_STRIDE, BF16)


def _nsa_prompt(kv_new, win_new, q, gate, w1k, pek, w2k, *, tq):
    b, seq, _ = kv_new.shape
    assert tq % LANES == 0 and tq <= KEY_TILE and seq % KEY_TILE == 0
    nq = NSA_R * tq
    nch = seq // CMP_STRIDE
    n_slc = seq // SLC_BLOCK
    n_tiles = seq // KEY_TILE
    ov = _overlap_t(n_slc, nch)
    kern = functools.partial(_nsa_prompt_kernel, seq=seq, tq=tq)
    return pl.pallas_call(
        kern,
        grid=(b, seq // tq),
        in_specs=[
            pl.BlockSpec((1, seq, LANES), lambda bi, qi: (bi, 0, 0)),
            pl.BlockSpec((1, seq, LANES), lambda bi, qi: (bi, 0, 1)),
            pl.BlockSpec((1, seq, LANES), lambda bi, qi: (bi, 0, 2)),
            pl.BlockSpec((1, seq, LANES), lambda bi, qi: (bi, 0, 3)),
            pl.BlockSpec((1, seq, LANES), lambda bi, qi: (bi, 0, 0)),
            pl.BlockSpec((1, seq, LANES), lambda bi, qi: (bi, 0, 1)),
            pl.BlockSpec((1, tq, NSA_WIDTH), lambda bi, qi: (bi, qi, 0)),
            pl.BlockSpec((1, tq, LANES), lambda bi, qi: (bi, qi, 0)),
            pl.BlockSpec((n_slc, nch), lambda bi, qi: (0, 0)),
            pl.BlockSpec(w1k.shape, lambda bi, qi: (0, 0, 0, 0)),
            pl.BlockSpec(pek.shape, lambda bi, qi: (0, 0, 0, 0)),
            pl.BlockSpec(w2k.shape, lambda bi, qi: (0, 0, 0)),
        ],
        out_specs=pl.BlockSpec((1, tq, NSA_WIDTH), lambda bi, qi: (bi, qi, 0)),
        out_shape=jax.ShapeDtypeStruct((b, seq, NSA_WIDTH), F32),
        scratch_shapes=[
            pltpu.VMEM((nch, LANES), BF16),
            pltpu.VMEM((LANES, nch), BF16),
            pltpu.VMEM((n_tiles, LANES, KEY_TILE), BF16),
            pltpu.VMEM((n_tiles, LANES, KEY_TILE), BF16),
            pltpu.VMEM((NSA_G, n_slc, nq), F32),
        ],
        compiler_params=pltpu.CompilerParams(dimension_semantics=("arbitrary", "arbitrary"), vmem_limit_bytes=VMEM_LIMIT),
        name="nsa_prompt",
    )(kv_new, kv_new, kv_new, kv_new, win_new, win_new, q, gate, ov, w1k, pek, w2k)


def _prep_cmp_weights(cmp_pe, cmp_w1, cmp_w2):
    groups = CMP_STRIDE // CMP_PACK
    w1k = jnp.concatenate([cmp_w1[:, 0], cmp_w1[:, 1]], axis=-1)
    w1k = w1k.reshape(2, groups, CMP_PACK * NSA_DK, 2 * CMP_HIDDEN).astype(BF16)
    pek = cmp_pe.reshape(2, 2, groups, CMP_PACK * NSA_DK).transpose(0, 2, 1, 3)
    pek = jnp.pad(pek, ((0, 0), (0, 0), (0, SUBLANES - 2), (0, 0))).astype(BF16)
    return w1k, pek, cmp_w2.astype(BF16)


def _q_to_lanes(q, tq_real, tq):
    b, t, _ = q.shape
    nqt = t // tq_real
    x = (q * Q_SCALE).reshape(b, nqt, tq_real, NSA_G, NSA_R, NSA_DK)
    x = x.transpose(0, 1, 3, 5, 4, 2)
    x = jnp.pad(x, ((0, 0),) * 5 + ((0, tq - tq_real),)).reshape(b, nqt, NSA_G, NSA_DK, NSA_R * tq)
    z = jnp.zeros_like(x)
    return jnp.stack([jnp.concatenate([x[:, :, 0], z[:, :, 0]], axis=2),
                      jnp.concatenate([z[:, :, 1], x[:, :, 1]], axis=2)], axis=2).astype(BF16)


def _gate_to_lanes(gate, tq_real, tq):
    b, t, _ = gate.shape
    nqt = t // tq_real
    x = gate[..., :COL_GATE].reshape(b, nqt, tq_real, 3, NSA_G, NSA_R).transpose(0, 1, 4, 3, 5, 2)
    x = jnp.pad(x, ((0, 0),) * 5 + ((0, tq - tq_real),)).reshape(b, nqt, NSA_G, 3, NSA_R * tq)
    return jnp.pad(x, ((0, 0), (0, 0), (0, 0), (0, SUBLANES - 3), (0, 0)))


def _o_from_lanes(o_t, tq_real, tq):
    b, nqt = o_t.shape[:2]
    x = o_t.reshape(b, nqt, NSA_G, NSA_DK, NSA_R, tq)[..., :tq_real]
    return x.transpose(0, 1, 5, 2, 4, 3).reshape(b, nqt * tq_real, NSA_WIDTH)


def _nsa_decode_kernel(pt_ref, pool_ref, kvn_ref, wck_ref, wcv_ref, wn_ref, qtz_ref, gate_ref, ov_ref, w1_ref, pe_ref, w2_ref,
                       o_ref, stg_c, stg_s, sem, ckb, cvb, skb, svt_ref, tmp, wkb, wvb, kc_ref, vct_ref, wvt_ref, selb_ref,
                       *, past, n_new, tq):
    nq = NSA_R * tq
    n_pages = past // PAGE_SIZE
    nch = (past + n_new) // CMP_STRIDE
    assert nch * CMP_STRIDE <= past
    n_slc = -(-(past + n_new) // SLC_BLOCK)
    s_pad = skb.shape[0]
    w_pad = wkb.shape[0]
    big = s_pad // DEC_TILES
    b = pl.program_id(0)
    nb = pl.num_programs(0)

    def page_copy(bb, p, which):
        page = pt_ref[bb * n_pages + p]
        stg = (stg_c, stg_s)[which]
        return pltpu.make_async_copy(pool_ref.at[page, pl.ds(2 * which, 2)], stg.at[p], sem.at[which])

    def start_fetch(bb, which):
        def body(p, carry):
            page_copy(bb, p, which).start()
            return carry
        lax.fori_loop(0, n_pages, body, 0)

    def wait_fetch(bb, which):
        def body(p, carry):
            page_copy(bb, p, which).wait()
            return carry
        lax.fori_loop(0, n_pages, body, 0)

    @pl.when(b == 0)
    def _first():
        start_fetch(0, 0)
        start_fetch(0, 1)

    nwin = wck_ref.shape[1]
    for buf, cache_ref, lane0 in ((wkb, wck_ref, 0), (wvb, wcv_ref, LANES)):
        buf[0:nwin, :] = cache_ref[0]
        buf[nwin:, :] = jnp.zeros((w_pad - nwin, LANES), F32)
        buf[nwin:nwin + n_new, :] = wn_ref[0, :, lane0:lane0 + LANES]
    for kt in range(w_pad // KEY_TILE):
        cols = slice(kt * KEY_TILE, (kt + 1) * KEY_TILE)
        wvt_ref[:, cols] = wvb[cols, :].T.astype(BF16)

    wait_fetch(b, 0)
    for p in range(n_pages):
        rows = slice(p * PAGE_SIZE, (p + 1) * PAGE_SIZE)
        ckb[rows, :] = stg_c[p, 0].T
        cvb[rows, :] = stg_c[p, 1].T

    @pl.when(b + 1 < nb)
    def _next_cmp():
        start_fetch(b + 1, 0)

    wait_fetch(b, 1)
    for p in range(n_pages):
        rows = slice(p * PAGE_SIZE, (p + 1) * PAGE_SIZE)
        skb[rows, :] = stg_s[p, 0].T.astype(BF16)
        svt_ref[:, rows] = stg_s[p, 1].astype(BF16)

    @pl.when(b + 1 < nb)
    def _next_slc():
        start_fetch(b + 1, 1)

    kc = _compress_pair(lambda s: ckb[pl.ds(s, nch, stride=CMP_STRIDE), :], w1_ref, pe_ref, w2_ref, 0, nch)
    vc = _compress_pair(lambda s: cvb[pl.ds(s, nch, stride=CMP_STRIDE), :], w1_ref, pe_ref, w2_ref, 1, nch)
    kc_ref[...] = kc.astype(BF16)
    vct_ref[...] = vc.T.astype(BF16)
    tmp[...] = jnp.zeros(tmp.shape, F32)
    tmp[0:n_new, :] = kvn_ref[0, :, 2 * LANES:3 * LANES]
    skb[past:past + PAGE_SIZE, :] = tmp[...].astype(BF16)
    tmp[0:n_new, :] = kvn_ref[0, :, 3 * LANES:4 * LANES]
    svt_ref[:, past:past + PAGE_SIZE] = tmp[...].T.astype(BF16)
    if s_pad > past + PAGE_SIZE:
        skb[past + PAGE_SIZE:, :] = jnp.zeros((s_pad - past - PAGE_SIZE, LANES), BF16)
        svt_ref[:, past + PAGE_SIZE:] = jnp.zeros((LANES, s_pad - past - PAGE_SIZE), BF16)

    qpos = past + lax.broadcasted_iota(jnp.int32, (1, nq), 1) % tq
    qtz = [qtz_ref[0, 0, g] for g in range(NSA_G)]
    o_cmp, scores = [], []
    for g in range(NSA_G):
        oc, sc = _cmp_branch(g, qtz[g], qpos, tq, nq, kc_ref, vct_ref, ov_ref, nch - 1)
        o_cmp.append(oc)
        scores.append(sc)
    _select_blocks(scores, n_slc, tq, nq, selb_ref)

    blocks_per_big = big // SLC_BLOCK
    carry = _attn_init(nq)
    for i in range(DEC_TILES):
        rows = slice(i * big, (i + 1) * big)
        k = skb[rows, :]
        biases = []
        for g in range(NSA_G):
            bg = _block_rows(selb_ref[g, i * blocks_per_big:(i + 1) * blocks_per_big, :], nq)
            if (i + 1) * big > past:
                bg = bg + _causal_bias(i * big + lax.broadcasted_iota(jnp.int32, (big, nq), 0), qpos)
            biases.append(bg)
        vts = [svt_ref[NSA_DK * g:NSA_DK * (g + 1), rows] for g in range(NSA_G)]
        carry = _attn_step(carry, k, qtz, vts, biases)
    o_slc = _attn_finish(carry)

    wb = _window_bias(past - nwin + lax.broadcasted_iota(jnp.int32, (w_pad, nq), 0), qpos)
    vts = [wvt_ref[NSA_DK * g:NSA_DK * (g + 1), :] for g in range(NSA_G)]
    o_win = _attn_finish(_attn_step(_attn_init(nq), wkb[...].astype(BF16), qtz, vts, [wb, wb]))

    for g in range(NSA_G):
        o_ref[0, 0, g] = _gated_sum(gate_ref[0, 0, g], o_cmp[g], o_slc[g], o_win[g])


def _nsa_decode(pool, page_table, kv_new, win_cache, win_new, qtz, gate_t, w1k, pek, w2k, *, tq):
    b, n_pages = page_table.shape
    n_new = kv_new.shape[1]
    nwin = win_cache.shape[1]
    past = n_pages * PAGE_SIZE
    nq = NSA_R * tq
    nch = (past + n_new) // CMP_STRIDE
    step_keys = DEC_TILES * KEY_TILE
    s_pad = -(-(past + n_new) // step_keys) * step_keys
    w_pad = -(-(nwin + n_new) // KEY_TILE) * KEY_TILE
    nj_pad = -(-(s_pad // SLC_BLOCK) // 16) * 16
    assert nwin % KEY_TILE == 0 and n_new <= LANES and (s_pad // DEC_TILES) % LANES == 0
    ov = _overlap_t(nj_pad, nch)
    kern = functools.partial(_nsa_decode_kernel, past=past, n_new=n_new, tq=tq)
    grid_spec = pltpu.PrefetchScalarGridSpec(
        num_scalar_prefetch=1,
        grid=(b,),
        in_specs=[
            pl.BlockSpec(memory_space=pl.ANY),
            pl.BlockSpec((1, n_new, COL_KV), lambda bi, pt: (bi, 0, 0)),
            pl.BlockSpec((1, nwin, LANES), lambda bi, pt: (bi, 0, 0)),
            pl.BlockSpec((1, nwin, LANES), lambda bi, pt: (bi, 0, 1)),
            pl.BlockSpec((1, n_new, COL_WIN), lambda bi, pt: (bi, 0, 0)),
            pl.BlockSpec((1, 1, NSA_G, LANES, nq), lambda bi, pt: (bi, 0, 0, 0, 0)),
            pl.BlockSpec((1, 1, NSA_G, SUBLANES, nq), lambda bi, pt: (bi, 0, 0, 0, 0)),
            pl.BlockSpec((nj_pad, nch), lambda bi, pt: (0, 0)),
            pl.BlockSpec(w1k.shape, lambda bi, pt: (0, 0, 0, 0)),
            pl.BlockSpec(pek.shape, lambda bi, pt: (0, 0, 0, 0)),
            pl.BlockSpec(w2k.shape, lambda bi, pt: (0, 0, 0)),
        ],
        out_specs=pl.BlockSpec((1, 1, NSA_G, NSA_DK, nq), lambda bi, pt: (bi, 0, 0, 0, 0)),
        scratch_shapes=[
            pltpu.VMEM((n_pages, 2, LANES, PAGE_SIZE), F32),
            pltpu.VMEM((n_pages, 2, LANES, PAGE_SIZE), F32),
            pltpu.SemaphoreType.DMA((2,)),
            pltpu.VMEM((past, LANES), F32),
            pltpu.VMEM((past, LANES), F32),
            pltpu.VMEM((s_pad, LANES), BF16),
            pltpu.VMEM((LANES, s_pad), BF16),
            pltpu.VMEM((PAGE_SIZE, LANES), F32),
            pltpu.VMEM((w_pad, LANES), F32),
            pltpu.VMEM((w_pad, LANES), F32),
            pltpu.VMEM((nch, LANES), BF16),
            pltpu.VMEM((LANES, nch), BF16),
            pltpu.VMEM((LANES, w_pad), BF16),
            pltpu.VMEM((NSA_G, nj_pad, nq), F32),
        ],
    )
    return pl.pallas_call(
        kern,
        grid_spec=grid_spec,
        out_shape=jax.ShapeDtypeStruct((b, 1, NSA_G, NSA_DK, nq), F32),
        compiler_params=pltpu.CompilerParams(dimension_semantics=("arbitrary",), vmem_limit_bytes=VMEM_LIMIT),
        name="nsa_decode",
    )(page_table.reshape(-1), pool, kv_new, win_cache, win_cache, win_new, qtz, gate_t, ov, w1k, pek, w2k)


FF_TILE = 256


def _rmsnorm(x, w):
    return (x * lax.rsqrt(jnp.mean(x * x, axis=-1, keepdims=True) + EPS)) * w


def _gelu_tanh(c):
    return c * (0.5 * (1.0 + jnp.tanh(0.7978845608028654 * (c + 0.044715 * (c * c * c)))))


def _ffn_kernel(x_ref, ohg_ref, onsa_ref, past_ref, wout_ref, n2_ref, wg_ref, wv_ref, cw_ref, cb_ref, wd_ref, nf_ref,
                y_ref, conv_ref, carry_ref, act_ref, *, tm):
    ti = pl.program_id(1)

    @pl.when(ti == 0)
    def _init():
        carry_ref[0:CONV_W - 1, :] = past_ref[0]

    x1 = (x_ref[0]
          + jnp.dot(ohg_ref[0].astype(BF16), wout_ref[0:HG_WIDTH, :], preferred_element_type=F32)
          + jnp.dot(onsa_ref[0].astype(BF16), wout_ref[HG_WIDTH:, :], preferred_element_type=F32))
    h = _rmsnorm(x1, n2_ref[...]).astype(BF16)
    row = lax.broadcasted_iota(jnp.int32, (tm, FF_TILE), 0)
    for j in range(D_FF // FF_TILE):
        cols = slice(j * FF_TILE, (j + 1) * FF_TILE)
        u = jnp.dot(h, wg_ref[:, cols], preferred_element_type=F32)
        val = jnp.dot(h, wv_ref[:, cols], preferred_element_type=F32)
        c0 = carry_ref[0:1, cols]
        c1 = carry_ref[1:2, cols]
        up1 = jnp.where(row == 0, c1, pltpu.roll(u, 1, axis=0))
        up2 = jnp.where(row == 0, c0, jnp.where(row == 1, c1, pltpu.roll(u, 2, axis=0)))
        c = cb_ref[:, cols] + up2 * cw_ref[0:1, cols] + up1 * cw_ref[1:2, cols] + u * cw_ref[2:3, cols]
        tail = u[tm - (CONV_W - 1):tm, :]
        carry_ref[0:CONV_W - 1, cols] = tail
        conv_ref[0, :, cols] = tail
        act_ref[:, cols] = (_gelu_tanh(c) * val).astype(BF16)
    y = jnp.dot(act_ref[...], wd_ref[...], preferred_element_type=F32)
    y_ref[0] = _rmsnorm(x1 + y, nf_ref[...])


def _ffn(x, o_hg, o_nsa, conv_past, wts, *, tm):
    b, t, _ = x.shape
    assert t % tm == 0 and D_FF % FF_TILE == 0
    wout, n2, wg, wv, cw, cb, wd, nf = wts
    const2 = lambda bi, ti: (0, 0)
    one = pl.Buffered(1)
    return pl.pallas_call(
        functools.partial(_ffn_kernel, tm=tm),
        grid=(b, t // tm),
        in_specs=[
            pl.BlockSpec((1, tm, D_MODEL), lambda bi, ti: (bi, ti, 0)),
            pl.BlockSpec((1, tm, HG_WIDTH), lambda bi, ti: (bi, ti, 0)),
            pl.BlockSpec((1, tm, NSA_WIDTH), lambda bi, ti: (bi, ti, 0)),
            pl.BlockSpec((1, CONV_W - 1, D_FF), lambda bi, ti: (bi, 0, 0)),
            pl.BlockSpec(wout.shape, const2, pipeline_mode=one),
            pl.BlockSpec(n2.shape, const2),
            pl.BlockSpec(wg.shape, const2, pipeline_mode=one),
            pl.BlockSpec(wv.shape, const2, pipeline_mode=one),
            pl.BlockSpec(cw.shape, const2),
            pl.BlockSpec(cb.shape, const2),
            pl.BlockSpec(wd.shape, const2, pipeline_mode=one),
            pl.BlockSpec(nf.shape, const2),
        ],
        out_specs=[
            pl.BlockSpec((1, tm, D_MODEL), lambda bi, ti: (bi, ti, 0)),
            pl.BlockSpec((1, CONV_W - 1, D_FF), lambda bi, ti: (bi, 0, 0)),
        ],
        out_shape=[jax.ShapeDtypeStruct((b, t, D_MODEL), F32), jax.ShapeDtypeStruct((b, CONV_W - 1, D_FF), F32)],
        scratch_shapes=[pltpu.VMEM((SUBLANES, D_FF), F32), pltpu.VMEM((tm, D_FF), BF16)],
        compiler_params=pltpu.CompilerParams(dimension_semantics=("arbitrary", "arbitrary"), vmem_limit_bytes=VMEM_LIMIT),
        name="ffn",
    )(x, o_hg, o_nsa, conv_past, wout, n2, wg, wv, cw, cb, wd, nf)


def _ffn_steps_kernel(x_ref, om_ref, past_ref, wout_ref, n2_ref, wg_ref, wv_ref, cw_ref, cb_ref, wd_ref, nf_ref,
                      y_ref, conv_ref, *, n_steps, nb):
    x1 = x_ref[...] + jnp.dot(om_ref[...].astype(BF16), wout_ref[...], preferred_element_type=F32)
    h = _rmsnorm(x1, n2_ref[...]).astype(BF16)
    acc = jnp.zeros((n_steps * nb, D_MODEL), F32)
    for j in range(D_FF // FF_TILE):
        cols = slice(j * FF_TILE, (j + 1) * FF_TILE)
        u = jnp.dot(h, wg_ref[:, cols], preferred_element_type=F32)
        val = jnp.dot(h, wv_ref[:, cols], preferred_element_type=F32)
        taps = [past_ref[k, :, cols] for k in range(CONV_W - 1)] + [u[t * nb:(t + 1) * nb, :] for t in range(n_steps)]
        cs = []
        for t in range(n_steps):
            c = cb_ref[:, cols]
            for k in range(CONV_W):
                c = c + taps[t + k] * cw_ref[k:k + 1, cols]
            cs.append(c)
        for k in range(CONV_W - 1):
            conv_ref[k, :, cols] = taps[n_steps + k]
        c_all = jnp.concatenate(cs, axis=0)
        acc = acc + jnp.dot((_gelu_tanh(c_all) * val).astype(BF16), wd_ref[cols, :], preferred_element_type=F32)
    y_ref[...] = _rmsnorm(x1 + acc, nf_ref[...])


def _ffn_steps(x_t, omix_t, past_t, wts, *, n_steps, nb):
    return pl.pallas_call(
        functools.partial(_ffn_steps_kernel, n_steps=n_steps, nb=nb),
        out_shape=[jax.ShapeDtypeStruct((n_steps * nb, D_MODEL), F32), jax.ShapeDtypeStruct((CONV_W - 1, nb, D_FF), F32)],
        compiler_params=pltpu.CompilerParams(vmem_limit_bytes=VMEM_LIMIT),
        name="ffn_steps",
    )(x_t, omix_t, past_t, *wts)


def kernel(x_prompt, x_sample, cache_nsa_kv, cache_win_kv, state_hgrn, state_ffn_conv, page_table, norm1, w_in, hg_lb_logits, hg_norm, cmp_pe, cmp_w1, cmp_w2, w_out, norm2, w_gate, w_val, conv_w, conv_b, w_down, norm_f):
    B, T, _ = x_prompt.shape
    Bd, Td, _ = x_sample.shape
    depth = norm1.shape[0]
    assert depth == 1
    l = 0
    w_in_bf = jnp.pad(w_in[l], ((0, 0), (0, _C_GATE - D_IN))).astype(BF16)
    w1k, pek, w2k = _prep_cmp_weights(cmp_pe[l], cmp_w1[l], cmp_w2[l])
    ffn_w = (w_out[l].astype(BF16), norm2[l].reshape(1, D_MODEL), w_gate[l].astype(BF16), w_val[l].astype(BF16),
             conv_w[l], conv_b[l].reshape(1, D_FF), w_down[l].astype(BF16), norm_f.reshape(1, D_MODEL))

    z_hg, q_p, kv_p, win_p, gate_p, kv_t = _in_proj(x_prompt.reshape(B * T, D_MODEL), norm1[l], w_in_bf, seq_for_kv_t=T)
    o_hg_p, s_p = _hgrn(z_hg.reshape(B, T, _C_HG), hg_lb_logits, hg_norm[l],
                        jnp.zeros((B, HG_HEADS, HG_DK, HG_DK), F32), chunk=HG_CHUNK, rows_per_step=512, t_valid=None,
                        heads_per_step=HG_HEADS)
    kv_p = kv_p.reshape(B, T, COL_KV)
    win_p = win_p.reshape(B, T, COL_WIN)
    o_nsa_p = _nsa_prompt(kv_p, win_p, q_p.reshape(B, T, NSA_WIDTH), gate_p.reshape(B, T, LANES), w1k, pek, w2k,
                          tq=KEY_TILE)
    y_p, conv_p = _ffn(x_prompt, o_hg_p, o_nsa_p, jnp.zeros((B, CONV_W - 1, D_FF), F32), ffn_w, tm=512)

    z_hg, q_s, kv_s, win_s, gate_s = _in_proj(x_sample.reshape(Bd * Td, D_MODEL), norm1[l], w_in_bf)
    t_pad = HG_SUB
    z_pad = jnp.pad(z_hg.reshape(Bd, Td, _C_HG), ((0, 0), (0, t_pad - Td), (0, 0)))
    o_hg_s, s_s = _hgrn(z_pad, hg_lb_logits, hg_norm[l], state_hgrn[l], chunk=t_pad, rows_per_step=t_pad, t_valid=Td,
                        heads_per_step=HG_HEADS)
    o_hg_s = o_hg_s[:, :Td]
    tqs = LANES // NSA_R
    kv_s = kv_s.reshape(Bd, Td, COL_KV)
    win_s = win_s.reshape(Bd, Td, COL_WIN)
    n_pool = cache_nsa_kv.shape[1]
    nwin = cache_win_kv.shape[2]
    pool_t = jnp.transpose(cache_nsa_kv[l], (0, 2, 3, 4, 1)).reshape(n_pool, 4, NSA_G * NSA_DK, PAGE_SIZE)
    o_t = _nsa_decode(pool_t, page_table, kv_s,
                      cache_win_kv[l].reshape(Bd, nwin, COL_WIN), win_s,
                      _q_to_lanes(q_s.reshape(Bd, Td, NSA_WIDTH), Td, tqs),
                      _gate_to_lanes(gate_s.reshape(Bd, Td, LANES), Td, tqs), w1k, pek, w2k, tq=tqs)
    o_nsa_s = _o_from_lanes(o_t, Td, tqs)
    omix_t = jnp.concatenate([o_hg_s, o_nsa_s], axis=-1).transpose(1, 0, 2).reshape(Td * Bd, D_MODEL)
    y_t, conv_t = _ffn_steps(x_sample.transpose(1, 0, 2).reshape(Td * Bd, D_MODEL), omix_t,
                             state_ffn_conv[l].transpose(1, 0, 2), ffn_w, n_steps=Td, nb=Bd)
    y_s = y_t.reshape(Td, Bd, D_MODEL).transpose(1, 0, 2)
    conv_s = conv_t.transpose(1, 0, 2)

    win_keep_p = min(WINDOW, T)
    win_all_s = jnp.concatenate([cache_win_kv[l].reshape(Bd, nwin, COL_WIN), win_s], axis=1)[:, -nwin:]
    kv_rows_p = kv_t.reshape(1, B, 4, NSA_G, NSA_DK, T).transpose(0, 1, 5, 2, 3, 4)
    return (y_p, y_s,
            kv_rows_p, kv_s.reshape(1, Bd, Td, 4, NSA_G, NSA_DK),
            win_p[:, T - win_keep_p:].reshape(1, B, win_keep_p, 2, NSA_G, NSA_DK),
            win_all_s.reshape(1, Bd, nwin, 2, NSA_G, NSA_DK),
            s_p[None], s_s[None], conv_p[None], conv_s[None])
```

```python
import functools

import numpy as np
import jax
import jax.numpy as jnp
from jax import lax
from jax.experimental import pallas as pl
from jax.experimental.pallas import tpu as pltpu

F32 = jnp.float32
BF16 = jnp.bfloat16

D_MODEL = 1024
PAGE_SIZE = 128
HG_WIDTH = 512
HG_HEADS = 4
HG_DK = 128
HG_CHUNK = 64
NSA_WIDTH = 512
NSA_HEADS = 8
NSA_DK = 64
NSA_G = 2
NSA_R = 4
CMP_STRIDE = 16
CMP_BLOCK = 32
CMP_HIDDEN = 256
SLC_BLOCK = 64
SLC_TOPK = 16
WINDOW = 512
D_FF = 2816
CONV_W = 3
EPS = 1e-6
COL_KV = 512
COL_WIN = 256
COL_GATE = 24

LANES = 128
SUBLANES = 8
HG_SUB = 16
HG_UNROLL = 8
HG_MASKED_EXP = -1e30
VMEM_LIMIT = 56 * 1024 * 1024

_C_HG = 4 * HG_WIDTH
_C_Q = _C_HG + NSA_WIDTH
_C_KV = _C_Q + COL_KV
_C_WIN = _C_KV + COL_WIN
_C_GATE = _C_WIN + LANES
D_IN = _C_WIN + COL_GATE


def _sigmoid(x):
    return 1.0 / (1.0 + jnp.exp(-x))


def _split3_bf16(x):
    hi = x.astype(BF16)
    r1 = x - hi.astype(F32)
    mid = r1.astype(BF16)
    lo = (r1 - mid.astype(F32)).astype(BF16)
    return hi, mid, lo


def _in_proj_kernel(x_ref, g_ref, w_ref, hg_ref, q_ref, kv_ref, win_ref, gate_ref, *kvt_ref):
    x = x_ref[...]
    ms = jnp.mean(x * x, axis=-1, keepdims=True)
    h = ((x * lax.rsqrt(ms + EPS)) * g_ref[...]).astype(BF16)
    hg_ref[...] = jnp.dot(h, w_ref[:, 0:_C_HG], preferred_element_type=F32)
    q_ref[...] = jnp.dot(h, w_ref[:, _C_HG:_C_Q], preferred_element_type=F32)
    kv = jnp.dot(h, w_ref[:, _C_Q:_C_KV], preferred_element_type=F32)
    kv_ref[...] = kv
    win_ref[...] = jnp.dot(h, w_ref[:, _C_KV:_C_WIN], preferred_element_type=F32)
    gate_ref[...] = jnp.dot(h, w_ref[:, _C_WIN:_C_GATE], preferred_element_type=F32)
    if kvt_ref:
        kvt_ref[0][0] = kv.T


def _in_proj(x2d, norm_w, w_in_bf, *, seq_for_kv_t=None):
    n = x2d.shape[0]
    tm = 512
    assert n % tm == 0
    widths = (_C_HG, NSA_WIDTH, COL_KV, COL_WIN, LANES)
    out_specs = [pl.BlockSpec((tm, w), lambda i: (i, 0)) for w in widths]
    out_shape = [jax.ShapeDtypeStruct((n, w), F32) for w in widths]
    if seq_for_kv_t is not None:
        tiles = seq_for_kv_t // tm
        assert seq_for_kv_t % tm == 0
        out_specs.append(pl.BlockSpec((1, COL_KV, tm), lambda i: (i // tiles, 0, i % tiles)))
        out_shape.append(jax.ShapeDtypeStruct((n // seq_for_kv_t, COL_KV, seq_for_kv_t), F32))
    return pl.pallas_call(
        _in_proj_kernel,
        grid=(n // tm,),
        in_specs=[
            pl.BlockSpec((tm, D_MODEL), lambda i: (i, 0)),
            pl.BlockSpec((1, D_MODEL), lambda i: (0, 0)),
            pl.BlockSpec((D_MODEL, _C_GATE), lambda i: (0, 0)),
        ],
        out_specs=out_specs,
        out_shape=out_shape,
        compiler_params=pltpu.CompilerParams(dimension_semantics=("arbitrary",), vmem_limit_bytes=VMEM_LIMIT),
        name="in_proj",
    )(x2d, norm_w.reshape(1, D_MODEL), w_in_bf)


def _hgrn_kernel(q_ref, f_ref, i_ref, g_ref, lb_ref, nw_ref, s0_ref, e_ref, o_ref, sfin_ref, st_ref,
                 *, chunk, rows_per_step, t_valid, layer):
    C = chunk
    nsb = C // HG_SUB
    ti = pl.program_id(2)

    hps = st_ref.shape[0]

    @pl.when(ti == 0)
    def _init():
        for hh in range(hps):
            st_ref[hh] = s0_ref[0, hh].T

    lg = lb_ref[...]
    le = jnp.exp(lg - jnp.max(lg, axis=0, keepdims=True))
    lb_all = jnp.sum(le[0:layer + 1, :], axis=0, keepdims=True) / jnp.sum(le, axis=0, keepdims=True)
    nw_all = nw_ref[...]
    row_i = lax.broadcasted_iota(jnp.int32, (C, C), 0)
    col_i = lax.broadcasted_iota(jnp.int32, (C, C), 1)
    tri_bf = jnp.where(col_i <= row_i, 1.0, 0.0).astype(BF16)
    diag_mask = ((row_i // HG_SUB) == (col_i // HG_SUB)) & (col_i <= row_i)
    off_mask = (col_i // HG_SUB) < (row_i // HG_SUB)
    trel = lax.broadcasted_iota(jnp.int32, (C, HG_DK), 0) % HG_SUB
    nt_dims = (((1,), (1,)), ((), ()))
    tn_dims = (((0,), (0,)), ((), ()))

    def do_chunk(hh, c):
        lanes = slice(hh * HG_DK, (hh + 1) * HG_DK)
        lb = lb_all[:, lanes]
        one_m_lb = 1.0 - lb
        nw = nw_all[:, lanes]
        r0 = pl.multiple_of(c * C, C)
        rows = pl.ds(r0, C)
        fr = f_ref[0, rows, lanes]
        qr = q_ref[0, rows, lanes]
        v = i_ref[0, rows, lanes]
        gr = g_ref[0, rows, lanes]
        f = lb + one_m_lb * _sigmoid(fr)
        kk = one_m_lb * _sigmoid(-fr)
        logf = jnp.log2(f)
        if t_valid is not None:
            tok = ti * rows_per_step + r0 + lax.broadcasted_iota(jnp.int32, (C, HG_DK), 0)
            ok = tok < t_valid
            logf = jnp.where(ok, logf, 0.0)
            kk = jnp.where(ok, kk, 0.0)
        q = qr * _sigmoid(qr)
        hi, mid, lo = _split3_bf16(logf)
        a = (jnp.dot(tri_bf, hi, preferred_element_type=F32)
             + jnp.dot(tri_bf, mid, preferred_element_type=F32)
             + jnp.dot(tri_bf, lo, preferred_element_type=F32))
        st = st_ref[hh]
        vb = v.astype(BF16)
        o = lax.dot_general((q * jnp.exp2(a)).astype(BF16), st.astype(BF16), nt_dims, preferred_element_type=F32)

        if nsb > 1:
            parts = [jnp.zeros((HG_SUB, C), F32)]
            for i in range(1, nsb):
                bi = a[HG_SUB * i - 1:HG_SUB * i, :]
                qi = q[HG_SUB * i:HG_SUB * (i + 1), :] * jnp.exp2(a[HG_SUB * i:HG_SUB * (i + 1), :] - bi)
                ki = kk * jnp.exp2(jnp.minimum(bi - a, 0.0))
                parts.append(lax.dot_general(qi.astype(BF16), ki.astype(BF16), nt_dims, preferred_element_type=F32))
            att = jnp.where(off_mask, jnp.concatenate(parts, axis=0), 0.0)
        else:
            att = jnp.zeros((C, C), F32)

        a3 = a.reshape(nsb, HG_SUB, HG_DK)
        k3 = kk.reshape(nsb, HG_SUB, HG_DK)
        zs = []
        for j in range(HG_SUB):
            aj = jnp.broadcast_to(a3[:, j:j + 1, :], (nsb, HG_SUB, HG_DK)).reshape(C, HG_DK)
            kj = jnp.broadcast_to(k3[:, j:j + 1, :], (nsb, HG_SUB, HG_DK)).reshape(C, HG_DK)
            expo = a - aj if j == 0 else jnp.where(trel >= j, a - aj, HG_MASKED_EXP)
            zs.append((q * kj * jnp.exp2(expo)).astype(BF16))
        zcat = jnp.concatenate(zs, axis=1)
        att_d = jnp.dot(zcat, e_ref[...], preferred_element_type=F32)
        att = att + jnp.where(diag_mask, att_d, 0.0)
        o = o + jnp.dot(att.astype(BF16), vb, preferred_element_type=F32)

        a_end = a[C - 1:C, :]
        kdec = kk * jnp.exp2(a_end - a)
        st_ref[hh] = st * jnp.exp2(a_end) + lax.dot_general(vb, kdec.astype(BF16), tn_dims, preferred_element_type=F32)

        on = (o * lax.rsqrt(jnp.mean(o * o, axis=-1, keepdims=True) + EPS)) * nw
        o_ref[0, rows, lanes] = on * (gr * _sigmoid(gr))

    n_chunks = rows_per_step // C
    if n_chunks == 1:
        for hh in range(hps):
            do_chunk(hh, 0)
    else:
        for hh in range(hps):
            def chunk_body(c, carry, hh=hh):
                do_chunk(hh, c)
                return carry
            lax.fori_loop(0, n_chunks, chunk_body, 0, unroll=min(n_chunks, HG_UNROLL))

    @pl.when(ti == pl.num_programs(2) - 1)
    def _fin():
        for hh in range(hps):
            sfin_ref[0, hh] = st_ref[hh].T


def _hgrn(z_hg, lb_logits, norm_w, s0, *, chunk, rows_per_step, t_valid, heads_per_step=1, layer=0):
    b, t, _ = z_hg.shape
    n_lb = lb_logits.shape[0]
    hps = heads_per_step
    assert t % rows_per_step == 0 and rows_per_step % chunk == 0 and chunk % HG_SUB == 0 and HG_HEADS % hps == 0
    nt = t // rows_per_step
    hblocks = HG_HEADS // hps
    width = hps * HG_DK
    sub = np.arange(chunk) % HG_SUB
    e_mat = jnp.asarray((np.repeat(np.arange(HG_SUB), HG_DK)[:, None] == sub[None, :]).astype(np.float32), BF16)

    def zspec(k):
        return pl.BlockSpec((1, rows_per_step, width), lambda bi, h, ti, k=k: (bi, ti, k * hblocks + h))

    kern = functools.partial(_hgrn_kernel, chunk=chunk, rows_per_step=rows_per_step, t_valid=t_valid, layer=layer)
    return pl.pallas_call(
        kern,
        grid=(b, hblocks, nt),
        in_specs=[
            zspec(0), zspec(1), zspec(2), zspec(3),
            pl.BlockSpec((n_lb, width), lambda bi, h, ti: (0, h)),
            pl.BlockSpec((1, width), lambda bi, h, ti: (0, h)),
            pl.BlockSpec((1, hps, HG_DK, HG_DK), lambda bi, h, ti: (bi, h, 0, 0)),
            pl.BlockSpec((HG_SUB * HG_DK, chunk), lambda bi, h, ti: (0, 0)),
        ],
        out_specs=[
            pl.BlockSpec((1, rows_per_step, width), lambda bi, h, ti: (bi, ti, h)),
            pl.BlockSpec((1, hps, HG_DK, HG_DK), lambda bi, h, ti: (bi, h, 0, 0)),
        ],
        out_shape=[jax.ShapeDtypeStruct((b, t, HG_WIDTH), F32), jax.ShapeDtypeStruct((b, HG_HEADS, HG_DK, HG_DK), F32)],
        scratch_shapes=[pltpu.VMEM((hps, HG_DK, HG_DK), F32)],
        compiler_params=pltpu.CompilerParams(dimension_semantics=("arbitrary", "arbitrary", "arbitrary"),
                                             vmem_limit_bytes=VMEM_LIMIT),
        name="hgrn2",
    )(z_hg, z_hg, z_hg, z_hg, lb_logits, norm_w.reshape(1, HG_WIDTH), s0, e_mat)


KEY_TILE = 256
CMP_PACK = 4
DEC_TILES = 3
LOG2E = 1.4426950408889634
Q_SCALE = LOG2E * NSA_DK ** -0.5
NEG_INF = float("-inf")


def _compress_pair(load_xs, w1_ref, pe_ref, w2_ref, slot, nch):
    r = jnp.zeros((SUBLANES, 2 * CMP_HIDDEN), F32)
    for u in range(CMP_STRIDE // CMP_PACK):
        r = r + jnp.dot(pe_ref[slot, u], w1_ref[slot, u], preferred_element_type=F32)
    bias = r[0:1, 0:CMP_HIDDEN] + r[1:2, CMP_HIDDEN:2 * CMP_HIDDEN]
    outs = []
    for g in range(NSA_G):
        acc = jnp.zeros((nch, 2 * CMP_HIDDEN), F32)
        for u in range(CMP_STRIDE // CMP_PACK):
            xg = jnp.concatenate([load_xs(CMP_PACK * u + i)[:, NSA_DK * g:NSA_DK * (g + 1)] for i in range(CMP_PACK)],
                                 axis=1).astype(BF16)
            acc = acc + jnp.dot(xg, w1_ref[slot, u], preferred_element_type=F32)
        h = acc[:, 0:CMP_HIDDEN] + pltpu.roll(acc[:, CMP_HIDDEN:2 * CMP_HIDDEN], nch - 1, axis=0) + bias
        hs = (h * _sigmoid(h)).astype(BF16)
        outs.append(jnp.dot(hs, w2_ref[slot], preferred_element_type=F32))
    return jnp.concatenate(outs, axis=1)


def _rank_bias(score, n_slc):
    nj, w = score.shape
    jrow = lax.broadcasted_iota(jnp.int32, (SUBLANES, w), 0)
    slabs = [score[SUBLANES * v:SUBLANES * (v + 1), :] for v in range(nj // SUBLANES)]
    ranks = [jnp.zeros((SUBLANES, w), F32) for _ in slabs]
    for jp in range(n_slc):
        row = score[jp:jp + 1, :]
        for v, slab in enumerate(slabs):
            lo = SUBLANES * v
            if lo > jp:
                beats = row >= slab
            elif lo + SUBLANES - 1 <= jp:
                beats = row > slab
            else:
                beats = (row > slab) | ((row == slab) & (jrow > jp - lo))
            ranks[v] = ranks[v] + jnp.where(beats, 1.0, 0.0)
    rank = jnp.concatenate(ranks, axis=0)
    return jnp.where((rank < float(SLC_TOPK)) & (score > NEG_INF), 0.0, NEG_INF)


def _cmp_branch(g, qtz, qpos, tq, nq, kc_ref, vct_ref, ov_ref, n_blk):
    nb_pad = kc_ref.shape[0]
    nj_pad = ov_ref.shape[0]
    s = jnp.dot(kc_ref[...], qtz, preferred_element_type=F32)
    n_idx = lax.broadcasted_iota(jnp.int32, (nb_pad, nq), 0)
    cmask = (n_idx * CMP_STRIDE + (CMP_BLOCK - 1) <= qpos) & (n_idx < n_blk)
    s = jnp.where(cmask, s, NEG_INF)
    m = jnp.max(s, axis=0, keepdims=True)
    m = jnp.where(m == NEG_INF, 0.0, m)
    e = jnp.exp2(s - m)
    d = jnp.sum(e, axis=0, keepdims=True)
    p = e / jnp.where(d > 0.0, d, 1.0)
    o_cmp = jnp.dot(vct_ref[NSA_DK * g:NSA_DK * (g + 1), :], p.astype(BF16), preferred_element_type=F32)

    if tq % LANES == 0:
        ps = p[:, 0:tq]
        for r in range(1, NSA_R):
            ps = ps + p[:, r * tq:(r + 1) * tq]
        qpos_w = qpos[:, 0:tq]
    else:
        assert nq == LANES and NSA_R * tq == LANES
        ps = p
        for r in range(1, NSA_R):
            ps = ps + pltpu.roll(p, r * tq, axis=1)
        qpos_w = qpos
    w = ps.shape[1]
    hi, mid, lo = _split3_bf16(ps)
    ov = ov_ref[...]
    imp = (jnp.dot(ov, hi, preferred_element_type=F32) + jnp.dot(ov, mid, preferred_element_type=F32)
           + jnp.dot(ov, lo, preferred_element_type=F32))
    j_idx = lax.broadcasted_iota(jnp.int32, (nj_pad, w), 0)
    cur = qpos_w // SLC_BLOCK
    forced = (j_idx == 0) | (j_idx == cur) | (j_idx == cur - 1)
    score = jnp.where(forced, jnp.inf, jnp.where(j_idx <= cur, imp, NEG_INF))
    return o_cmp, score


def _select_blocks(scores, n_slc, tq, nq, selb_ref):
    w = scores[0].shape[1]
    if w == nq:
        assert NSA_G == 2 and nq == LANES
        lane = lax.broadcasted_iota(jnp.int32, scores[0].shape, 1)
        low = lane < LANES // 2
        bias = _rank_bias(jnp.where(low, scores[0], scores[1]), n_slc)
        swapped = pltpu.roll(bias, LANES // 2, axis=1)
        selb_ref[0] = jnp.where(low, bias, swapped)
        selb_ref[1] = jnp.where(low, swapped, bias)
        return
    bias = _rank_bias(jnp.concatenate(scores, axis=1), n_slc)
    for g in range(NSA_G):
        selb_ref[g] = jnp.concatenate([bias[:, g * w:(g + 1) * w]] * (nq // w), axis=1)


def _block_rows(rows, nq):
    n = rows.shape[0]
    return jnp.broadcast_to(rows[:, None, :], (n, SLC_BLOCK, nq)).reshape(n * SLC_BLOCK, nq)


def _attn_step(carry, k, qtz, vts, biases):
    out = []
    for g in range(NSA_G):
        m, l, acc = carry[g]
        s = jnp.dot(k, qtz[g], preferred_element_type=F32) + biases[g]
        m_new = jnp.maximum(m, jnp.max(s, axis=0, keepdims=True))
        m_safe = jnp.where(m_new == NEG_INF, 0.0, m_new)
        alpha = jnp.exp2(m - m_safe)
        p = jnp.exp2(s - m_safe)
        l = l * alpha + jnp.sum(p, axis=0, keepdims=True)
        acc = acc * alpha + jnp.dot(vts[g], p.astype(BF16), preferred_element_type=F32)
        out.append((m_new, l, acc))
    return tuple(out)


def _attn_init(nq):
    return tuple((jnp.full((1, nq), NEG_INF, F32), jnp.zeros((1, nq), F32), jnp.zeros((NSA_DK, nq), F32))
                 for _ in range(NSA_G))


def _attn_finish(carry):
    return [acc / jnp.where(l > 0.0, l, 1.0) for (_, l, acc) in carry]


def _causal_bias(kpos, qpos):
    return jnp.where(kpos <= qpos, 0.0, NEG_INF)


def _window_bias(kpos, qpos):
    d = kpos - qpos
    return jnp.where((d <= 0) & (d > -WINDOW), 0.0, NEG_INF)


def _gated_sum(gate_raw, o_cmp, o_slc, o_win):
    gt = _sigmoid(gate_raw)
    return gt[0:1, :] * o_cmp + gt[1:2, :] * o_slc + gt[2:3, :] * o_win


def _nsa_prompt_kernel(ck_ref, cv_ref, sk_ref, sv_ref, wk_ref, wv_ref, q_ref, gate_ref, ov_ref, w1_ref, pe_ref, w2_ref,
                       o_ref, kc_ref, vct_ref, svt_ref, wvt_ref, selb_ref, *, seq, tq):
    nq = NSA_R * tq
    nch = seq // CMP_STRIDE
    n_tiles = seq // KEY_TILE
    n_slc = seq // SLC_BLOCK
    blocks_per_tile = KEY_TILE // SLC_BLOCK
    assert 2 * blocks_per_tile == SUBLANES
    qt = pl.program_id(1)

    @pl.when(qt == 0)
    def _prep():
        kc = _compress_pair(lambda s: ck_ref[0, pl.ds(s, nch, stride=CMP_STRIDE), :], w1_ref, pe_ref, w2_ref, 0, nch)
        vc = _compress_pair(lambda s: cv_ref[0, pl.ds(s, nch, stride=CMP_STRIDE), :], w1_ref, pe_ref, w2_ref, 1, nch)
        kc_ref[...] = kc.astype(BF16)
        vct_ref[...] = vc.T.astype(BF16)
        for kt in range(n_tiles):
            rows = slice(kt * KEY_TILE, (kt + 1) * KEY_TILE)
            svt_ref[kt] = sv_ref[0, rows, :].T.astype(BF16)
            wvt_ref[kt] = wv_ref[0, rows, :].T.astype(BF16)

    t0 = qt * tq
    qpos = t0 + lax.broadcasted_iota(jnp.int32, (1, nq), 1) % tq
    hi_tile = (t0 + tq - 1) // KEY_TILE + 1
    win_lo = jnp.maximum(t0 - (WINDOW - 1), 0) // KEY_TILE
    krow = lax.broadcasted_iota(jnp.int32, (KEY_TILE, nq), 0)

    gate_t = gate_ref[0].T
    zeros_half = jnp.zeros((NSA_DK, nq), F32)
    qtz, gates = [], []
    for g in range(NSA_G):
        qg_t = q_ref[0, :, g * NSA_R * NSA_DK:(g + 1) * NSA_R * NSA_DK].T
        q_t = jnp.concatenate([qg_t[r * NSA_DK:(r + 1) * NSA_DK, :] for r in range(NSA_R)], axis=1) * Q_SCALE
        qtz.append(jnp.concatenate([q_t, zeros_half] if g == 0 else [zeros_half, q_t], axis=0).astype(BF16))
        gates.append(jnp.concatenate(
            [jnp.concatenate([gate_t[k * NSA_HEADS + g * NSA_R + r:k * NSA_HEADS + g * NSA_R + r + 1, :]
                              for r in range(NSA_R)], axis=1) for k in range(3)], axis=0))

    o_cmp, scores = [], []
    for g in range(NSA_G):
        oc, sc = _cmp_branch(g, qtz[g], qpos, tq, nq, kc_ref, vct_ref, ov_ref, nch - 1)
        o_cmp.append(oc)
        scores.append(sc)
    _select_blocks(scores, n_slc, tq, nq, selb_ref)

    def slc_step(kt, carry, causal):
        rows = pl.ds(pl.multiple_of(kt * KEY_TILE, KEY_TILE), KEY_TILE)
        k = sk_ref[0, rows, :].astype(BF16)
        biases = []
        for g in range(NSA_G):
            rows8 = selb_ref[g, pl.ds(pl.multiple_of((kt // 2) * SUBLANES, SUBLANES), SUBLANES), :]
            rows4 = jnp.where(kt % 2 == 0, rows8[0:blocks_per_tile, :], rows8[blocks_per_tile:, :])
            bg = _block_rows(rows4, nq)
            biases.append(bg + _causal_bias(kt * KEY_TILE + krow, qpos) if causal else bg)
        vts = [svt_ref[kt, NSA_DK * g:NSA_DK * (g + 1), :] for g in range(NSA_G)]
        return _attn_step(carry, k, qtz, vts, biases)

    def win_step(kt, carry):
        rows = pl.ds(pl.multiple_of(kt * KEY_TILE, KEY_TILE), KEY_TILE)
        k = wk_ref[0, rows, :].astype(BF16)
        wb = _window_bias(kt * KEY_TILE + krow, qpos)
        vts = [wvt_ref[kt, NSA_DK * g:NSA_DK * (g + 1), :] for g in range(NSA_G)]
        return _attn_step(carry, k, qtz, vts, [wb, wb])

    carry = lax.fori_loop(0, hi_tile - 1, lambda kt, c: slc_step(kt, c, False), _attn_init(nq))
    o_slc = _attn_finish(slc_step(hi_tile - 1, carry, True))
    o_win = _attn_finish(lax.fori_loop(win_lo, hi_tile, win_step, _attn_init(nq)))

    for g in range(NSA_G):
        o_t = _gated_sum(gates[g], o_cmp[g], o_slc[g], o_win[g])
        stack = jnp.concatenate([o_t[:, r * tq:(r + 1) * tq] for r in range(NSA_R)], axis=0)
        o_ref[0, :, g * NSA_R * NSA_DK:(g + 1) * NSA_R * NSA_DK] = stack.T


def _overlap_t(nj_pad, nb_pad):
    n = np.arange(nb_pad)[None, :]
    j = np.arange(nj_pad)[:, None]
    lo = np.maximum(n * CMP_STRIDE, j * SLC_BLOCK)
    hi = np.minimum(n * CMP_STRIDE + CMP_BLOCK, (j + 1) * SLC_BLOCK)
    return jnp.asarray(np.maximum(hi - lo, 0).astype(np.float32) / CMP_STRIDE, BF16)


def _nsa_prompt(kv_new, win_new, q, gate, w1k, pek, w2k, *, tq):
    b, seq, _ = kv_new.shape
    assert tq % LANES == 0 and tq <= KEY_TILE and seq % KEY_TILE == 0
    nq = NSA_R * tq
    nch = seq // CMP_STRIDE
    n_slc = seq // SLC_BLOCK
    n_tiles = seq // KEY_TILE
    ov = _overlap_t(n_slc, nch)
    kern = functools.partial(_nsa_prompt_kernel, seq=seq, tq=tq)
    return pl.pallas_call(
        kern,
        grid=(b, seq // tq),
        in_specs=[
            pl.BlockSpec((1, seq, LANES), lambda bi, qi: (bi, 0, 0)),
            pl.BlockSpec((1, seq, LANES), lambda bi, qi: (bi, 0, 1)),
            pl.BlockSpec((1, seq, LANES), lambda bi, qi: (bi, 0, 2)),
            pl.BlockSpec((1, seq, LANES), lambda bi, qi: (bi, 0, 3)),
            pl.BlockSpec((1, seq, LANES), lambda bi, qi: (bi, 0, 0)),
            pl.BlockSpec((1, seq, LANES), lambda bi, qi: (bi, 0, 1)),
            pl.BlockSpec((1, tq, NSA_WIDTH), lambda bi, qi: (bi, qi, 0)),
            pl.BlockSpec((1, tq, LANES), lambda bi, qi: (bi, qi, 0)),
            pl.BlockSpec((n_slc, nch), lambda bi, qi: (0, 0)),
            pl.BlockSpec(w1k.shape, lambda bi, qi: (0, 0, 0, 0)),
            pl.BlockSpec(pek.shape, lambda bi, qi: (0, 0, 0, 0)),
            pl.BlockSpec(w2k.shape, lambda bi, qi: (0, 0, 0)),
        ],
        out_specs=pl.BlockSpec((1, tq, NSA_WIDTH), lambda bi, qi: (bi, qi, 0)),
        out_shape=jax.ShapeDtypeStruct((b, seq, NSA_WIDTH), F32),
        scratch_shapes=[
            pltpu.VMEM((nch, LANES), BF16),
            pltpu.VMEM((LANES, nch), BF16),
            pltpu.VMEM((n_tiles, LANES, KEY_TILE), BF16),
            pltpu.VMEM((n_tiles, LANES, KEY_TILE), BF16),
            pltpu.VMEM((NSA_G, n_slc, nq), F32),
        ],
        compiler_params=pltpu.CompilerParams(dimension_semantics=("arbitrary", "arbitrary"), vmem_limit_bytes=VMEM_LIMIT),
        name="nsa_prompt",
    )(kv_new, kv_new, kv_new, kv_new, win_new, win_new, q, gate, ov, w1k, pek, w2k)


def _prep_cmp_weights(cmp_pe, cmp_w1, cmp_w2):
    groups = CMP_STRIDE // CMP_PACK
    w1k = jnp.concatenate([cmp_w1[:, 0], cmp_w1[:, 1]], axis=-1)
    w1k = w1k.reshape(2, groups, CMP_PACK * NSA_DK, 2 * CMP_HIDDEN).astype(BF16)
    pek = cmp_pe.reshape(2, 2, groups, CMP_PACK * NSA_DK).transpose(0, 2, 1, 3)
    pek = jnp.pad(pek, ((0, 0), (0, 0), (0, SUBLANES - 2), (0, 0))).astype(BF16)
    return w1k, pek, cmp_w2.astype(BF16)


def _q_to_lanes(q, tq_real, tq):
    b, t, _ = q.shape
    nqt = t // tq_real
    x = (q * Q_SCALE).reshape(b, nqt, tq_real, NSA_G, NSA_R, NSA_DK)
    x = x.transpose(0, 1, 3, 5, 4, 2)
    x = jnp.pad(x, ((0, 0),) * 5 + ((0, tq - tq_real),)).reshape(b, nqt, NSA_G, NSA_DK, NSA_R * tq)
    z = jnp.zeros_like(x)
    return jnp.stack([jnp.concatenate([x[:, :, 0], z[:, :, 0]], axis=2),
                      jnp.concatenate([z[:, :, 1], x[:, :, 1]], axis=2)], axis=2).astype(BF16)


def _gate_to_lanes(gate, tq_real, tq):
    b, t, _ = gate.shape
    nqt = t // tq_real
    x = gate[..., :COL_GATE].reshape(b, nqt, tq_real, 3, NSA_G, NSA_R).transpose(0, 1, 4, 3, 5, 2)
    x = jnp.pad(x, ((0, 0),) * 5 + ((0, tq - tq_real),)).reshape(b, nqt, NSA_G, 3, NSA_R * tq)
    return jnp.pad(x, ((0, 0), (0, 0), (0, 0), (0, SUBLANES - 3), (0, 0)))


def _o_from_lanes(o_t, tq_real, tq):
    b, nqt = o_t.shape[:2]
    x = o_t.reshape(b, nqt, NSA_G, NSA_DK, NSA_R, tq)[..., :tq_real]
    return x.transpose(0, 1, 5, 2, 4, 3).reshape(b, nqt * tq_real, NSA_WIDTH)


def _nsa_decode_kernel(pt_ref, pool_ref, kvn_ref, wck_ref, wcv_ref, wn_ref, qtz_ref, gate_ref, ov_ref, w1_ref, pe_ref, w2_ref,
                       o_ref, stg, sem, ckb, cvb, skb, svt_ref, tmp, wkb, wvb, kc_ref, vct_ref, wvt_ref, selb_ref,
                       *, past, n_new, tq):
    nq = NSA_R * tq
    n_pages = past // PAGE_SIZE
    nch = (past + n_new) // CMP_STRIDE
    assert nch * CMP_STRIDE <= past
    n_slc = -(-(past + n_new) // SLC_BLOCK)
    s_pad = skb.shape[0]
    w_pad = wkb.shape[0]
    big = s_pad // DEC_TILES
    b = pl.program_id(0)
    nb = pl.num_programs(0)

    def page_copy(bb, p):
        return pltpu.make_async_copy(pool_ref.at[pt_ref[bb * n_pages + p]], stg.at[p], sem.at[0])

    def start_fetch(bb):
        def body(p, carry):
            page_copy(bb, p).start()
            return carry
        lax.fori_loop(0, n_pages, body, 0)

    def wait_fetch():
        pltpu.make_async_copy(pool_ref.at[pl.ds(0, n_pages)], stg, sem.at[0]).wait()

    @pl.when(b == 0)
    def _first():
        start_fetch(0)

    nwin = wck_ref.shape[1]
    for buf, cache_ref, lane0 in ((wkb, wck_ref, 0), (wvb, wcv_ref, LANES)):
        buf[0:nwin, :] = cache_ref[0]
        buf[nwin:, :] = jnp.zeros((w_pad - nwin, LANES), F32)
        buf[nwin:nwin + n_new, :] = wn_ref[0, :, lane0:lane0 + LANES]
    for kt in range(w_pad // KEY_TILE):
        cols = slice(kt * KEY_TILE, (kt + 1) * KEY_TILE)
        wvt_ref[:, cols] = wvb[cols, :].T.astype(BF16)

    wait_fetch()
    for p in range(n_pages):
        rows = slice(p * PAGE_SIZE, (p + 1) * PAGE_SIZE)
        ckb[rows, :] = stg[p, 0].T
        cvb[rows, :] = stg[p, 1].T
        skb[rows, :] = stg[p, 2].T.astype(BF16)
        svt_ref[:, rows] = stg[p, 3].astype(BF16)

    @pl.when(b + 1 < nb)
    def _next():
        start_fetch(b + 1)

    kc = _compress_pair(lambda s: ckb[pl.ds(s, nch, stride=CMP_STRIDE), :], w1_ref, pe_ref, w2_ref, 0, nch)
    vc = _compress_pair(lambda s: cvb[pl.ds(s, nch, stride=CMP_STRIDE), :], w1_ref, pe_ref, w2_ref, 1, nch)
    kc_ref[...] = kc.astype(BF16)
    vct_ref[...] = vc.T.astype(BF16)
    tmp[...] = jnp.zeros(tmp.shape, F32)
    tmp[0:n_new, :] = kvn_ref[0, :, 2 * LANES:3 * LANES]
    skb[past:past + PAGE_SIZE, :] = tmp[...].astype(BF16)
    tmp[0:n_new, :] = kvn_ref[0, :, 3 * LANES:4 * LANES]
    svt_ref[:, past:past + PAGE_SIZE] = tmp[...].T.astype(BF16)
    if s_pad > past + PAGE_SIZE:
        skb[past + PAGE_SIZE:, :] = jnp.zeros((s_pad - past - PAGE_SIZE, LANES), BF16)
        svt_ref[:, past + PAGE_SIZE:] = jnp.zeros((LANES, s_pad - past - PAGE_SIZE), BF16)

    qpos = past + lax.broadcasted_iota(jnp.int32, (1, nq), 1) % tq
    qtz = [qtz_ref[0, 0, g] for g in range(NSA_G)]
    o_cmp, scores = [], []
    for g in range(NSA_G):
        oc, sc = _cmp_branch(g, qtz[g], qpos, tq, nq, kc_ref, vct_ref, ov_ref, nch - 1)
        o_cmp.append(oc)
        scores.append(sc)
    _select_blocks(scores, n_slc, tq, nq, selb_ref)

    blocks_per_big = big // SLC_BLOCK
    carry = _attn_init(nq)
    for i in range(DEC_TILES):
        rows = slice(i * big, (i + 1) * big)
        k = skb[rows, :]
        biases = []
        for g in range(NSA_G):
            bg = _block_rows(selb_ref[g, i * blocks_per_big:(i + 1) * blocks_per_big, :], nq)
            if (i + 1) * big > past:
                bg = bg + _causal_bias(i * big + lax.broadcasted_iota(jnp.int32, (big, nq), 0), qpos)
            biases.append(bg)
        vts = [svt_ref[NSA_DK * g:NSA_DK * (g + 1), rows] for g in range(NSA_G)]
        carry = _attn_step(carry, k, qtz, vts, biases)
    o_slc = _attn_finish(carry)

    wb = _window_bias(past - nwin + lax.broadcasted_iota(jnp.int32, (w_pad, nq), 0), qpos)
    vts = [wvt_ref[NSA_DK * g:NSA_DK * (g + 1), :] for g in range(NSA_G)]
    o_win = _attn_finish(_attn_step(_attn_init(nq), wkb[...].astype(BF16), qtz, vts, [wb, wb]))

    for g in range(NSA_G):
        o_ref[0, 0, g] = _gated_sum(gate_ref[0, 0, g], o_cmp[g], o_slc[g], o_win[g])


def _nsa_decode(pool, page_table, kv_new, win_cache, win_new, qtz, gate_t, w1k, pek, w2k, *, tq):
    b, n_pages = page_table.shape
    n_new = kv_new.shape[1]
    nwin = win_cache.shape[1]
    past = n_pages * PAGE_SIZE
    nq = NSA_R * tq
    nch = (past + n_new) // CMP_STRIDE
    step_keys = DEC_TILES * KEY_TILE
    s_pad = -(-(past + n_new) // step_keys) * step_keys
    w_pad = -(-(nwin + n_new) // KEY_TILE) * KEY_TILE
    nj_pad = -(-(s_pad // SLC_BLOCK) // 16) * 16
    assert nwin % KEY_TILE == 0 and n_new <= LANES and (s_pad // DEC_TILES) % LANES == 0
    ov = _overlap_t(nj_pad, nch)
    kern = functools.partial(_nsa_decode_kernel, past=past, n_new=n_new, tq=tq)
    grid_spec = pltpu.PrefetchScalarGridSpec(
        num_scalar_prefetch=1,
        grid=(b,),
        in_specs=[
            pl.BlockSpec(memory_space=pl.ANY),
            pl.BlockSpec((1, n_new, COL_KV), lambda bi, pt: (bi, 0, 0)),
            pl.BlockSpec((1, nwin, LANES), lambda bi, pt: (bi, 0, 0)),
            pl.BlockSpec((1, nwin, LANES), lambda bi, pt: (bi, 0, 1)),
            pl.BlockSpec((1, n_new, COL_WIN), lambda bi, pt: (bi, 0, 0)),
            pl.BlockSpec((1, 1, NSA_G, LANES, nq), lambda bi, pt: (bi, 0, 0, 0, 0)),
            pl.BlockSpec((1, 1, NSA_G, SUBLANES, nq), lambda bi, pt: (bi, 0, 0, 0, 0)),
            pl.BlockSpec((nj_pad, nch), lambda bi, pt: (0, 0)),
            pl.BlockSpec(w1k.shape, lambda bi, pt: (0, 0, 0, 0)),
            pl.BlockSpec(pek.shape, lambda bi, pt: (0, 0, 0, 0)),
            pl.BlockSpec(w2k.shape, lambda bi, pt: (0, 0, 0)),
        ],
        out_specs=pl.BlockSpec((1, 1, NSA_G, NSA_DK, nq), lambda bi, pt: (bi, 0, 0, 0, 0)),
        scratch_shapes=[
            pltpu.VMEM((n_pages, 4, LANES, PAGE_SIZE), F32),
            pltpu.SemaphoreType.DMA((1,)),
            pltpu.VMEM((past, LANES), F32),
            pltpu.VMEM((past, LANES), F32),
            pltpu.VMEM((s_pad, LANES), BF16),
            pltpu.VMEM((LANES, s_pad), BF16),
            pltpu.VMEM((PAGE_SIZE, LANES), F32),
            pltpu.VMEM((w_pad, LANES), F32),
            pltpu.VMEM((w_pad, LANES), F32),
            pltpu.VMEM((nch, LANES), BF16),
            pltpu.VMEM((LANES, nch), BF16),
            pltpu.VMEM((LANES, w_pad), BF16),
            pltpu.VMEM((NSA_G, nj_pad, nq), F32),
        ],
    )
    return pl.pallas_call(
        kern,
        grid_spec=grid_spec,
        out_shape=jax.ShapeDtypeStruct((b, 1, NSA_G, NSA_DK, nq), F32),
        compiler_params=pltpu.CompilerParams(dimension_semantics=("arbitrary",), vmem_limit_bytes=VMEM_LIMIT),
        name="nsa_decode",
    )(page_table.reshape(-1), pool, kv_new, win_cache, win_cache, win_new, qtz, gate_t, ov, w1k, pek, w2k)


FF_TILE = 256


def _rmsnorm(x, w):
    return (x * lax.rsqrt(jnp.mean(x * x, axis=-1, keepdims=True) + EPS)) * w


def _gelu_tanh(c):
    return c * (0.5 * (1.0 + jnp.tanh(0.7978845608028654 * (c + 0.044715 * (c * c * c)))))


def _ffn_kernel(x_ref, ohg_ref, onsa_ref, past_ref, wout_ref, n2_ref, wg_ref, wv_ref, cw_ref, cb_ref, wd_ref, nf_ref,
                y_ref, conv_ref, carry_ref, act_ref, *, tm):
    ti = pl.program_id(1)

    @pl.when(ti == 0)
    def _init():
        carry_ref[0:CONV_W - 1, :] = past_ref[0]

    x1 = (x_ref[0]
          + jnp.dot(ohg_ref[0].astype(BF16), wout_ref[0:HG_WIDTH, :], preferred_element_type=F32)
          + jnp.dot(onsa_ref[0].astype(BF16), wout_ref[HG_WIDTH:, :], preferred_element_type=F32))
    h = _rmsnorm(x1, n2_ref[...]).astype(BF16)
    row = lax.broadcasted_iota(jnp.int32, (tm, FF_TILE), 0)
    for j in range(D_FF // FF_TILE):
        cols = slice(j * FF_TILE, (j + 1) * FF_TILE)
        u = jnp.dot(h, wg_ref[:, cols], preferred_element_type=F32)
        val = jnp.dot(h, wv_ref[:, cols], preferred_element_type=F32)
        c0 = carry_ref[0:1, cols]
        c1 = carry_ref[1:2, cols]
        up1 = jnp.where(row == 0, c1, pltpu.roll(u, 1, axis=0))
        up2 = jnp.where(row == 0, c0, jnp.where(row == 1, c1, pltpu.roll(u, 2, axis=0)))
        c = cb_ref[:, cols] + up2 * cw_ref[0:1, cols] + up1 * cw_ref[1:2, cols] + u * cw_ref[2:3, cols]
        tail = u[tm - (CONV_W - 1):tm, :]
        carry_ref[0:CONV_W - 1, cols] = tail
        conv_ref[0, :, cols] = tail
        act_ref[:, cols] = (_gelu_tanh(c) * val).astype(BF16)
    y = jnp.dot(act_ref[...], wd_ref[...], preferred_element_type=F32)
    y_ref[0] = _rmsnorm(x1 + y, nf_ref[...])


def _ffn(x, o_hg, o_nsa, conv_past, wts, *, tm):
    b, t, _ = x.shape
    assert t % tm == 0 and D_FF % FF_TILE == 0
    wout, n2, wg, wv, cw, cb, wd, nf = wts
    const2 = lambda bi, ti: (0, 0)
    one = pl.Buffered(1)
    return pl.pallas_call(
        functools.partial(_ffn_kernel, tm=tm),
        grid=(b, t // tm),
        in_specs=[
            pl.BlockSpec((1, tm, D_MODEL), lambda bi, ti: (bi, ti, 0)),
            pl.BlockSpec((1, tm, HG_WIDTH), lambda bi, ti: (bi, ti, 0)),
            pl.BlockSpec((1, tm, NSA_WIDTH), lambda bi, ti: (bi, ti, 0)),
            pl.BlockSpec((1, CONV_W - 1, D_FF), lambda bi, ti: (bi, 0, 0)),
            pl.BlockSpec(wout.shape, const2, pipeline_mode=one),
            pl.BlockSpec(n2.shape, const2),
            pl.BlockSpec(wg.shape, const2, pipeline_mode=one),
            pl.BlockSpec(wv.shape, const2, pipeline_mode=one),
            pl.BlockSpec(cw.shape, const2),
            pl.BlockSpec(cb.shape, const2),
            pl.BlockSpec(wd.shape, const2, pipeline_mode=one),
            pl.BlockSpec(nf.shape, const2),
        ],
        out_specs=[
            pl.BlockSpec((1, tm, D_MODEL), lambda bi, ti: (bi, ti, 0)),
            pl.BlockSpec((1, CONV_W - 1, D_FF), lambda bi, ti: (bi, 0, 0)),
        ],
        out_shape=[jax.ShapeDtypeStruct((b, t, D_MODEL), F32), jax.ShapeDtypeStruct((b, CONV_W - 1, D_FF), F32)],
        scratch_shapes=[pltpu.VMEM((SUBLANES, D_FF), F32), pltpu.VMEM((tm, D_FF), BF16)],
        compiler_params=pltpu.CompilerParams(dimension_semantics=("arbitrary", "arbitrary"), vmem_limit_bytes=VMEM_LIMIT),
        name="ffn",
    )(x, o_hg, o_nsa, conv_past, wout, n2, wg, wv, cw, cb, wd, nf)


def _ffn_steps_kernel(x_ref, om_ref, past_ref, wout_ref, n2_ref, wg_ref, wv_ref, cw_ref, cb_ref, wd_ref, nf_ref,
                      y_ref, conv_ref, *, n_steps, nb):
    x1 = x_ref[...] + jnp.dot(om_ref[...].astype(BF16), wout_ref[...], preferred_element_type=F32)
    h = _rmsnorm(x1, n2_ref[...]).astype(BF16)
    acc = jnp.zeros((n_steps * nb, D_MODEL), F32)
    for j in range(D_FF // FF_TILE):
        cols = slice(j * FF_TILE, (j + 1) * FF_TILE)
        u = jnp.dot(h, wg_ref[:, cols], preferred_element_type=F32)
        val = jnp.dot(h, wv_ref[:, cols], preferred_element_type=F32)
        taps = [past_ref[k, :, cols] for k in range(CONV_W - 1)] + [u[t * nb:(t + 1) * nb, :] for t in range(n_steps)]
        cs = []
        for t in range(n_steps):
            c = cb_ref[:, cols]
            for k in range(CONV_W):
                c = c + taps[t + k] * cw_ref[k:k + 1, cols]
            cs.append(c)
        for k in range(CONV_W - 1):
            conv_ref[k, :, cols] = taps[n_steps + k]
        c_all = jnp.concatenate(cs, axis=0)
        acc = acc + jnp.dot((_gelu_tanh(c_all) * val).astype(BF16), wd_ref[cols, :], preferred_element_type=F32)
    y_ref[...] = _rmsnorm(x1 + acc, nf_ref[...])


def _ffn_steps(x_t, omix_t, past_t, wts, *, n_steps, nb):
    return pl.pallas_call(
        functools.partial(_ffn_steps_kernel, n_steps=n_steps, nb=nb),
        out_shape=[jax.ShapeDtypeStruct((n_steps * nb, D_MODEL), F32), jax.ShapeDtypeStruct((CONV_W - 1, nb, D_FF), F32)],
        compiler_params=pltpu.CompilerParams(vmem_limit_bytes=VMEM_LIMIT),
        name="ffn_steps",
    )(x_t, omix_t, past_t, *wts)


def kernel(x_prompt, x_sample, cache_nsa_kv, cache_win_kv, state_hgrn, state_ffn_conv, page_table, norm1, w_in, hg_lb_logits, hg_norm, cmp_pe, cmp_w1, cmp_w2, w_out, norm2, w_gate, w_val, conv_w, conv_b, w_down, norm_f):
    B, T, _ = x_prompt.shape
    Bd, Td, _ = x_sample.shape
    depth = norm1.shape[0]
    assert depth == 1
    l = 0
    w_in_bf = jnp.pad(w_in[l], ((0, 0), (0, _C_GATE - D_IN))).astype(BF16)
    w1k, pek, w2k = _prep_cmp_weights(cmp_pe[l], cmp_w1[l], cmp_w2[l])
    ffn_w = (w_out[l].astype(BF16), norm2[l].reshape(1, D_MODEL), w_gate[l].astype(BF16), w_val[l].astype(BF16),
             conv_w[l], conv_b[l].reshape(1, D_FF), w_down[l].astype(BF16), norm_f.reshape(1, D_MODEL))

    z_hg, q_p, kv_p, win_p, gate_p, kv_t = _in_proj(x_prompt.reshape(B * T, D_MODEL), norm1[l], w_in_bf, seq_for_kv_t=T)
    o_hg_p, s_p = _hgrn(z_hg.reshape(B, T, _C_HG), hg_lb_logits, hg_norm[l],
                        jnp.zeros((B, HG_HEADS, HG_DK, HG_DK), F32), chunk=HG_CHUNK, rows_per_step=512, t_valid=None,
                        heads_per_step=HG_HEADS)
    kv_p = kv_p.reshape(B, T, COL_KV)
    win_p = win_p.reshape(B, T, COL_WIN)
    o_nsa_p = _nsa_prompt(kv_p, win_p, q_p.reshape(B, T, NSA_WIDTH), gate_p.reshape(B, T, LANES), w1k, pek, w2k,
                          tq=KEY_TILE)
    y_p, conv_p = _ffn(x_prompt, o_hg_p, o_nsa_p, jnp.zeros((B, CONV_W - 1, D_FF), F32), ffn_w, tm=512)

    z_hg, q_s, kv_s, win_s, gate_s = _in_proj(x_sample.reshape(Bd * Td, D_MODEL), norm1[l], w_in_bf)
    t_pad = HG_SUB
    z_pad = jnp.pad(z_hg.reshape(Bd, Td, _C_HG), ((0, 0), (0, t_pad - Td), (0, 0)))
    o_hg_s, s_s = _hgrn(z_pad, hg_lb_logits, hg_norm[l], state_hgrn[l], chunk=t_pad, rows_per_step=t_pad, t_valid=Td,
                        heads_per_step=HG_HEADS)
    o_hg_s = o_hg_s[:, :Td]
    tqs = LANES // NSA_R
    kv_s = kv_s.reshape(Bd, Td, COL_KV)
    win_s = win_s.reshape(Bd, Td, COL_WIN)
    n_pool = cache_nsa_kv.shape[1]
    nwin = cache_win_kv.shape[2]
    pool_t = jnp.transpose(cache_nsa_kv[l], (0, 2, 3, 4, 1)).reshape(n_pool, 4, NSA_G * NSA_DK, PAGE_SIZE)
    o_t = _nsa_decode(pool_t, page_table, kv_s,
                      cache_win_kv[l].reshape(Bd, nwin, COL_WIN), win_s,
                      _q_to_lanes(q_s.reshape(Bd, Td, NSA_WIDTH), Td, tqs),
                      _gate_to_lanes(gate_s.reshape(Bd, Td, LANES), Td, tqs), w1k, pek, w2k, tq=tqs)
    o_nsa_s = _o_from_lanes(o_t, Td, tqs)
    omix_t = jnp.concatenate([o_hg_s, o_nsa_s], axis=-1).transpose(1, 0, 2).reshape(Td * Bd, D_MODEL)
    y_t, conv_t = _ffn_steps(x_sample.transpose(1, 0, 2).reshape(Td * Bd, D_MODEL), omix_t,
                             state_ffn_conv[l].transpose(1, 0, 2), ffn_w, n_steps=Td, nb=Bd)
    y_s = y_t.reshape(Td, Bd, D_MODEL).transpose(1, 0, 2)
    conv_s = conv_t.transpose(1, 0, 2)

    win_keep_p = min(WINDOW, T)
    win_all_s = jnp.concatenate([cache_win_kv[l].reshape(Bd, nwin, COL_WIN), win_s], axis=1)[:, -nwin:]
    kv_rows_p = kv_t.reshape(1, B, 4, NSA_G, NSA_DK, T).transpose(0, 1, 5, 2, 3, 4)
    return (y_p, y_s,
            kv_rows_p, kv_s.reshape(1, Bd, Td, 4, NSA_G, NSA_DK),
            win_p[:, T - win_keep_p:].reshape(1, B, win_keep_p, 2, NSA_G, NSA_DK),
            win_all_s.reshape(1, Bd, nwin, 2, NSA_G, NSA_DK),
            s_p[None], s_s[None], conv_p[None], conv_s[None])
```

```python
import functools

import numpy as np
import jax
import jax.numpy as jnp
from jax import lax
from jax.experimental import pallas as pl
from jax.experimental.pallas import tpu as pltpu

F32 = jnp.float32
BF16 = jnp.bfloat16

D_MODEL = 1024
PAGE_SIZE = 128
HG_WIDTH = 512
HG_HEADS = 4
HG_DK = 128
HG_CHUNK = 64
NSA_WIDTH = 512
NSA_HEADS = 8
NSA_DK = 64
NSA_G = 2
NSA_R = 4
CMP_STRIDE = 16
CMP_BLOCK = 32
CMP_HIDDEN = 256
SLC_BLOCK = 64
SLC_TOPK = 16
WINDOW = 512
D_FF = 2816
CONV_W = 3
EPS = 1e-6
COL_KV = 512
COL_WIN = 256
COL_GATE = 24

LANES = 128
SUBLANES = 8
HG_SUB = 16
HG_UNROLL = 8
HG_MASKED_EXP = -1e30
VMEM_LIMIT = 56 * 1024 * 1024

_C_HG = 4 * HG_WIDTH
_C_Q = _C_HG + NSA_WIDTH
_C_KV = _C_Q + COL_KV
_C_WIN = _C_KV + COL_WIN
_C_GATE = _C_WIN + LANES
D_IN = _C_WIN + COL_GATE


def _sigmoid(x):
    return 1.0 / (1.0 + jnp.exp(-x))


def _split3_bf16(x):
    hi = x.astype(BF16)
    r1 = x - hi.astype(F32)
    mid = r1.astype(BF16)
    lo = (r1 - mid.astype(F32)).astype(BF16)
    return hi, mid, lo


def _in_proj_kernel(x_ref, g_ref, w_ref, hg_ref, q_ref, kv_ref, win_ref, gate_ref, *kvt_ref):
    x = x_ref[...]
    ms = jnp.mean(x * x, axis=-1, keepdims=True)
    h = ((x * lax.rsqrt(ms + EPS)) * g_ref[...]).astype(BF16)
    hg_ref[...] = jnp.dot(h, w_ref[:, 0:_C_HG], preferred_element_type=F32)
    q_ref[...] = jnp.dot(h, w_ref[:, _C_HG:_C_Q], preferred_element_type=F32)
    kv = jnp.dot(h, w_ref[:, _C_Q:_C_KV], preferred_element_type=F32)
    kv_ref[...] = kv
    win_ref[...] = jnp.dot(h, w_ref[:, _C_KV:_C_WIN], preferred_element_type=F32)
    gate_ref[...] = jnp.dot(h, w_ref[:, _C_WIN:_C_GATE], preferred_element_type=F32)
    if kvt_ref:
        kvt_ref[0][0] = kv.T


def _in_proj(x2d, norm_w, w_in_bf, *, seq_for_kv_t=None):
    n = x2d.shape[0]
    tm = 512
    assert n % tm == 0
    widths = (_C_HG, NSA_WIDTH, COL_KV, COL_WIN, LANES)
    out_specs = [pl.BlockSpec((tm, w), lambda i: (i, 0)) for w in widths]
    out_shape = [jax.ShapeDtypeStruct((n, w), F32) for w in widths]
    if seq_for_kv_t is not None:
        tiles = seq_for_kv_t // tm
        assert seq_for_kv_t % tm == 0
        out_specs.append(pl.BlockSpec((1, COL_KV, tm), lambda i: (i // tiles, 0, i % tiles)))
        out_shape.append(jax.ShapeDtypeStruct((n // seq_for_kv_t, COL_KV, seq_for_kv_t), F32))
    return pl.pallas_call(
        _in_proj_kernel,
        grid=(n // tm,),
        in_specs=[
            pl.BlockSpec((tm, D_MODEL), lambda i: (i, 0)),
            pl.BlockSpec((1, D_MODEL), lambda i: (0, 0)),
            pl.BlockSpec((D_MODEL, _C_GATE), lambda i: (0, 0)),
        ],
        out_specs=out_specs,
        out_shape=out_shape,
        compiler_params=pltpu.CompilerParams(dimension_semantics=("arbitrary",), vmem_limit_bytes=VMEM_LIMIT),
        name="in_proj",
    )(x2d, norm_w.reshape(1, D_MODEL), w_in_bf)


def _hgrn_kernel(q_ref, f_ref, i_ref, g_ref, lb_ref, nw_ref, s0_ref, e_ref, o_ref, sfin_ref, st_ref,
                 *, chunk, rows_per_step, t_valid, layer):
    C = chunk
    nsb = C // HG_SUB
    ti = pl.program_id(2)

    hps = st_ref.shape[0]

    @pl.when(ti == 0)
    def _init():
        for hh in range(hps):
            st_ref[hh] = s0_ref[0, hh].T

    lg = lb_ref[...]
    le = jnp.exp(lg - jnp.max(lg, axis=0, keepdims=True))
    lb_all = jnp.sum(le[0:layer + 1, :], axis=0, keepdims=True) / jnp.sum(le, axis=0, keepdims=True)
    nw_all = nw_ref[...]
    row_i = lax.broadcasted_iota(jnp.int32, (C, C), 0)
    col_i = lax.broadcasted_iota(jnp.int32, (C, C), 1)
    tri_bf = jnp.where(col_i <= row_i, 1.0, 0.0).astype(BF16)
    diag_mask = ((row_i // HG_SUB) == (col_i // HG_SUB)) & (col_i <= row_i)
    off_mask = (col_i // HG_SUB) < (row_i // HG_SUB)
    trel = lax.broadcasted_iota(jnp.int32, (C, HG_DK), 0) % HG_SUB
    nt_dims = (((1,), (1,)), ((), ()))
    tn_dims = (((0,), (0,)), ((), ()))

    def do_chunk(hh, c):
        lanes = slice(hh * HG_DK, (hh + 1) * HG_DK)
        lb = lb_all[:, lanes]
        one_m_lb = 1.0 - lb
        nw = nw_all[:, lanes]
        r0 = pl.multiple_of(c * C, C)
        rows = pl.ds(r0, C)
        fr = f_ref[0, rows, lanes]
        qr = q_ref[0, rows, lanes]
        v = i_ref[0, rows, lanes]
        gr = g_ref[0, rows, lanes]
        f = lb + one_m_lb * _sigmoid(fr)
        kk = one_m_lb * _sigmoid(-fr)
        logf = jnp.log2(f)
        if t_valid is not None:
            tok = ti * rows_per_step + r0 + lax.broadcasted_iota(jnp.int32, (C, HG_DK), 0)
            ok = tok < t_valid
            logf = jnp.where(ok, logf, 0.0)
            kk = jnp.where(ok, kk, 0.0)
        q = qr * _sigmoid(qr)
        hi, mid, lo = _split3_bf16(logf)
        a = (jnp.dot(tri_bf, hi, preferred_element_type=F32)
             + jnp.dot(tri_bf, mid, preferred_element_type=F32)
             + jnp.dot(tri_bf, lo, preferred_element_type=F32))
        st = st_ref[hh]
        vb = v.astype(BF16)
        o = lax.dot_general((q * jnp.exp2(a)).astype(BF16), st.astype(BF16), nt_dims, preferred_element_type=F32)

        if nsb > 1:
            parts = [jnp.zeros((HG_SUB, C), F32)]
            for i in range(1, nsb):
                bi = a[HG_SUB * i - 1:HG_SUB * i, :]
                qi = q[HG_SUB * i:HG_SUB * (i + 1), :] * jnp.exp2(a[HG_SUB * i:HG_SUB * (i + 1), :] - bi)
                ki = kk * jnp.exp2(jnp.minimum(bi - a, 0.0))
                parts.append(lax.dot_general(qi.astype(BF16), ki.astype(BF16), nt_dims, preferred_element_type=F32))
            att = jnp.where(off_mask, jnp.concatenate(parts, axis=0), 0.0)
        else:
            att = jnp.zeros((C, C), F32)

        a3 = a.reshape(nsb, HG_SUB, HG_DK)
        k3 = kk.reshape(nsb, HG_SUB, HG_DK)
        zs = []
        for j in range(HG_SUB):
            aj = jnp.broadcast_to(a3[:, j:j + 1, :], (nsb, HG_SUB, HG_DK)).reshape(C, HG_DK)
            kj = jnp.broadcast_to(k3[:, j:j + 1, :], (nsb, HG_SUB, HG_DK)).reshape(C, HG_DK)
            expo = a - aj if j == 0 else jnp.where(trel >= j, a - aj, HG_MASKED_EXP)
            zs.append((q * kj * jnp.exp2(expo)).astype(BF16))
        zcat = jnp.concatenate(zs, axis=1)
        att_d = jnp.dot(zcat, e_ref[...], preferred_element_type=F32)
        att = att + jnp.where(diag_mask, att_d, 0.0)
        o = o + jnp.dot(att.astype(BF16), vb, preferred_element_type=F32)

        a_end = a[C - 1:C, :]
        kdec = kk * jnp.exp2(a_end - a)
        st_ref[hh] = st * jnp.exp2(a_end) + lax.dot_general(vb, kdec.astype(BF16), tn_dims, preferred_element_type=F32)

        on = (o * lax.rsqrt(jnp.mean(o * o, axis=-1, keepdims=True) + EPS)) * nw
        o_ref[0, rows, lanes] = on * (gr * _sigmoid(gr))

    n_chunks = rows_per_step // C
    if n_chunks == 1:
        for hh in range(hps):
            do_chunk(hh, 0)
    else:
        for hh in range(hps):
            def chunk_body(c, carry, hh=hh):
                do_chunk(hh, c)
                return carry
            lax.fori_loop(0, n_chunks, chunk_body, 0, unroll=min(n_chunks, HG_UNROLL))

    @pl.when(ti == pl.num_programs(2) - 1)
    def _fin():
        for hh in range(hps):
            sfin_ref[0, hh] = st_ref[hh].T


def _hgrn(z_hg, lb_logits, norm_w, s0, *, chunk, rows_per_step, t_valid, heads_per_step=1, layer=0):
    b, t, _ = z_hg.shape
    n_lb = lb_logits.shape[0]
    hps = heads_per_step
    assert t % rows_per_step == 0 and rows_per_step % chunk == 0 and chunk % HG_SUB == 0 and HG_HEADS % hps == 0
    nt = t // rows_per_step
    hblocks = HG_HEADS // hps
    width = hps * HG_DK
    sub = np.arange(chunk) % HG_SUB
    e_mat = jnp.asarray((np.repeat(np.arange(HG_SUB), HG_DK)[:, None] == sub[None, :]).astype(np.float32), BF16)

    def zspec(k):
        return pl.BlockSpec((1, rows_per_step, width), lambda bi, h, ti, k=k: (bi, ti, k * hblocks + h))

    kern = functools.partial(_hgrn_kernel, chunk=chunk, rows_per_step=rows_per_step, t_valid=t_valid, layer=layer)
    return pl.pallas_call(
        kern,
        grid=(b, hblocks, nt),
        in_specs=[
            zspec(0), zspec(1), zspec(2), zspec(3),
            pl.BlockSpec((n_lb, width), lambda bi, h, ti: (0, h)),
            pl.BlockSpec((1, width), lambda bi, h, ti: (0, h)),
            pl.BlockSpec((1, hps, HG_DK, HG_DK), lambda bi, h, ti: (bi, h, 0, 0)),
            pl.BlockSpec((HG_SUB * HG_DK, chunk), lambda bi, h, ti: (0, 0)),
        ],
        out_specs=[
            pl.BlockSpec((1, rows_per_step, width), lambda bi, h, ti: (bi, ti, h)),
            pl.BlockSpec((1, hps, HG_DK, HG_DK), lambda bi, h, ti: (bi, h, 0, 0)),
        ],
        out_shape=[jax.ShapeDtypeStruct((b, t, HG_WIDTH), F32), jax.ShapeDtypeStruct((b, HG_HEADS, HG_DK, HG_DK), F32)],
        scratch_shapes=[pltpu.VMEM((hps, HG_DK, HG_DK), F32)],
        compiler_params=pltpu.CompilerParams(dimension_semantics=("arbitrary", "arbitrary", "arbitrary"),
                                             vmem_limit_bytes=VMEM_LIMIT),
        name="hgrn2",
    )(z_hg, z_hg, z_hg, z_hg, lb_logits, norm_w.reshape(1, HG_WIDTH), s0, e_mat)


KEY_TILE = 256
CMP_PACK = 4
LOG2E = 1.4426950408889634
Q_SCALE = LOG2E * NSA_DK ** -0.5
NEG_INF = float("-inf")


def _compress_pair(load_xs, w1_ref, pe_ref, w2_ref, slot, nch):
    r = jnp.zeros((SUBLANES, 2 * CMP_HIDDEN), F32)
    for u in range(CMP_STRIDE // CMP_PACK):
        r = r + jnp.dot(pe_ref[slot, u], w1_ref[slot, u], preferred_element_type=F32)
    bias = r[0:1, 0:CMP_HIDDEN] + r[1:2, CMP_HIDDEN:2 * CMP_HIDDEN]
    outs = []
    for g in range(NSA_G):
        acc = jnp.zeros((nch, 2 * CMP_HIDDEN), F32)
        for u in range(CMP_STRIDE // CMP_PACK):
            xg = jnp.concatenate([load_xs(CMP_PACK * u + i)[:, NSA_DK * g:NSA_DK * (g + 1)] for i in range(CMP_PACK)],
                                 axis=1).astype(BF16)
            acc = acc + jnp.dot(xg, w1_ref[slot, u], preferred_element_type=F32)
        h = acc[:, 0:CMP_HIDDEN] + pltpu.roll(acc[:, CMP_HIDDEN:2 * CMP_HIDDEN], nch - 1, axis=0) + bias
        hs = (h * _sigmoid(h)).astype(BF16)
        outs.append(jnp.dot(hs, w2_ref[slot], preferred_element_type=F32))
    return jnp.concatenate(outs, axis=1)


def _rank_bias(score, n_slc):
    nj, w = score.shape
    jrow = lax.broadcasted_iota(jnp.int32, (SUBLANES, w), 0)
    slabs = [score[SUBLANES * v:SUBLANES * (v + 1), :] for v in range(nj // SUBLANES)]
    ranks = [jnp.zeros((SUBLANES, w), F32) for _ in slabs]
    for jp in range(n_slc):
        row = score[jp:jp + 1, :]
        for v, slab in enumerate(slabs):
            lo = SUBLANES * v
            if lo > jp:
                beats = row >= slab
            elif lo + SUBLANES - 1 <= jp:
                beats = row > slab
            else:
                beats = (row > slab) | ((row == slab) & (jrow > jp - lo))
            ranks[v] = ranks[v] + jnp.where(beats, 1.0, 0.0)
    rank = jnp.concatenate(ranks, axis=0)
    return jnp.where((rank < float(SLC_TOPK)) & (score > NEG_INF), 0.0, NEG_INF)


def _cmp_branch(g, qtz, qpos, tq, nq, kc_ref, vct_ref, ov_ref, n_blk):
    assert tq % LANES == 0
    nb_pad = kc_ref.shape[0]
    nj_pad = ov_ref.shape[0]
    s = jnp.dot(kc_ref[...], qtz, preferred_element_type=F32)
    n_idx = lax.broadcasted_iota(jnp.int32, (nb_pad, nq), 0)
    cmask = (n_idx * CMP_STRIDE + (CMP_BLOCK - 1) <= qpos) & (n_idx < n_blk)
    s = jnp.where(cmask, s, NEG_INF)
    m = jnp.max(s, axis=0, keepdims=True)
    m = jnp.where(m == NEG_INF, 0.0, m)
    e = jnp.exp2(s - m)
    d = jnp.sum(e, axis=0, keepdims=True)
    p = e / jnp.where(d > 0.0, d, 1.0)
    o_cmp = jnp.dot(vct_ref[NSA_DK * g:NSA_DK * (g + 1), :], p.astype(BF16), preferred_element_type=F32)

    ps = p[:, 0:tq]
    for r in range(1, NSA_R):
        ps = ps + p[:, r * tq:(r + 1) * tq]
    qpos_w = qpos[:, 0:tq]
    w = tq
    hi, mid, lo = _split3_bf16(ps)
    ov = ov_ref[...]
    imp = (jnp.dot(ov, hi, preferred_element_type=F32) + jnp.dot(ov, mid, preferred_element_type=F32)
           + jnp.dot(ov, lo, preferred_element_type=F32))
    j_idx = lax.broadcasted_iota(jnp.int32, (nj_pad, w), 0)
    cur = qpos_w // SLC_BLOCK
    forced = (j_idx == 0) | (j_idx == cur) | (j_idx == cur - 1)
    score = jnp.where(forced, jnp.inf, jnp.where(j_idx <= cur, imp, NEG_INF))
    return o_cmp, score


def _select_blocks(scores, n_slc, tq, nq, selb_ref):
    w = scores[0].shape[1]
    bias = _rank_bias(jnp.concatenate(scores, axis=1), n_slc)
    for g in range(NSA_G):
        selb_ref[g] = jnp.concatenate([bias[:, g * w:(g + 1) * w]] * (nq // w), axis=1)


def _block_rows(rows, nq):
    n = rows.shape[0]
    return jnp.broadcast_to(rows[:, None, :], (n, SLC_BLOCK, nq)).reshape(n * SLC_BLOCK, nq)


def _attn_step(carry, k, qtz, vts, biases):
    out = []
    for g in range(NSA_G):
        m, l, acc = carry[g]
        s = jnp.dot(k, qtz[g], preferred_element_type=F32) + biases[g]
        m_new = jnp.maximum(m, jnp.max(s, axis=0, keepdims=True))
        m_safe = jnp.where(m_new == NEG_INF, 0.0, m_new)
        alpha = jnp.exp2(m - m_safe)
        p = jnp.exp2(s - m_safe)
        l = l * alpha + jnp.sum(p, axis=0, keepdims=True)
        acc = acc * alpha + jnp.dot(vts[g], p.astype(BF16), preferred_element_type=F32)
        out.append((m_new, l, acc))
    return tuple(out)


def _attn_init(nq):
    return tuple((jnp.full((1, nq), NEG_INF, F32), jnp.zeros((1, nq), F32), jnp.zeros((NSA_DK, nq), F32))
                 for _ in range(NSA_G))


def _attn_finish(carry):
    return [acc / jnp.where(l > 0.0, l, 1.0) for (_, l, acc) in carry]


def _causal_bias(kpos, qpos):
    return jnp.where(kpos <= qpos, 0.0, NEG_INF)


def _window_bias(kpos, qpos):
    d = kpos - qpos
    return jnp.where((d <= 0) & (d > -WINDOW), 0.0, NEG_INF)


def _gated_sum(gate_raw, o_cmp, o_slc, o_win):
    gt = _sigmoid(gate_raw)
    return gt[0:1, :] * o_cmp + gt[1:2, :] * o_slc + gt[2:3, :] * o_win


def _nsa_prompt_kernel(ck_ref, cv_ref, sk_ref, sv_ref, wk_ref, wv_ref, q_ref, gate_ref, ov_ref, w1_ref, pe_ref, w2_ref,
                       o_ref, kc_ref, vct_ref, svt_ref, wvt_ref, selb_ref, *, seq, tq):
    nq = NSA_R * tq
    nch = seq // CMP_STRIDE
    n_tiles = seq // KEY_TILE
    n_slc = seq // SLC_BLOCK
    blocks_per_tile = KEY_TILE // SLC_BLOCK
    assert 2 * blocks_per_tile == SUBLANES
    qt = pl.program_id(1)

    @pl.when(qt == 0)
    def _prep():
        kc = _compress_pair(lambda s: ck_ref[0, pl.ds(s, nch, stride=CMP_STRIDE), :], w1_ref, pe_ref, w2_ref, 0, nch)
        vc = _compress_pair(lambda s: cv_ref[0, pl.ds(s, nch, stride=CMP_STRIDE), :], w1_ref, pe_ref, w2_ref, 1, nch)
        kc_ref[...] = kc.astype(BF16)
        vct_ref[...] = vc.T.astype(BF16)
        for kt in range(n_tiles):
            rows = slice(kt * KEY_TILE, (kt + 1) * KEY_TILE)
            svt_ref[kt] = sv_ref[0, rows, :].T.astype(BF16)
            wvt_ref[kt] = wv_ref[0, rows, :].T.astype(BF16)

    t0 = qt * tq
    qpos = t0 + lax.broadcasted_iota(jnp.int32, (1, nq), 1) % tq
    hi_tile = (t0 + tq - 1) // KEY_TILE + 1
    win_lo = jnp.maximum(t0 - (WINDOW - 1), 0) // KEY_TILE
    krow = lax.broadcasted_iota(jnp.int32, (KEY_TILE, nq), 0)

    gate_t = gate_ref[0].T
    zeros_half = jnp.zeros((NSA_DK, nq), F32)
    qtz, gates = [], []
    for g in range(NSA_G):
        qg_t = q_ref[0, :, g * NSA_R * NSA_DK:(g + 1) * NSA_R * NSA_DK].T
        q_t = jnp.concatenate([qg_t[r * NSA_DK:(r + 1) * NSA_DK, :] for r in range(NSA_R)], axis=1) * Q_SCALE
        qtz.append(jnp.concatenate([q_t, zeros_half] if g == 0 else [zeros_half, q_t], axis=0).astype(BF16))
        gates.append(jnp.concatenate(
            [jnp.concatenate([gate_t[k * NSA_HEADS + g * NSA_R + r:k * NSA_HEADS + g * NSA_R + r + 1, :]
                              for r in range(NSA_R)], axis=1) for k in range(3)], axis=0))

    o_cmp, scores = [], []
    for g in range(NSA_G):
        oc, sc = _cmp_branch(g, qtz[g], qpos, tq, nq, kc_ref, vct_ref, ov_ref, nch - 1)
        o_cmp.append(oc)
        scores.append(sc)
    _select_blocks(scores, n_slc, tq, nq, selb_ref)

    def slc_step(kt, carry, causal):
        rows = pl.ds(pl.multiple_of(kt * KEY_TILE, KEY_TILE), KEY_TILE)
        k = sk_ref[0, rows, :].astype(BF16)
        biases = []
        for g in range(NSA_G):
            rows8 = selb_ref[g, pl.ds(pl.multiple_of((kt // 2) * SUBLANES, SUBLANES), SUBLANES), :]
            rows4 = jnp.where(kt % 2 == 0, rows8[0:blocks_per_tile, :], rows8[blocks_per_tile:, :])
            bg = _block_rows(rows4, nq)
            biases.append(bg + _causal_bias(kt * KEY_TILE + krow, qpos) if causal else bg)
        vts = [svt_ref[kt, NSA_DK * g:NSA_DK * (g + 1), :] for g in range(NSA_G)]
        return _attn_step(carry, k, qtz, vts, biases)

    def win_step(kt, carry):
        rows = pl.ds(pl.multiple_of(kt * KEY_TILE, KEY_TILE), KEY_TILE)
        k = wk_ref[0, rows, :].astype(BF16)
        wb = _window_bias(kt * KEY_TILE + krow, qpos)
        vts = [wvt_ref[kt, NSA_DK * g:NSA_DK * (g + 1), :] for g in range(NSA_G)]
        return _attn_step(carry, k, qtz, vts, [wb, wb])

    carry = lax.fori_loop(0, hi_tile - 1, lambda kt, c: slc_step(kt, c, False), _attn_init(nq))
    o_slc = _attn_finish(slc_step(hi_tile - 1, carry, True))
    o_win = _attn_finish(lax.fori_loop(win_lo, hi_tile, win_step, _attn_init(nq)))

    for g in range(NSA_G):
        o_t = _gated_sum(gates[g], o_cmp[g], o_slc[g], o_win[g])
        stack = jnp.concatenate([o_t[:, r * tq:(r + 1) * tq] for r in range(NSA_R)], axis=0)
        o_ref[0, :, g * NSA_R * NSA_DK:(g + 1) * NSA_R * NSA_DK] = stack.T


def _overlap_t(nj_pad, nb_pad):
    n = np.arange(nb_pad)[None, :]
    j = np.arange(nj_pad)[:, None]
    lo = np.maximum(n * CMP_STRIDE, j * SLC_BLOCK)
    hi = np.minimum(n * CMP_STRIDE + CMP_BLOCK, (j + 1) * SLC_BLOCK)
    return jnp.asarray(np.maximum(hi - lo, 0).astype(np.float32) / CMP_STRIDE, BF16)


def _nsa_prompt(kv_new, win_new, q, gate, w1k, pek, w2k, *, tq):
    b, seq, _ = kv_new.shape
    assert tq % LANES == 0 and tq <= KEY_TILE and seq % KEY_TILE == 0
    nq = NSA_R * tq
    nch = seq // CMP_STRIDE
    n_slc = seq // SLC_BLOCK
    n_tiles = seq // KEY_TILE
    ov = _overlap_t(n_slc, nch)
    kern = functools.partial(_nsa_prompt_kernel, seq=seq, tq=tq)
    return pl.pallas_call(
        kern,
        grid=(b, seq // tq),
        in_specs=[
            pl.BlockSpec((1, seq, LANES), lambda bi, qi: (bi, 0, 0)),
            pl.BlockSpec((1, seq, LANES), lambda bi, qi: (bi, 0, 1)),
            pl.BlockSpec((1, seq, LANES), lambda bi, qi: (bi, 0, 2)),
            pl.BlockSpec((1, seq, LANES), lambda bi, qi: (bi, 0, 3)),
            pl.BlockSpec((1, seq, LANES), lambda bi, qi: (bi, 0, 0)),
            pl.BlockSpec((1, seq, LANES), lambda bi, qi: (bi, 0, 1)),
            pl.BlockSpec((1, tq, NSA_WIDTH), lambda bi, qi: (bi, qi, 0)),
            pl.BlockSpec((1, tq, LANES), lambda bi, qi: (bi, qi, 0)),
            pl.BlockSpec((n_slc, nch), lambda bi, qi: (0, 0)),
            pl.BlockSpec(w1k.shape, lambda bi, qi: (0, 0, 0, 0)),
            pl.BlockSpec(pek.shape, lambda bi, qi: (0, 0, 0, 0)),
            pl.BlockSpec(w2k.shape, lambda bi, qi: (0, 0, 0)),
        ],
        out_specs=pl.BlockSpec((1, tq, NSA_WIDTH), lambda bi, qi: (bi, qi, 0)),
        out_shape=jax.ShapeDtypeStruct((b, seq, NSA_WIDTH), F32),
        scratch_shapes=[
            pltpu.VMEM((nch, LANES), BF16),
            pltpu.VMEM((LANES, nch), BF16),
            pltpu.VMEM((n_tiles, LANES, KEY_TILE), BF16),
            pltpu.VMEM((n_tiles, LANES, KEY_TILE), BF16),
            pltpu.VMEM((NSA_G, n_slc, nq), F32),
        ],
        compiler_params=pltpu.CompilerParams(dimension_semantics=("arbitrary", "arbitrary"), vmem_limit_bytes=VMEM_LIMIT),
        name="nsa_prompt",
    )(kv_new, kv_new, kv_new, kv_new, win_new, win_new, q, gate, ov, w1k, pek, w2k)


def _prep_cmp_weights(cmp_pe, cmp_w1, cmp_w2):
    groups = CMP_STRIDE // CMP_PACK
    w1k = jnp.concatenate([cmp_w1[:, 0], cmp_w1[:, 1]], axis=-1)
    w1k = w1k.reshape(2, groups, CMP_PACK * NSA_DK, 2 * CMP_HIDDEN).astype(BF16)
    pek = cmp_pe.reshape(2, 2, groups, CMP_PACK * NSA_DK).transpose(0, 2, 1, 3)
    pek = jnp.pad(pek, ((0, 0), (0, 0), (0, SUBLANES - 2), (0, 0))).astype(BF16)
    return w1k, pek, cmp_w2.astype(BF16)


def _softmax_rows(s):
    m = jnp.max(s, axis=1, keepdims=True)
    m = jnp.where(m == NEG_INF, 0.0, m)
    e = jnp.exp2(s - m)
    return e, jnp.sum(e, axis=1, keepdims=True)


def _nsa_step_kernel(pt_ref, pool_ref, kvn_ref, wc_ref, wn_ref, qz_ref, gate_ref, ov_ref, w1_ref, pe_ref, w2_ref,
                     o_ref, stg, sem, ckb, cvb, skt_ref, svt_ref, wkt_ref, wvt_ref, tmp, kc_ref, vc_ref,
                     *, past, n_new):
    n_pages = past // PAGE_SIZE
    nch = (past + n_new) // CMP_STRIDE
    assert nch * CMP_STRIDE <= past
    n_slc = -(-(past + n_new) // SLC_BLOCK)
    s_pad = skt_ref.shape[1]
    w_pad = wkt_ref.shape[1]
    nwin = wc_ref.shape[3]
    njl = ov_ref.shape[1]
    rows = NSA_R * n_new
    b = pl.program_id(0)
    nb = pl.num_programs(0)

    def page_copy(bb, p):
        return pltpu.make_async_copy(pool_ref.at[pt_ref[bb * n_pages + p]], stg.at[p], sem.at[0])

    def start_fetch(bb):
        def body(p, carry):
            page_copy(bb, p).start()
            return carry
        lax.fori_loop(0, n_pages, body, 0)

    @pl.when(b == 0)
    def _first():
        start_fetch(0)

    def new_tile_t(rows_new):
        tmp[...] = jnp.zeros(tmp.shape, F32)
        tmp[0:n_new, :] = rows_new
        return tmp[...].T.astype(BF16)

    wkt_ref[:, 0:nwin] = wc_ref[0, 0].astype(BF16)
    wvt_ref[:, 0:nwin] = wc_ref[0, 1].astype(BF16)
    wkt_ref[:, nwin:] = new_tile_t(wn_ref[0, :, 0:LANES])
    wvt_ref[:, nwin:] = new_tile_t(wn_ref[0, :, LANES:2 * LANES])
    skt_ref[:, past:] = new_tile_t(kvn_ref[0, :, 2 * LANES:3 * LANES])
    svt_ref[:, past:] = new_tile_t(kvn_ref[0, :, 3 * LANES:4 * LANES])

    pltpu.make_async_copy(pool_ref.at[pl.ds(0, n_pages)], stg, sem.at[0]).wait()
    for p in range(n_pages):
        cols = slice(p * PAGE_SIZE, (p + 1) * PAGE_SIZE)
        ckb[cols, :] = stg[p, 0].T
        cvb[cols, :] = stg[p, 1].T
        skt_ref[:, cols] = stg[p, 2].astype(BF16)
        svt_ref[:, cols] = stg[p, 3].astype(BF16)

    @pl.when(b + 1 < nb)
    def _next():
        start_fetch(b + 1)

    kc_ref[...] = _compress_pair(lambda s: ckb[pl.ds(s, nch, stride=CMP_STRIDE), :], w1_ref, pe_ref, w2_ref, 0, nch).astype(BF16)
    vc_ref[...] = _compress_pair(lambda s: cvb[pl.ds(s, nch, stride=CMP_STRIDE), :], w1_ref, pe_ref, w2_ref, 1, nch).astype(BF16)

    nt_dims = (((1,), (1,)), ((), ()))
    qpos = past + lax.broadcasted_iota(jnp.int32, (rows, 1), 0) % n_new
    qz = [qz_ref[0, g] for g in range(NSA_G)]

    n_lane = lax.broadcasted_iota(jnp.int32, (1, nch), 1)
    cmask = (n_lane * CMP_STRIDE + (CMP_BLOCK - 1) <= qpos) & (n_lane < nch - 1)
    o_cmp, ps = [], []
    for g in range(NSA_G):
        s = lax.dot_general(qz[g], kc_ref[...], nt_dims, preferred_element_type=F32)
        e, d = _softmax_rows(jnp.where(cmask, s, NEG_INF))
        p = e / jnp.where(d > 0.0, d, 1.0)
        o_cmp.append(jnp.dot(p.astype(BF16), vc_ref[...], preferred_element_type=F32))
        pg = p[0:n_new, :]
        for r in range(1, NSA_R):
            pg = pg + p[r * n_new:(r + 1) * n_new, :]
        ps.append(pg)
    hi, mid, lo = _split3_bf16(jnp.concatenate(ps, axis=0))
    ov = ov_ref[...]
    imp = (jnp.dot(hi, ov, preferred_element_type=F32) + jnp.dot(mid, ov, preferred_element_type=F32)
           + jnp.dot(lo, ov, preferred_element_type=F32))
    gt_rows = NSA_G * n_new
    j_lane = lax.broadcasted_iota(jnp.int32, (gt_rows, njl), 1)
    cur = (past + lax.broadcasted_iota(jnp.int32, (gt_rows, 1), 0) % n_new) // SLC_BLOCK
    forced = (j_lane == 0) | (j_lane == cur) | (j_lane == cur - 1)
    score = jnp.where(forced, jnp.inf, jnp.where(j_lane <= cur, imp, NEG_INF))
    rank = jnp.zeros((gt_rows, njl), F32)
    for jp in range(n_slc):
        col = score[:, jp:jp + 1]
        beats = (col > score) | ((col == score) & (j_lane > jp))
        rank = rank + jnp.where(beats, 1.0, 0.0)
    sel_bias = jnp.where((rank < float(SLC_TOPK)) & (score > NEG_INF), 0.0, NEG_INF)

    lane = lax.broadcasted_iota(jnp.int32, (1, PAGE_SIZE), 1)
    first_block = lane < SLC_BLOCK
    assert PAGE_SIZE == 2 * SLC_BLOCK
    wpos = past - nwin + lax.broadcasted_iota(jnp.int32, (1, w_pad), 1) - qpos
    wmask = (wpos <= 0) & (wpos > -WINDOW)
    outs = []
    for g in range(NSA_G):
        sb = jnp.concatenate([sel_bias[g * n_new:(g + 1) * n_new, :]] * NSA_R, axis=0)
        s = jnp.dot(qz[g], skt_ref[...], preferred_element_type=F32)
        tiles = []
        for p in range(s_pad // PAGE_SIZE):
            t = s[:, p * PAGE_SIZE:(p + 1) * PAGE_SIZE] + jnp.where(first_block, sb[:, 2 * p:2 * p + 1], sb[:, 2 * p + 1:2 * p + 2])
            if (p + 1) * PAGE_SIZE > past:
                t = jnp.where(p * PAGE_SIZE + lane <= qpos, t, NEG_INF)
            tiles.append(t)
        e, d = _softmax_rows(jnp.concatenate(tiles, axis=1))
        o_slc = lax.dot_general(e.astype(BF16), svt_ref[...], nt_dims, preferred_element_type=F32) / jnp.where(d > 0.0, d, 1.0)
        s = jnp.dot(qz[g], wkt_ref[...], preferred_element_type=F32)
        e, d = _softmax_rows(jnp.where(wmask, s, NEG_INF))
        o_win = lax.dot_general(e.astype(BF16), wvt_ref[...], nt_dims, preferred_element_type=F32) / jnp.where(d > 0.0, d, 1.0)
        gt = _sigmoid(gate_ref[0, g])
        outs.append(gt[:, 0:1] * o_cmp[g] + gt[:, 1:2] * o_slc + gt[:, 2:3] * o_win)
    assert NSA_G == 2
    o_ref[0] = jnp.where(lax.broadcasted_iota(jnp.int32, (rows, LANES), 1) < NSA_DK, outs[0], outs[1])


def _nsa_step(pool, page_table, kv_new, win_cache_t, win_new, q, gate, w1k, pek, w2k):
    b, n_pages = page_table.shape
    n_new = kv_new.shape[1]
    nwin = win_cache_t.shape[3]
    past = n_pages * PAGE_SIZE
    rows = NSA_R * n_new
    nch = (past + n_new) // CMP_STRIDE
    s_pad = past + PAGE_SIZE
    w_pad = nwin + PAGE_SIZE
    njl = -(-(2 * s_pad // PAGE_SIZE) // LANES) * LANES
    assert n_new <= PAGE_SIZE and nwin % LANES == 0 and rows % 16 == 0
    ov = _overlap_t(njl, nch).T
    x = (q * Q_SCALE).reshape(b, n_new, NSA_G, NSA_R, NSA_DK).transpose(0, 2, 3, 1, 4).reshape(b, NSA_G, rows, NSA_DK)
    z = jnp.zeros_like(x[:, 0])
    qz = jnp.stack([jnp.concatenate([x[:, 0], z], axis=-1), jnp.concatenate([z, x[:, 1]], axis=-1)], axis=1).astype(BF16)
    gz = gate[..., :COL_GATE].reshape(b, n_new, 3, NSA_G, NSA_R).transpose(0, 3, 4, 1, 2).reshape(b, NSA_G, rows, 3)
    gz = jnp.pad(gz, ((0, 0), (0, 0), (0, 0), (0, LANES - 3)))
    kern = functools.partial(_nsa_step_kernel, past=past, n_new=n_new)
    grid_spec = pltpu.PrefetchScalarGridSpec(
        num_scalar_prefetch=1,
        grid=(b,),
        in_specs=[
            pl.BlockSpec(memory_space=pl.ANY),
            pl.BlockSpec((1, n_new, COL_KV), lambda bi, pt: (bi, 0, 0)),
            pl.BlockSpec((1, 2, LANES, nwin), lambda bi, pt: (bi, 0, 0, 0)),
            pl.BlockSpec((1, n_new, COL_WIN), lambda bi, pt: (bi, 0, 0)),
            pl.BlockSpec((1, NSA_G, rows, LANES), lambda bi, pt: (bi, 0, 0, 0)),
            pl.BlockSpec((1, NSA_G, rows, LANES), lambda bi, pt: (bi, 0, 0, 0)),
            pl.BlockSpec((nch, njl), lambda bi, pt: (0, 0)),
            pl.BlockSpec(w1k.shape, lambda bi, pt: (0, 0, 0, 0)),
            pl.BlockSpec(pek.shape, lambda bi, pt: (0, 0, 0, 0)),
            pl.BlockSpec(w2k.shape, lambda bi, pt: (0, 0, 0)),
        ],
        out_specs=pl.BlockSpec((1, rows, LANES), lambda bi, pt: (bi, 0, 0)),
        scratch_shapes=[
            pltpu.VMEM((n_pages, 4, LANES, PAGE_SIZE), F32),
            pltpu.SemaphoreType.DMA((1,)),
            pltpu.VMEM((past, LANES), F32),
            pltpu.VMEM((past, LANES), F32),
            pltpu.VMEM((LANES, s_pad), BF16),
            pltpu.VMEM((LANES, s_pad), BF16),
            pltpu.VMEM((LANES, w_pad), BF16),
            pltpu.VMEM((LANES, w_pad), BF16),
            pltpu.VMEM((PAGE_SIZE, LANES), F32),
            pltpu.VMEM((nch, LANES), BF16),
            pltpu.VMEM((nch, LANES), BF16),
        ],
    )
    o = pl.pallas_call(
        kern,
        grid_spec=grid_spec,
        out_shape=jax.ShapeDtypeStruct((b, rows, LANES), F32),
        compiler_params=pltpu.CompilerParams(dimension_semantics=("arbitrary",), vmem_limit_bytes=VMEM_LIMIT),
        name="nsa_step",
    )(page_table.reshape(-1), pool, kv_new, win_cache_t, win_new, qz, gz, ov, w1k, pek, w2k)
    return o.reshape(b, NSA_R, n_new, NSA_G, NSA_DK).transpose(0, 2, 3, 1, 4).reshape(b, n_new, NSA_WIDTH)


FF_TILE = 256


def _rmsnorm(x, w):
    return (x * lax.rsqrt(jnp.mean(x * x, axis=-1, keepdims=True) + EPS)) * w


def _gelu_tanh(c):
    return c * (0.5 * (1.0 + jnp.tanh(0.7978845608028654 * (c + 0.044715 * (c * c * c)))))


def _ffn_kernel(x_ref, ohg_ref, onsa_ref, past_ref, wout_ref, n2_ref, wg_ref, wv_ref, cw_ref, cb_ref, wd_ref, nf_ref,
                y_ref, conv_ref, carry_ref, act_ref, *, tm):
    ti = pl.program_id(1)

    @pl.when(ti == 0)
    def _init():
        carry_ref[0:CONV_W - 1, :] = past_ref[0]

    x1 = (x_ref[0]
          + jnp.dot(ohg_ref[0].astype(BF16), wout_ref[0:HG_WIDTH, :], preferred_element_type=F32)
          + jnp.dot(onsa_ref[0].astype(BF16), wout_ref[HG_WIDTH:, :], preferred_element_type=F32))
    h = _rmsnorm(x1, n2_ref[...]).astype(BF16)
    row = lax.broadcasted_iota(jnp.int32, (tm, FF_TILE), 0)
    for j in range(D_FF // FF_TILE):
        cols = slice(j * FF_TILE, (j + 1) * FF_TILE)
        u = jnp.dot(h, wg_ref[:, cols], preferred_element_type=F32)
        val = jnp.dot(h, wv_ref[:, cols], preferred_element_type=F32)
        c0 = carry_ref[0:1, cols]
        c1 = carry_ref[1:2, cols]
        up1 = jnp.where(row == 0, c1, pltpu.roll(u, 1, axis=0))
        up2 = jnp.where(row == 0, c0, jnp.where(row == 1, c1, pltpu.roll(u, 2, axis=0)))
        c = cb_ref[:, cols] + up2 * cw_ref[0:1, cols] + up1 * cw_ref[1:2, cols] + u * cw_ref[2:3, cols]
        tail = u[tm - (CONV_W - 1):tm, :]
        carry_ref[0:CONV_W - 1, cols] = tail
        conv_ref[0, :, cols] = tail
        act_ref[:, cols] = (_gelu_tanh(c) * val).astype(BF16)
    y = jnp.dot(act_ref[...], wd_ref[...], preferred_element_type=F32)
    y_ref[0] = _rmsnorm(x1 + y, nf_ref[...])


def _ffn(x, o_hg, o_nsa, conv_past, wts, *, tm):
    b, t, _ = x.shape
    assert t % tm == 0 and D_FF % FF_TILE == 0
    wout, n2, wg, wv, cw, cb, wd, nf = wts
    const2 = lambda bi, ti: (0, 0)
    one = pl.Buffered(1)
    return pl.pallas_call(
        functools.partial(_ffn_kernel, tm=tm),
        grid=(b, t // tm),
        in_specs=[
            pl.BlockSpec((1, tm, D_MODEL), lambda bi, ti: (bi, ti, 0)),
            pl.BlockSpec((1, tm, HG_WIDTH), lambda bi, ti: (bi, ti, 0)),
            pl.BlockSpec((1, tm, NSA_WIDTH), lambda bi, ti: (bi, ti, 0)),
            pl.BlockSpec((1, CONV_W - 1, D_FF), lambda bi, ti: (bi, 0, 0)),
            pl.BlockSpec(wout.shape, const2, pipeline_mode=one),
            pl.BlockSpec(n2.shape, const2),
            pl.BlockSpec(wg.shape, const2, pipeline_mode=one),
            pl.BlockSpec(wv.shape, const2, pipeline_mode=one),
            pl.BlockSpec(cw.shape, const2),
            pl.BlockSpec(cb.shape, const2),
            pl.BlockSpec(wd.shape, const2, pipeline_mode=one),
            pl.BlockSpec(nf.shape, const2),
        ],
        out_specs=[
            pl.BlockSpec((1, tm, D_MODEL), lambda bi, ti: (bi, ti, 0)),
            pl.BlockSpec((1, CONV_W - 1, D_FF), lambda bi, ti: (bi, 0, 0)),
        ],
        out_shape=[jax.ShapeDtypeStruct((b, t, D_MODEL), F32), jax.ShapeDtypeStruct((b, CONV_W - 1, D_FF), F32)],
        scratch_shapes=[pltpu.VMEM((SUBLANES, D_FF), F32), pltpu.VMEM((tm, D_FF), BF16)],
        compiler_params=pltpu.CompilerParams(dimension_semantics=("arbitrary", "arbitrary"), vmem_limit_bytes=VMEM_LIMIT),
        name="ffn",
    )(x, o_hg, o_nsa, conv_past, wout, n2, wg, wv, cw, cb, wd, nf)


def _ffn_steps_kernel(x_ref, om_ref, past_ref, wout_ref, n2_ref, wg_ref, wv_ref, cw_ref, cb_ref, wd_ref, nf_ref,
                      y_ref, conv_ref, *, n_steps, nb):
    x1 = x_ref[...] + jnp.dot(om_ref[...].astype(BF16), wout_ref[...], preferred_element_type=F32)
    h = _rmsnorm(x1, n2_ref[...]).astype(BF16)
    acc = jnp.zeros((n_steps * nb, D_MODEL), F32)
    for j in range(D_FF // FF_TILE):
        cols = slice(j * FF_TILE, (j + 1) * FF_TILE)
        u = jnp.dot(h, wg_ref[:, cols], preferred_element_type=F32)
        val = jnp.dot(h, wv_ref[:, cols], preferred_element_type=F32)
        taps = [past_ref[k, :, cols] for k in range(CONV_W - 1)] + [u[t * nb:(t + 1) * nb, :] for t in range(n_steps)]
        cs = []
        for t in range(n_steps):
            c = cb_ref[:, cols]
            for k in range(CONV_W):
                c = c + taps[t + k] * cw_ref[k:k + 1, cols]
            cs.append(c)
        for k in range(CONV_W - 1):
            conv_ref[k, :, cols] = taps[n_steps + k]
        c_all = jnp.concatenate(cs, axis=0)
        acc = acc + jnp.dot((_gelu_tanh(c_all) * val).astype(BF16), wd_ref[cols, :], preferred_element_type=F32)
    y_ref[...] = _rmsnorm(x1 + acc, nf_ref[...])


def _ffn_steps(x_t, omix_t, past_t, wts, *, n_steps, nb):
    return pl.pallas_call(
        functools.partial(_ffn_steps_kernel, n_steps=n_steps, nb=nb),
        out_shape=[jax.ShapeDtypeStruct((n_steps * nb, D_MODEL), F32), jax.ShapeDtypeStruct((CONV_W - 1, nb, D_FF), F32)],
        compiler_params=pltpu.CompilerParams(vmem_limit_bytes=VMEM_LIMIT),
        name="ffn_steps",
    )(x_t, omix_t, past_t, *wts)


def kernel(x_prompt, x_sample, cache_nsa_kv, cache_win_kv, state_hgrn, state_ffn_conv, page_table, norm1, w_in, hg_lb_logits, hg_norm, cmp_pe, cmp_w1, cmp_w2, w_out, norm2, w_gate, w_val, conv_w, conv_b, w_down, norm_f):
    B, T, _ = x_prompt.shape
    Bd, Td, _ = x_sample.shape
    depth = norm1.shape[0]
    assert depth == 1
    l = 0
    w_in_bf = jnp.pad(w_in[l], ((0, 0), (0, _C_GATE - D_IN))).astype(BF16)
    w1k, pek, w2k = _prep_cmp_weights(cmp_pe[l], cmp_w1[l], cmp_w2[l])
    ffn_w = (w_out[l].astype(BF16), norm2[l].reshape(1, D_MODEL), w_gate[l].astype(BF16), w_val[l].astype(BF16),
             conv_w[l], conv_b[l].reshape(1, D_FF), w_down[l].astype(BF16), norm_f.reshape(1, D_MODEL))

    z_hg, q_p, kv_p, win_p, gate_p, kv_t = _in_proj(x_prompt.reshape(B * T, D_MODEL), norm1[l], w_in_bf, seq_for_kv_t=T)
    o_hg_p, s_p = _hgrn(z_hg.reshape(B, T, _C_HG), hg_lb_logits, hg_norm[l],
                        jnp.zeros((B, HG_HEADS, HG_DK, HG_DK), F32), chunk=HG_CHUNK, rows_per_step=512, t_valid=None,
                        heads_per_step=HG_HEADS)
    kv_p = kv_p.reshape(B, T, COL_KV)
    win_p = win_p.reshape(B, T, COL_WIN)
    o_nsa_p = _nsa_prompt(kv_p, win_p, q_p.reshape(B, T, NSA_WIDTH), gate_p.reshape(B, T, LANES), w1k, pek, w2k,
                          tq=KEY_TILE)
    y_p, conv_p = _ffn(x_prompt, o_hg_p, o_nsa_p, jnp.zeros((B, CONV_W - 1, D_FF), F32), ffn_w, tm=512)

    z_hg, q_s, kv_s, win_s, gate_s = _in_proj(x_sample.reshape(Bd * Td, D_MODEL), norm1[l], w_in_bf)
    t_pad = HG_SUB
    z_pad = jnp.pad(z_hg.reshape(Bd, Td, _C_HG), ((0, 0), (0, t_pad - Td), (0, 0)))
    o_hg_s, s_s = _hgrn(z_pad, hg_lb_logits, hg_norm[l], state_hgrn[l], chunk=t_pad, rows_per_step=t_pad, t_valid=Td,
                        heads_per_step=HG_HEADS)
    o_hg_s = o_hg_s[:, :Td]
    kv_s = kv_s.reshape(Bd, Td, COL_KV)
    win_s = win_s.reshape(Bd, Td, COL_WIN)
    n_pool = cache_nsa_kv.shape[1]
    nwin = cache_win_kv.shape[2]
    pool_t = jnp.transpose(cache_nsa_kv[l], (0, 2, 3, 4, 1)).reshape(n_pool, 4, NSA_G * NSA_DK, PAGE_SIZE)
    win_t = jnp.transpose(cache_win_kv[l], (0, 2, 3, 4, 1)).reshape(Bd, 2, NSA_G * NSA_DK, nwin)
    o_nsa_s = _nsa_step(pool_t, page_table, kv_s, win_t, win_s, q_s.reshape(Bd, Td, NSA_WIDTH),
                        gate_s.reshape(Bd, Td, LANES), w1k, pek, w2k)
    omix_t = jnp.concatenate([o_hg_s, o_nsa_s], axis=-1).transpose(1, 0, 2).reshape(Td * Bd, D_MODEL)
    y_t, conv_t = _ffn_steps(x_sample.transpose(1, 0, 2).reshape(Td * Bd, D_MODEL), omix_t,
                             state_ffn_conv[l].transpose(1, 0, 2), ffn_w, n_steps=Td, nb=Bd)
    y_s = y_t.reshape(Td, Bd, D_MODEL).transpose(1, 0, 2)
    conv_s = conv_t.transpose(1, 0, 2)

    win_keep_p = min(WINDOW, T)
    win_new_t = win_s.reshape(Bd, Td, 2, NSA_G * NSA_DK).transpose(0, 2, 3, 1)
    win_all_t = jnp.concatenate([win_t[..., Td:], win_new_t], axis=-1)
    win_rows_s = win_all_t.reshape(1, Bd, 2, NSA_G, NSA_DK, nwin).transpose(0, 1, 5, 2, 3, 4)
    kv_rows_p = kv_t.reshape(1, B, 4, NSA_G, NSA_DK, T).transpose(0, 1, 5, 2, 3, 4)
    return (y_p, y_s,
            kv_rows_p, kv_s.reshape(1, Bd, Td, 4, NSA_G, NSA_DK),
            win_p[:, T - win_keep_p:].reshape(1, B, win_keep_p, 2, NSA_G, NSA_DK),
            win_rows_s,
            s_p[None], s_s[None], conv_p[None], conv_s[None])
```

```python
import functools

import numpy as np
import jax
import jax.numpy as jnp
from jax import lax
from jax.experimental import pallas as pl
from jax.experimental.pallas import tpu as pltpu

F32 = jnp.float32
BF16 = jnp.bfloat16

D_MODEL = 1024
PAGE_SIZE = 128
HG_WIDTH = 512
HG_HEADS = 4
HG_DK = 128
HG_CHUNK = 64
NSA_WIDTH = 512
NSA_HEADS = 8
NSA_DK = 64
NSA_G = 2
NSA_R = 4
CMP_STRIDE = 16
CMP_BLOCK = 32
CMP_HIDDEN = 256
SLC_BLOCK = 64
SLC_TOPK = 16
WINDOW = 512
D_FF = 2816
CONV_W = 3
EPS = 1e-6
COL_KV = 512
COL_WIN = 256
COL_GATE = 24

LANES = 128
SUBLANES = 8
HG_SUB = 16
HG_UNROLL = 8
HG_MASKED_EXP = -1e30
VMEM_LIMIT = 56 * 1024 * 1024

_C_HG = 4 * HG_WIDTH
_C_Q = _C_HG + NSA_WIDTH
_C_KV = _C_Q + COL_KV
_C_WIN = _C_KV + COL_WIN
_C_GATE = _C_WIN + LANES
D_IN = _C_WIN + COL_GATE


def _sigmoid(x):
    return 1.0 / (1.0 + jnp.exp(-x))


def _split3_bf16(x):
    hi = x.astype(BF16)
    r1 = x - hi.astype(F32)
    mid = r1.astype(BF16)
    lo = (r1 - mid.astype(F32)).astype(BF16)
    return hi, mid, lo


def _in_proj_kernel(x_ref, g_ref, w_ref, hg_ref, q_ref, kv_ref, win_ref, gate_ref, *kvt_ref):
    x = x_ref[...]
    ms = jnp.mean(x * x, axis=-1, keepdims=True)
    h = ((x * lax.rsqrt(ms + EPS)) * g_ref[...]).astype(BF16)
    hg_ref[...] = jnp.dot(h, w_ref[:, 0:_C_HG], preferred_element_type=F32)
    q_ref[...] = jnp.dot(h, w_ref[:, _C_HG:_C_Q], preferred_element_type=F32)
    kv = jnp.dot(h, w_ref[:, _C_Q:_C_KV], preferred_element_type=F32)
    kv_ref[...] = kv
    win_ref[...] = jnp.dot(h, w_ref[:, _C_KV:_C_WIN], preferred_element_type=F32)
    gate_ref[...] = jnp.dot(h, w_ref[:, _C_WIN:_C_GATE], preferred_element_type=F32)
    if kvt_ref:
        kvt_ref[0][0] = kv.T


def _in_proj(x2d, norm_w, w_in_bf, *, seq_for_kv_t=None):
    n = x2d.shape[0]
    tm = 512
    assert n % tm == 0
    widths = (_C_HG, NSA_WIDTH, COL_KV, COL_WIN, LANES)
    out_specs = [pl.BlockSpec((tm, w), lambda i: (i, 0)) for w in widths]
    out_shape = [jax.ShapeDtypeStruct((n, w), F32) for w in widths]
    if seq_for_kv_t is not None:
        tiles = seq_for_kv_t // tm
        assert seq_for_kv_t % tm == 0
        out_specs.append(pl.BlockSpec((1, COL_KV, tm), lambda i: (i // tiles, 0, i % tiles)))
        out_shape.append(jax.ShapeDtypeStruct((n // seq_for_kv_t, COL_KV, seq_for_kv_t), F32))
    return pl.pallas_call(
        _in_proj_kernel,
        grid=(n // tm,),
        in_specs=[
            pl.BlockSpec((tm, D_MODEL), lambda i: (i, 0)),
            pl.BlockSpec((1, D_MODEL), lambda i: (0, 0)),
            pl.BlockSpec((D_MODEL, _C_GATE), lambda i: (0, 0)),
        ],
        out_specs=out_specs,
        out_shape=out_shape,
        compiler_params=pltpu.CompilerParams(dimension_semantics=("arbitrary",), vmem_limit_bytes=VMEM_LIMIT),
        name="in_proj",
    )(x2d, norm_w.reshape(1, D_MODEL), w_in_bf)


def _hgrn_kernel(q_ref, f_ref, i_ref, g_ref, lb_ref, nw_ref, s0_ref, e_ref, o_ref, sfin_ref, st_ref,
                 *, chunk, rows_per_step, t_valid, layer):
    C = chunk
    nsb = C // HG_SUB
    ti = pl.program_id(2)

    hps = st_ref.shape[0]

    @pl.when(ti == 0)
    def _init():
        for hh in range(hps):
            st_ref[hh] = s0_ref[0, hh].T

    lg = lb_ref[...]
    le = jnp.exp(lg - jnp.max(lg, axis=0, keepdims=True))
    lb_all = jnp.sum(le[0:layer + 1, :], axis=0, keepdims=True) / jnp.sum(le, axis=0, keepdims=True)
    nw_all = nw_ref[...]
    row_i = lax.broadcasted_iota(jnp.int32, (C, C), 0)
    col_i = lax.broadcasted_iota(jnp.int32, (C, C), 1)
    tri_bf = jnp.where(col_i <= row_i, 1.0, 0.0).astype(BF16)
    diag_mask = ((row_i // HG_SUB) == (col_i // HG_SUB)) & (col_i <= row_i)
    off_mask = (col_i // HG_SUB) < (row_i // HG_SUB)
    trel = lax.broadcasted_iota(jnp.int32, (C, HG_DK), 0) % HG_SUB
    nt_dims = (((1,), (1,)), ((), ()))
    tn_dims = (((0,), (0,)), ((), ()))

    def do_chunk(hh, c):
        lanes = slice(hh * HG_DK, (hh + 1) * HG_DK)
        lb = lb_all[:, lanes]
        one_m_lb = 1.0 - lb
        nw = nw_all[:, lanes]
        r0 = pl.multiple_of(c * C, C)
        rows = pl.ds(r0, C)
        fr = f_ref[0, rows, lanes]
        qr = q_ref[0, rows, lanes]
        v = i_ref[0, rows, lanes]
        gr = g_ref[0, rows, lanes]
        f = lb + one_m_lb * _sigmoid(fr)
        kk = one_m_lb * _sigmoid(-fr)
        logf = jnp.log2(f)
        if t_valid is not None:
            tok = ti * rows_per_step + r0 + lax.broadcasted_iota(jnp.int32, (C, HG_DK), 0)
            ok = tok < t_valid
            logf = jnp.where(ok, logf, 0.0)
            kk = jnp.where(ok, kk, 0.0)
        q = qr * _sigmoid(qr)
        hi, mid, lo = _split3_bf16(logf)
        a = (jnp.dot(tri_bf, hi, preferred_element_type=F32)
             + jnp.dot(tri_bf, mid, preferred_element_type=F32)
             + jnp.dot(tri_bf, lo, preferred_element_type=F32))
        st = st_ref[hh]
        vb = v.astype(BF16)
        o = lax.dot_general((q * jnp.exp2(a)).astype(BF16), st.astype(BF16), nt_dims, preferred_element_type=F32)

        if nsb > 1:
            parts = [jnp.zeros((HG_SUB, C), F32)]
            for i in range(1, nsb):
                bi = a[HG_SUB * i - 1:HG_SUB * i, :]
                qi = q[HG_SUB * i:HG_SUB * (i + 1), :] * jnp.exp2(a[HG_SUB * i:HG_SUB * (i + 1), :] - bi)
                ki = kk * jnp.exp2(jnp.minimum(bi - a, 0.0))
                parts.append(lax.dot_general(qi.astype(BF16), ki.astype(BF16), nt_dims, preferred_element_type=F32))
            att = jnp.where(off_mask, jnp.concatenate(parts, axis=0), 0.0)
        else:
            att = jnp.zeros((C, C), F32)

        a3 = a.reshape(nsb, HG_SUB, HG_DK)
        k3 = kk.reshape(nsb, HG_SUB, HG_DK)
        zs = []
        for j in range(HG_SUB):
            aj = jnp.broadcast_to(a3[:, j:j + 1, :], (nsb, HG_SUB, HG_DK)).reshape(C, HG_DK)
            kj = jnp.broadcast_to(k3[:, j:j + 1, :], (nsb, HG_SUB, HG_DK)).reshape(C, HG_DK)
            expo = a - aj if j == 0 else jnp.where(trel >= j, a - aj, HG_MASKED_EXP)
            zs.append((q * kj * jnp.exp2(expo)).astype(BF16))
        zcat = jnp.concatenate(zs, axis=1)
        att_d = jnp.dot(zcat, e_ref[...], preferred_element_type=F32)
        att = att + jnp.where(diag_mask, att_d, 0.0)
        o = o + jnp.dot(att.astype(BF16), vb, preferred_element_type=F32)

        a_end = a[C - 1:C, :]
        kdec = kk * jnp.exp2(a_end - a)
        st_ref[hh] = st * jnp.exp2(a_end) + lax.dot_general(vb, kdec.astype(BF16), tn_dims, preferred_element_type=F32)

        on = (o * lax.rsqrt(jnp.mean(o * o, axis=-1, keepdims=True) + EPS)) * nw
        o_ref[0, rows, lanes] = on * (gr * _sigmoid(gr))

    n_chunks = rows_per_step // C
    if n_chunks == 1:
        for hh in range(hps):
            do_chunk(hh, 0)
    else:
        for hh in range(hps):
            def chunk_body(c, carry, hh=hh):
                do_chunk(hh, c)
                return carry
            lax.fori_loop(0, n_chunks, chunk_body, 0, unroll=min(n_chunks, HG_UNROLL))

    @pl.when(ti == pl.num_programs(2) - 1)
    def _fin():
        for hh in range(hps):
            sfin_ref[0, hh] = st_ref[hh].T


def _hgrn(z_hg, lb_logits, norm_w, s0, *, chunk, rows_per_step, t_valid, heads_per_step=1, layer=0):
    b, t, _ = z_hg.shape
    n_lb = lb_logits.shape[0]
    hps = heads_per_step
    assert t % rows_per_step == 0 and rows_per_step % chunk == 0 and chunk % HG_SUB == 0 and HG_HEADS % hps == 0
    nt = t // rows_per_step
    hblocks = HG_HEADS // hps
    width = hps * HG_DK
    sub = np.arange(chunk) % HG_SUB
    e_mat = jnp.asarray((np.repeat(np.arange(HG_SUB), HG_DK)[:, None] == sub[None, :]).astype(np.float32), BF16)

    def zspec(k):
        return pl.BlockSpec((1, rows_per_step, width), lambda bi, h, ti, k=k: (bi, ti, k * hblocks + h))

    kern = functools.partial(_hgrn_kernel, chunk=chunk, rows_per_step=rows_per_step, t_valid=t_valid, layer=layer)
    return pl.pallas_call(
        kern,
        grid=(b, hblocks, nt),
        in_specs=[
            zspec(0), zspec(1), zspec(2), zspec(3),
            pl.BlockSpec((n_lb, width), lambda bi, h, ti: (0, h)),
            pl.BlockSpec((1, width), lambda bi, h, ti: (0, h)),
            pl.BlockSpec((1, hps, HG_DK, HG_DK), lambda bi, h, ti: (bi, h, 0, 0)),
            pl.BlockSpec((HG_SUB * HG_DK, chunk), lambda bi, h, ti: (0, 0)),
        ],
        out_specs=[
            pl.BlockSpec((1, rows_per_step, width), lambda bi, h, ti: (bi, ti, h)),
            pl.BlockSpec((1, hps, HG_DK, HG_DK), lambda bi, h, ti: (bi, h, 0, 0)),
        ],
        out_shape=[jax.ShapeDtypeStruct((b, t, HG_WIDTH), F32), jax.ShapeDtypeStruct((b, HG_HEADS, HG_DK, HG_DK), F32)],
        scratch_shapes=[pltpu.VMEM((hps, HG_DK, HG_DK), F32)],
        compiler_params=pltpu.CompilerParams(dimension_semantics=("arbitrary", "arbitrary", "arbitrary"),
                                             vmem_limit_bytes=VMEM_LIMIT),
        name="hgrn2",
    )(z_hg, z_hg, z_hg, z_hg, lb_logits, norm_w.reshape(1, HG_WIDTH), s0, e_mat)


KEY_TILE = 256
CMP_PACK = 4
LOG2E = 1.4426950408889634
Q_SCALE = LOG2E * NSA_DK ** -0.5
NEG_INF = float("-inf")


def _compress_pair(load_xs, w1_ref, pe_ref, w2_ref, slot, nch):
    r = jnp.zeros((SUBLANES, 2 * CMP_HIDDEN), F32)
    for u in range(CMP_STRIDE // CMP_PACK):
        r = r + jnp.dot(pe_ref[slot, u], w1_ref[slot, u], preferred_element_type=F32)
    bias = r[0:1, 0:CMP_HIDDEN] + r[1:2, CMP_HIDDEN:2 * CMP_HIDDEN]
    outs = []
    for g in range(NSA_G):
        acc = jnp.zeros((nch, 2 * CMP_HIDDEN), F32)
        for u in range(CMP_STRIDE // CMP_PACK):
            xg = jnp.concatenate([load_xs(CMP_PACK * u + i)[:, NSA_DK * g:NSA_DK * (g + 1)] for i in range(CMP_PACK)],
                                 axis=1).astype(BF16)
            acc = acc + jnp.dot(xg, w1_ref[slot, u], preferred_element_type=F32)
        h = acc[:, 0:CMP_HIDDEN] + pltpu.roll(acc[:, CMP_HIDDEN:2 * CMP_HIDDEN], nch - 1, axis=0) + bias
        hs = (h * _sigmoid(h)).astype(BF16)
        outs.append(jnp.dot(hs, w2_ref[slot], preferred_element_type=F32))
    return jnp.concatenate(outs, axis=1)


def _rank_bias(score, n_slc):
    nj, w = score.shape
    jrow = lax.broadcasted_iota(jnp.int32, (SUBLANES, w), 0)
    slabs = [score[SUBLANES * v:SUBLANES * (v + 1), :] for v in range(nj // SUBLANES)]
    ranks = [jnp.zeros((SUBLANES, w), F32) for _ in slabs]
    for jp in range(n_slc):
        row = score[jp:jp + 1, :]
        for v, slab in enumerate(slabs):
            lo = SUBLANES * v
            if lo > jp:
                beats = row >= slab
            elif lo + SUBLANES - 1 <= jp:
                beats = row > slab
            else:
                beats = (row > slab) | ((row == slab) & (jrow > jp - lo))
            ranks[v] = ranks[v] + jnp.where(beats, 1.0, 0.0)
    rank = jnp.concatenate(ranks, axis=0)
    return jnp.where((rank < float(SLC_TOPK)) & (score > NEG_INF), 0.0, NEG_INF)


def _cmp_branch(g, qtz, qpos, tq, nq, kc_ref, vct_ref, ov_ref, n_blk):
    assert tq % LANES == 0
    nb_pad = kc_ref.shape[0]
    nj_pad = ov_ref.shape[0]
    s = jnp.dot(kc_ref[...], qtz, preferred_element_type=F32)
    n_idx = lax.broadcasted_iota(jnp.int32, (nb_pad, nq), 0)
    cmask = (n_idx * CMP_STRIDE + (CMP_BLOCK - 1) <= qpos) & (n_idx < n_blk)
    s = jnp.where(cmask, s, NEG_INF)
    m = jnp.max(s, axis=0, keepdims=True)
    m = jnp.where(m == NEG_INF, 0.0, m)
    e = jnp.exp2(s - m)
    d = jnp.sum(e, axis=0, keepdims=True)
    p = e / jnp.where(d > 0.0, d, 1.0)
    o_cmp = jnp.dot(vct_ref[NSA_DK * g:NSA_DK * (g + 1), :], p.astype(BF16), preferred_element_type=F32)

    ps = p[:, 0:tq]
    for r in range(1, NSA_R):
        ps = ps + p[:, r * tq:(r + 1) * tq]
    qpos_w = qpos[:, 0:tq]
    w = tq
    hi, mid, lo = _split3_bf16(ps)
    ov = ov_ref[...]
    imp = (jnp.dot(ov, hi, preferred_element_type=F32) + jnp.dot(ov, mid, preferred_element_type=F32)
           + jnp.dot(ov, lo, preferred_element_type=F32))
    j_idx = lax.broadcasted_iota(jnp.int32, (nj_pad, w), 0)
    cur = qpos_w // SLC_BLOCK
    forced = (j_idx == 0) | (j_idx == cur) | (j_idx == cur - 1)
    score = jnp.where(forced, jnp.inf, jnp.where(j_idx <= cur, imp, NEG_INF))
    return o_cmp, score


def _select_blocks(scores, n_slc, tq, nq, selb_ref):
    w = scores[0].shape[1]
    bias = _rank_bias(jnp.concatenate(scores, axis=1), n_slc)
    for g in range(NSA_G):
        selb_ref[g] = jnp.concatenate([bias[:, g * w:(g + 1) * w]] * (nq // w), axis=1)


def _block_rows(rows, nq):
    n = rows.shape[0]
    return jnp.broadcast_to(rows[:, None, :], (n, SLC_BLOCK, nq)).reshape(n * SLC_BLOCK, nq)


def _attn_step(carry, k, qtz, vts, biases):
    out = []
    for g in range(NSA_G):
        m, l, acc = carry[g]
        s = jnp.dot(k, qtz[g], preferred_element_type=F32) + biases[g]
        m_new = jnp.maximum(m, jnp.max(s, axis=0, keepdims=True))
        m_safe = jnp.where(m_new == NEG_INF, 0.0, m_new)
        alpha = jnp.exp2(m - m_safe)
        p = jnp.exp2(s - m_safe)
        l = l * alpha + jnp.sum(p, axis=0, keepdims=True)
        acc = acc * alpha + jnp.dot(vts[g], p.astype(BF16), preferred_element_type=F32)
        out.append((m_new, l, acc))
    return tuple(out)


def _attn_init(nq):
    return tuple((jnp.full((1, nq), NEG_INF, F32), jnp.zeros((1, nq), F32), jnp.zeros((NSA_DK, nq), F32))
                 for _ in range(NSA_G))


def _attn_finish(carry):
    return [acc / jnp.where(l > 0.0, l, 1.0) for (_, l, acc) in carry]


def _causal_bias(kpos, qpos):
    return jnp.where(kpos <= qpos, 0.0, NEG_INF)


def _window_bias(kpos, qpos):
    d = kpos - qpos
    return jnp.where((d <= 0) & (d > -WINDOW), 0.0, NEG_INF)


def _gated_sum(gate_raw, o_cmp, o_slc, o_win):
    gt = _sigmoid(gate_raw)
    return gt[0:1, :] * o_cmp + gt[1:2, :] * o_slc + gt[2:3, :] * o_win


def _nsa_prompt_kernel(ck_ref, cv_ref, sk_ref, sv_ref, wk_ref, wv_ref, q_ref, gate_ref, ov_ref, w1_ref, pe_ref, w2_ref,
                       o_ref, kc_ref, vct_ref, svt_ref, wvt_ref, selb_ref, *, seq, tq):
    nq = NSA_R * tq
    nch = seq // CMP_STRIDE
    n_tiles = seq // KEY_TILE
    n_slc = seq // SLC_BLOCK
    blocks_per_tile = KEY_TILE // SLC_BLOCK
    assert 2 * blocks_per_tile == SUBLANES
    qt = pl.program_id(1)

    @pl.when(qt == 0)
    def _prep():
        kc = _compress_pair(lambda s: ck_ref[0, pl.ds(s, nch, stride=CMP_STRIDE), :], w1_ref, pe_ref, w2_ref, 0, nch)
        vc = _compress_pair(lambda s: cv_ref[0, pl.ds(s, nch, stride=CMP_STRIDE), :], w1_ref, pe_ref, w2_ref, 1, nch)
        kc_ref[...] = kc.astype(BF16)
        vct_ref[...] = vc.T.astype(BF16)
        for kt in range(n_tiles):
            rows = slice(kt * KEY_TILE, (kt + 1) * KEY_TILE)
            svt_ref[kt] = sv_ref[0, rows, :].T.astype(BF16)
            wvt_ref[kt] = wv_ref[0, rows, :].T.astype(BF16)

    t0 = qt * tq
    qpos = t0 + lax.broadcasted_iota(jnp.int32, (1, nq), 1) % tq
    hi_tile = (t0 + tq - 1) // KEY_TILE + 1
    win_lo = jnp.maximum(t0 - (WINDOW - 1), 0) // KEY_TILE
    krow = lax.broadcasted_iota(jnp.int32, (KEY_TILE, nq), 0)

    gate_t = gate_ref[0].T
    zeros_half = jnp.zeros((NSA_DK, nq), F32)
    qtz, gates = [], []
    for g in range(NSA_G):
        qg_t = q_ref[0, :, g * NSA_R * NSA_DK:(g + 1) * NSA_R * NSA_DK].T
        q_t = jnp.concatenate([qg_t[r * NSA_DK:(r + 1) * NSA_DK, :] for r in range(NSA_R)], axis=1) * Q_SCALE
        qtz.append(jnp.concatenate([q_t, zeros_half] if g == 0 else [zeros_half, q_t], axis=0).astype(BF16))
        gates.append(jnp.concatenate(
            [jnp.concatenate([gate_t[k * NSA_HEADS + g * NSA_R + r:k * NSA_HEADS + g * NSA_R + r + 1, :]
                              for r in range(NSA_R)], axis=1) for k in range(3)], axis=0))

    o_cmp, scores = [], []
    for g in range(NSA_G):
        oc, sc = _cmp_branch(g, qtz[g], qpos, tq, nq, kc_ref, vct_ref, ov_ref, nch - 1)
        o_cmp.append(oc)
        scores.append(sc)
    _select_blocks(scores, n_slc, tq, nq, selb_ref)

    def slc_step(kt, carry, causal):
        rows = pl.ds(pl.multiple_of(kt * KEY_TILE, KEY_TILE), KEY_TILE)
        k = sk_ref[0, rows, :].astype(BF16)
        biases = []
        for g in range(NSA_G):
            rows8 = selb_ref[g, pl.ds(pl.multiple_of((kt // 2) * SUBLANES, SUBLANES), SUBLANES), :]
            rows4 = jnp.where(kt % 2 == 0, rows8[0:blocks_per_tile, :], rows8[blocks_per_tile:, :])
            bg = _block_rows(rows4, nq)
            biases.append(bg + _causal_bias(kt * KEY_TILE + krow, qpos) if causal else bg)
        vts = [svt_ref[kt, NSA_DK * g:NSA_DK * (g + 1), :] for g in range(NSA_G)]
        return _attn_step(carry, k, qtz, vts, biases)

    def win_step(kt, carry):
        rows = pl.ds(pl.multiple_of(kt * KEY_TILE, KEY_TILE), KEY_TILE)
        k = wk_ref[0, rows, :].astype(BF16)
        wb = _window_bias(kt * KEY_TILE + krow, qpos)
        vts = [wvt_ref[kt, NSA_DK * g:NSA_DK * (g + 1), :] for g in range(NSA_G)]
        return _attn_step(carry, k, qtz, vts, [wb, wb])

    carry = lax.fori_loop(0, hi_tile - 1, lambda kt, c: slc_step(kt, c, False), _attn_init(nq))
    o_slc = _attn_finish(slc_step(hi_tile - 1, carry, True))
    o_win = _attn_finish(lax.fori_loop(win_lo, hi_tile, win_step, _attn_init(nq)))

    for g in range(NSA_G):
        o_t = _gated_sum(gates[g], o_cmp[g], o_slc[g], o_win[g])
        stack = jnp.concatenate([o_t[:, r * tq:(r + 1) * tq] for r in range(NSA_R)], axis=0)
        o_ref[0, :, g * NSA_R * NSA_DK:(g + 1) * NSA_R * NSA_DK] = stack.T


def _overlap_t(nj_pad, nb_pad):
    n = np.arange(nb_pad)[None, :]
    j = np.arange(nj_pad)[:, None]
    lo = np.maximum(n * CMP_STRIDE, j * SLC_BLOCK)
    hi = np.minimum(n * CMP_STRIDE + CMP_BLOCK, (j + 1) * SLC_BLOCK)
    return jnp.asarray(np.maximum(hi - lo, 0).astype(np.float32) / CMP_STRIDE, BF16)


def _nsa_prompt(kv_new, win_new, q, gate, w1k, pek, w2k, *, tq):
    b, seq, _ = kv_new.shape
    assert tq % LANES == 0 and tq <= KEY_TILE and seq % KEY_TILE == 0
    nq = NSA_R * tq
    nch = seq // CMP_STRIDE
    n_slc = seq // SLC_BLOCK
    n_tiles = seq // KEY_TILE
    ov = _overlap_t(n_slc, nch)
    kern = functools.partial(_nsa_prompt_kernel, seq=seq, tq=tq)
    return pl.pallas_call(
        kern,
        grid=(b, seq // tq),
        in_specs=[
            pl.BlockSpec((1, seq, LANES), lambda bi, qi: (bi, 0, 0)),
            pl.BlockSpec((1, seq, LANES), lambda bi, qi: (bi, 0, 1)),
            pl.BlockSpec((1, seq, LANES), lambda bi, qi: (bi, 0, 2)),
            pl.BlockSpec((1, seq, LANES), lambda bi, qi: (bi, 0, 3)),
            pl.BlockSpec((1, seq, LANES), lambda bi, qi: (bi, 0, 0)),
            pl.BlockSpec((1, seq, LANES), lambda bi, qi: (bi, 0, 1)),
            pl.BlockSpec((1, tq, NSA_WIDTH), lambda bi, qi: (bi, qi, 0)),
            pl.BlockSpec((1, tq, LANES), lambda bi, qi: (bi, qi, 0)),
            pl.BlockSpec((n_slc, nch), lambda bi, qi: (0, 0)),
            pl.BlockSpec(w1k.shape, lambda bi, qi: (0, 0, 0, 0)),
            pl.BlockSpec(pek.shape, lambda bi, qi: (0, 0, 0, 0)),
            pl.BlockSpec(w2k.shape, lambda bi, qi: (0, 0, 0)),
        ],
        out_specs=pl.BlockSpec((1, tq, NSA_WIDTH), lambda bi, qi: (bi, qi, 0)),
        out_shape=jax.ShapeDtypeStruct((b, seq, NSA_WIDTH), F32),
        scratch_shapes=[
            pltpu.VMEM((nch, LANES), BF16),
            pltpu.VMEM((LANES, nch), BF16),
            pltpu.VMEM((n_tiles, LANES, KEY_TILE), BF16),
            pltpu.VMEM((n_tiles, LANES, KEY_TILE), BF16),
            pltpu.VMEM((NSA_G, n_slc, nq), F32),
        ],
        compiler_params=pltpu.CompilerParams(dimension_semantics=("arbitrary", "arbitrary"), vmem_limit_bytes=VMEM_LIMIT),
        name="nsa_prompt",
    )(kv_new, kv_new, kv_new, kv_new, win_new, win_new, q, gate, ov, w1k, pek, w2k)


def _prep_cmp_weights(cmp_pe, cmp_w1, cmp_w2):
    groups = CMP_STRIDE // CMP_PACK
    w1k = jnp.concatenate([cmp_w1[:, 0], cmp_w1[:, 1]], axis=-1)
    w1k = w1k.reshape(2, groups, CMP_PACK * NSA_DK, 2 * CMP_HIDDEN).astype(BF16)
    pek = cmp_pe.reshape(2, 2, groups, CMP_PACK * NSA_DK).transpose(0, 2, 1, 3)
    pek = jnp.pad(pek, ((0, 0), (0, 0), (0, SUBLANES - 2), (0, 0))).astype(BF16)
    return w1k, pek, cmp_w2.astype(BF16)


def _softmax_rows(s):
    m = jnp.max(s, axis=1, keepdims=True)
    m = jnp.where(m == NEG_INF, 0.0, m)
    e = jnp.exp2(s - m)
    return e, jnp.sum(e, axis=1, keepdims=True)


def _nsa_step_kernel(pt_ref, pool_ref, kvn_ref, wc_ref, wn_ref, qz_ref, gate_ref, ov_ref, w1_ref, pe_ref, w2_ref,
                     o_ref, stg, sem, ckb, cvb, skt_ref, svt_ref, wkt_ref, wvt_ref, tmp, kc_ref, vc_ref,
                     *, past, n_new):
    n_pages = past // PAGE_SIZE
    nch = (past + n_new) // CMP_STRIDE
    assert nch * CMP_STRIDE <= past
    n_slc = -(-(past + n_new) // SLC_BLOCK)
    s_pad = skt_ref.shape[1]
    w_pad = wkt_ref.shape[1]
    nwin = wc_ref.shape[3]
    njl = ov_ref.shape[1]
    rows = NSA_R * n_new
    b = pl.program_id(0)
    nb = pl.num_programs(0)

    def page_copy(bb, p):
        return pltpu.make_async_copy(pool_ref.at[pt_ref[bb * n_pages + p]], stg.at[p], sem.at[0])

    def start_fetch(bb):
        def body(p, carry):
            page_copy(bb, p).start()
            return carry
        lax.fori_loop(0, n_pages, body, 0)

    @pl.when(b == 0)
    def _first():
        start_fetch(0)

    def new_tile_t(rows_new):
        tmp[...] = jnp.zeros(tmp.shape, F32)
        tmp[0:n_new, :] = rows_new
        return tmp[...].T.astype(BF16)

    wkt_ref[:, 0:nwin] = wc_ref[0, 0].astype(BF16)
    wvt_ref[:, 0:nwin] = wc_ref[0, 1].astype(BF16)
    wkt_ref[:, nwin:] = new_tile_t(wn_ref[0, :, 0:LANES])
    wvt_ref[:, nwin:] = new_tile_t(wn_ref[0, :, LANES:2 * LANES])
    skt_ref[:, past:] = new_tile_t(kvn_ref[0, :, 2 * LANES:3 * LANES])
    svt_ref[:, past:] = new_tile_t(kvn_ref[0, :, 3 * LANES:4 * LANES])

    pltpu.make_async_copy(pool_ref.at[pl.ds(0, n_pages)], stg, sem.at[0]).wait()
    for p in range(n_pages):
        cols = slice(p * PAGE_SIZE, (p + 1) * PAGE_SIZE)
        ckb[cols, :] = stg[p, 0].astype(BF16).T.astype(F32)
        cvb[cols, :] = stg[p, 1].astype(BF16).T.astype(F32)
        skt_ref[:, cols] = stg[p, 2].astype(BF16)
        svt_ref[:, cols] = stg[p, 3].astype(BF16)

    @pl.when(b + 1 < nb)
    def _next():
        start_fetch(b + 1)

    kc_ref[...] = _compress_pair(lambda s: ckb[pl.ds(s, nch, stride=CMP_STRIDE), :], w1_ref, pe_ref, w2_ref, 0, nch).astype(BF16)
    vc_ref[...] = _compress_pair(lambda s: cvb[pl.ds(s, nch, stride=CMP_STRIDE), :], w1_ref, pe_ref, w2_ref, 1, nch).astype(BF16)

    nt_dims = (((1,), (1,)), ((), ()))
    qpos = past + lax.broadcasted_iota(jnp.int32, (rows, 1), 0) % n_new
    qz = [qz_ref[0, g] for g in range(NSA_G)]

    n_lane = lax.broadcasted_iota(jnp.int32, (1, nch), 1)
    cmask = (n_lane * CMP_STRIDE + (CMP_BLOCK - 1) <= qpos) & (n_lane < nch - 1)
    o_cmp, ps = [], []
    for g in range(NSA_G):
        s = lax.dot_general(qz[g], kc_ref[...], nt_dims, preferred_element_type=F32)
        e, d = _softmax_rows(jnp.where(cmask, s, NEG_INF))
        p = e / jnp.where(d > 0.0, d, 1.0)
        o_cmp.append(jnp.dot(p.astype(BF16), vc_ref[...], preferred_element_type=F32))
        pg = p[0:n_new, :]
        for r in range(1, NSA_R):
            pg = pg + p[r * n_new:(r + 1) * n_new, :]
        ps.append(pg)
    hi, mid, lo = _split3_bf16(jnp.concatenate(ps, axis=0))
    ov = ov_ref[...]
    imp = (jnp.dot(hi, ov, preferred_element_type=F32) + jnp.dot(mid, ov, preferred_element_type=F32)
           + jnp.dot(lo, ov, preferred_element_type=F32))
    gt_rows = NSA_G * n_new
    j_lane = lax.broadcasted_iota(jnp.int32, (gt_rows, njl), 1)
    cur = (past + lax.broadcasted_iota(jnp.int32, (gt_rows, 1), 0) % n_new) // SLC_BLOCK
    forced = (j_lane == 0) | (j_lane == cur) | (j_lane == cur - 1)
    score = jnp.where(forced, jnp.inf, jnp.where(j_lane <= cur, imp, NEG_INF))
    rank = jnp.zeros((gt_rows, njl), F32)
    for jp in range(n_slc):
        col = score[:, jp:jp + 1]
        beats = (col > score) | ((col == score) & (j_lane > jp))
        rank = rank + jnp.where(beats, 1.0, 0.0)
    sel_bias = jnp.where((rank < float(SLC_TOPK)) & (score > NEG_INF), 0.0, NEG_INF)

    lane = lax.broadcasted_iota(jnp.int32, (1, PAGE_SIZE), 1)
    first_block = lane < SLC_BLOCK
    assert PAGE_SIZE == 2 * SLC_BLOCK
    wpos = past - nwin + lax.broadcasted_iota(jnp.int32, (1, w_pad), 1) - qpos
    wmask = (wpos <= 0) & (wpos > -WINDOW)
    outs = []
    for g in range(NSA_G):
        sb = jnp.concatenate([sel_bias[g * n_new:(g + 1) * n_new, :]] * NSA_R, axis=0)
        s = jnp.dot(qz[g], skt_ref[...], preferred_element_type=F32)
        tiles = []
        for p in range(s_pad // PAGE_SIZE):
            t = s[:, p * PAGE_SIZE:(p + 1) * PAGE_SIZE] + jnp.where(first_block, sb[:, 2 * p:2 * p + 1], sb[:, 2 * p + 1:2 * p + 2])
            if (p + 1) * PAGE_SIZE > past:
                t = jnp.where(p * PAGE_SIZE + lane <= qpos, t, NEG_INF)
            tiles.append(t)
        e, d = _softmax_rows(jnp.concatenate(tiles, axis=1))
        o_slc = lax.dot_general(e.astype(BF16), svt_ref[...], nt_dims, preferred_element_type=F32) / jnp.where(d > 0.0, d, 1.0)
        s = jnp.dot(qz[g], wkt_ref[...], preferred_element_type=F32)
        e, d = _softmax_rows(jnp.where(wmask, s, NEG_INF))
        o_win = lax.dot_general(e.astype(BF16), wvt_ref[...], nt_dims, preferred_element_type=F32) / jnp.where(d > 0.0, d, 1.0)
        gt = _sigmoid(gate_ref[0, g])
        outs.append(gt[:, 0:1] * o_cmp[g] + gt[:, 1:2] * o_slc + gt[:, 2:3] * o_win)
    assert NSA_G == 2
    o_ref[0] = jnp.where(lax.broadcasted_iota(jnp.int32, (rows, LANES), 1) < NSA_DK, outs[0], outs[1])


def _nsa_step(pool, page_table, kv_new, win_cache_t, win_new, q, gate, w1k, pek, w2k):
    b, n_pages = page_table.shape
    n_new = kv_new.shape[1]
    nwin = win_cache_t.shape[3]
    past = n_pages * PAGE_SIZE
    rows = NSA_R * n_new
    nch = (past + n_new) // CMP_STRIDE
    s_pad = past + PAGE_SIZE
    w_pad = nwin + PAGE_SIZE
    njl = -(-(2 * s_pad // PAGE_SIZE) // LANES) * LANES
    assert n_new <= PAGE_SIZE and nwin % LANES == 0 and rows % 16 == 0
    ov = _overlap_t(njl, nch).T
    x = (q * Q_SCALE).reshape(b, n_new, NSA_G, NSA_R, NSA_DK).transpose(0, 2, 3, 1, 4).reshape(b, NSA_G, rows, NSA_DK)
    z = jnp.zeros_like(x[:, 0])
    qz = jnp.stack([jnp.concatenate([x[:, 0], z], axis=-1), jnp.concatenate([z, x[:, 1]], axis=-1)], axis=1).astype(BF16)
    gz = gate[..., :COL_GATE].reshape(b, n_new, 3, NSA_G, NSA_R).transpose(0, 3, 4, 1, 2).reshape(b, NSA_G, rows, 3)
    gz = jnp.pad(gz, ((0, 0), (0, 0), (0, 0), (0, LANES - 3)))
    kern = functools.partial(_nsa_step_kernel, past=past, n_new=n_new)
    grid_spec = pltpu.PrefetchScalarGridSpec(
        num_scalar_prefetch=1,
        grid=(b,),
        in_specs=[
            pl.BlockSpec(memory_space=pl.ANY),
            pl.BlockSpec((1, n_new, COL_KV), lambda bi, pt: (bi, 0, 0)),
            pl.BlockSpec((1, 2, LANES, nwin), lambda bi, pt: (bi, 0, 0, 0)),
            pl.BlockSpec((1, n_new, COL_WIN), lambda bi, pt: (bi, 0, 0)),
            pl.BlockSpec((1, NSA_G, rows, LANES), lambda bi, pt: (bi, 0, 0, 0)),
            pl.BlockSpec((1, NSA_G, rows, LANES), lambda bi, pt: (bi, 0, 0, 0)),
            pl.BlockSpec((nch, njl), lambda bi, pt: (0, 0)),
            pl.BlockSpec(w1k.shape, lambda bi, pt: (0, 0, 0, 0)),
            pl.BlockSpec(pek.shape, lambda bi, pt: (0, 0, 0, 0)),
            pl.BlockSpec(w2k.shape, lambda bi, pt: (0, 0, 0)),
        ],
        out_specs=pl.BlockSpec((1, rows, LANES), lambda bi, pt: (bi, 0, 0)),
        scratch_shapes=[
            pltpu.VMEM((n_pages, 4, LANES, PAGE_SIZE), F32),
            pltpu.SemaphoreType.DMA((1,)),
            pltpu.VMEM((past, LANES), F32),
            pltpu.VMEM((past, LANES), F32),
            pltpu.VMEM((LANES, s_pad), BF16),
            pltpu.VMEM((LANES, s_pad), BF16),
            pltpu.VMEM((LANES, w_pad), BF16),
            pltpu.VMEM((LANES, w_pad), BF16),
            pltpu.VMEM((PAGE_SIZE, LANES), F32),
            pltpu.VMEM((nch, LANES), BF16),
            pltpu.VMEM((nch, LANES), BF16),
        ],
    )
    o = pl.pallas_call(
        kern,
        grid_spec=grid_spec,
        out_shape=jax.ShapeDtypeStruct((b, rows, LANES), F32),
        compiler_params=pltpu.CompilerParams(dimension_semantics=("arbitrary",), vmem_limit_bytes=VMEM_LIMIT),
        name="nsa_step",
    )(page_table.reshape(-1), pool, kv_new, win_cache_t, win_new, qz, gz, ov, w1k, pek, w2k)
    return o.reshape(b, NSA_R, n_new, NSA_G, NSA_DK).transpose(0, 2, 3, 1, 4).reshape(b, n_new, NSA_WIDTH)


FF_TILE = 256


def _rmsnorm(x, w):
    return (x * lax.rsqrt(jnp.mean(x * x, axis=-1, keepdims=True) + EPS)) * w


def _gelu_tanh(c):
    return c * (0.5 * (1.0 + jnp.tanh(0.7978845608028654 * (c + 0.044715 * (c * c * c)))))


def _ffn_kernel(x_ref, ohg_ref, onsa_ref, past_ref, wout_ref, n2_ref, wg_ref, wv_ref, cw_ref, cb_ref, wd_ref, nf_ref,
                y_ref, conv_ref, carry_ref, act_ref, *, tm):
    ti = pl.program_id(1)

    @pl.when(ti == 0)
    def _init():
        carry_ref[0:CONV_W - 1, :] = past_ref[0]

    x1 = (x_ref[0]
          + jnp.dot(ohg_ref[0].astype(BF16), wout_ref[0:HG_WIDTH, :], preferred_element_type=F32)
          + jnp.dot(onsa_ref[0].astype(BF16), wout_ref[HG_WIDTH:, :], preferred_element_type=F32))
    h = _rmsnorm(x1, n2_ref[...]).astype(BF16)
    row = lax.broadcasted_iota(jnp.int32, (tm, FF_TILE), 0)
    for j in range(D_FF // FF_TILE):
        cols = slice(j * FF_TILE, (j + 1) * FF_TILE)
        u = jnp.dot(h, wg_ref[:, cols], preferred_element_type=F32)
        val = jnp.dot(h, wv_ref[:, cols], preferred_element_type=F32)
        c0 = carry_ref[0:1, cols]
        c1 = carry_ref[1:2, cols]
        up1 = jnp.where(row == 0, c1, pltpu.roll(u, 1, axis=0))
        up2 = jnp.where(row == 0, c0, jnp.where(row == 1, c1, pltpu.roll(u, 2, axis=0)))
        c = cb_ref[:, cols] + up2 * cw_ref[0:1, cols] + up1 * cw_ref[1:2, cols] + u * cw_ref[2:3, cols]
        tail = u[tm - (CONV_W - 1):tm, :]
        carry_ref[0:CONV_W - 1, cols] = tail
        conv_ref[0, :, cols] = tail
        act_ref[:, cols] = (_gelu_tanh(c) * val).astype(BF16)
    y = jnp.dot(act_ref[...], wd_ref[...], preferred_element_type=F32)
    y_ref[0] = _rmsnorm(x1 + y, nf_ref[...])


def _ffn(x, o_hg, o_nsa, conv_past, wts, *, tm):
    b, t, _ = x.shape
    assert t % tm == 0 and D_FF % FF_TILE == 0
    wout, n2, wg, wv, cw, cb, wd, nf = wts
    const2 = lambda bi, ti: (0, 0)
    one = pl.Buffered(1)
    return pl.pallas_call(
        functools.partial(_ffn_kernel, tm=tm),
        grid=(b, t // tm),
        in_specs=[
            pl.BlockSpec((1, tm, D_MODEL), lambda bi, ti: (bi, ti, 0)),
            pl.BlockSpec((1, tm, HG_WIDTH), lambda bi, ti: (bi, ti, 0)),
            pl.BlockSpec((1, tm, NSA_WIDTH), lambda bi, ti: (bi, ti, 0)),
            pl.BlockSpec((1, CONV_W - 1, D_FF), lambda bi, ti: (bi, 0, 0)),
            pl.BlockSpec(wout.shape, const2, pipeline_mode=one),
            pl.BlockSpec(n2.shape, const2),
            pl.BlockSpec(wg.shape, const2, pipeline_mode=one),
            pl.BlockSpec(wv.shape, const2, pipeline_mode=one),
            pl.BlockSpec(cw.shape, const2),
            pl.BlockSpec(cb.shape, const2),
            pl.BlockSpec(wd.shape, const2, pipeline_mode=one),
            pl.BlockSpec(nf.shape, const2),
        ],
        out_specs=[
            pl.BlockSpec((1, tm, D_MODEL), lambda bi, ti: (bi, ti, 0)),
            pl.BlockSpec((1, CONV_W - 1, D_FF), lambda bi, ti: (bi, 0, 0)),
        ],
        out_shape=[jax.ShapeDtypeStruct((b, t, D_MODEL), F32), jax.ShapeDtypeStruct((b, CONV_W - 1, D_FF), F32)],
        scratch_shapes=[pltpu.VMEM((SUBLANES, D_FF), F32), pltpu.VMEM((tm, D_FF), BF16)],
        compiler_params=pltpu.CompilerParams(dimension_semantics=("arbitrary", "arbitrary"), vmem_limit_bytes=VMEM_LIMIT),
        name="ffn",
    )(x, o_hg, o_nsa, conv_past, wout, n2, wg, wv, cw, cb, wd, nf)


def _ffn_steps_kernel(x_ref, om_ref, past_ref, wout_ref, n2_ref, wg_ref, wv_ref, cw_ref, cb_ref, wd_ref, nf_ref,
                      y_ref, conv_ref, *, n_steps, nb):
    x1 = x_ref[...] + jnp.dot(om_ref[...].astype(BF16), wout_ref[...], preferred_element_type=F32)
    h = _rmsnorm(x1, n2_ref[...]).astype(BF16)
    acc = jnp.zeros((n_steps * nb, D_MODEL), F32)
    for j in range(D_FF // FF_TILE):
        cols = slice(j * FF_TILE, (j + 1) * FF_TILE)
        u = jnp.dot(h, wg_ref[:, cols], preferred_element_type=F32)
        val = jnp.dot(h, wv_ref[:, cols], preferred_element_type=F32)
        taps = [past_ref[k, :, cols] for k in range(CONV_W - 1)] + [u[t * nb:(t + 1) * nb, :] for t in range(n_steps)]
        cs = []
        for t in range(n_steps):
            c = cb_ref[:, cols]
            for k in range(CONV_W):
                c = c + taps[t + k] * cw_ref[k:k + 1, cols]
            cs.append(c)
        for k in range(CONV_W - 1):
            conv_ref[k, :, cols] = taps[n_steps + k]
        c_all = jnp.concatenate(cs, axis=0)
        acc = acc + jnp.dot((_gelu_tanh(c_all) * val).astype(BF16), wd_ref[cols, :], preferred_element_type=F32)
    y_ref[...] = _rmsnorm(x1 + acc, nf_ref[...])


def _ffn_steps(x_t, omix_t, past_t, wts, *, n_steps, nb):
    return pl.pallas_call(
        functools.partial(_ffn_steps_kernel, n_steps=n_steps, nb=nb),
        out_shape=[jax.ShapeDtypeStruct((n_steps * nb, D_MODEL), F32), jax.ShapeDtypeStruct((CONV_W - 1, nb, D_FF), F32)],
        compiler_params=pltpu.CompilerParams(vmem_limit_bytes=VMEM_LIMIT),
        name="ffn_steps",
    )(x_t, omix_t, past_t, *wts)


def kernel(x_prompt, x_sample, cache_nsa_kv, cache_win_kv, state_hgrn, state_ffn_conv, page_table, norm1, w_in, hg_lb_logits, hg_norm, cmp_pe, cmp_w1, cmp_w2, w_out, norm2, w_gate, w_val, conv_w, conv_b, w_down, norm_f):
    B, T, _ = x_prompt.shape
    Bd, Td, _ = x_sample.shape
    depth = norm1.shape[0]
    assert depth == 1
    l = 0
    w_in_bf = jnp.pad(w_in[l], ((0, 0), (0, _C_GATE - D_IN))).astype(BF16)
    w1k, pek, w2k = _prep_cmp_weights(cmp_pe[l], cmp_w1[l], cmp_w2[l])
    ffn_w = (w_out[l].astype(BF16), norm2[l].reshape(1, D_MODEL), w_gate[l].astype(BF16), w_val[l].astype(BF16),
             conv_w[l], conv_b[l].reshape(1, D_FF), w_down[l].astype(BF16), norm_f.reshape(1, D_MODEL))

    z_hg, q_p, kv_p, win_p, gate_p, kv_t = _in_proj(x_prompt.reshape(B * T, D_MODEL), norm1[l], w_in_bf, seq_for_kv_t=T)
    o_hg_p, s_p = _hgrn(z_hg.reshape(B, T, _C_HG), hg_lb_logits, hg_norm[l],
                        jnp.zeros((B, HG_HEADS, HG_DK, HG_DK), F32), chunk=HG_CHUNK, rows_per_step=512, t_valid=None,
                        heads_per_step=HG_HEADS)
    kv_p = kv_p.reshape(B, T, COL_KV)
    win_p = win_p.reshape(B, T, COL_WIN)
    o_nsa_p = _nsa_prompt(kv_p, win_p, q_p.reshape(B, T, NSA_WIDTH), gate_p.reshape(B, T, LANES), w1k, pek, w2k,
                          tq=KEY_TILE)
    y_p, conv_p = _ffn(x_prompt, o_hg_p, o_nsa_p, jnp.zeros((B, CONV_W - 1, D_FF), F32), ffn_w, tm=512)

    z_hg, q_s, kv_s, win_s, gate_s = _in_proj(x_sample.reshape(Bd * Td, D_MODEL), norm1[l], w_in_bf)
    t_pad = HG_SUB
    z_pad = jnp.pad(z_hg.reshape(Bd, Td, _C_HG), ((0, 0), (0, t_pad - Td), (0, 0)))
    o_hg_s, s_s = _hgrn(z_pad, hg_lb_logits, hg_norm[l], state_hgrn[l], chunk=t_pad, rows_per_step=t_pad, t_valid=Td,
                        heads_per_step=HG_HEADS)
    o_hg_s = o_hg_s[:, :Td]
    kv_s = kv_s.reshape(Bd, Td, COL_KV)
    win_s = win_s.reshape(Bd, Td, COL_WIN)
    n_pool = cache_nsa_kv.shape[1]
    nwin = cache_win_kv.shape[2]
    pool_t = jnp.transpose(cache_nsa_kv[l], (0, 2, 3, 4, 1)).reshape(n_pool, 4, NSA_G * NSA_DK, PAGE_SIZE)
    win_t = jnp.transpose(cache_win_kv[l], (0, 2, 3, 4, 1)).reshape(Bd, 2, NSA_G * NSA_DK, nwin)
    o_nsa_s = _nsa_step(pool_t, page_table, kv_s, win_t, win_s, q_s.reshape(Bd, Td, NSA_WIDTH),
                        gate_s.reshape(Bd, Td, LANES), w1k, pek, w2k)
    omix_t = jnp.concatenate([o_hg_s, o_nsa_s], axis=-1).transpose(1, 0, 2).reshape(Td * Bd, D_MODEL)
    y_t, conv_t = _ffn_steps(x_sample.transpose(1, 0, 2).reshape(Td * Bd, D_MODEL), omix_t,
                             state_ffn_conv[l].transpose(1, 0, 2), ffn_w, n_steps=Td, nb=Bd)
    y_s = y_t.reshape(Td, Bd, D_MODEL).transpose(1, 0, 2)
    conv_s = conv_t.transpose(1, 0, 2)

    win_keep_p = min(WINDOW, T)
    win_new_t = win_s.reshape(Bd, Td, 2, NSA_G * NSA_DK).transpose(0, 2, 3, 1)
    win_all_t = jnp.concatenate([win_t[..., Td:], win_new_t], axis=-1)
    win_rows_s = win_all_t.reshape(1, Bd, 2, NSA_G, NSA_DK, nwin).transpose(0, 1, 5, 2, 3, 4)
    kv_rows_p = kv_t.reshape(1, B, 4, NSA_G, NSA_DK, T).transpose(0, 1, 5, 2, 3, 4)
    return (y_p, y_s,
            kv_rows_p, kv_s.reshape(1, Bd, Td, 4, NSA_G, NSA_DK),
            win_p[:, T - win_keep_p:].reshape(1, B, win_keep_p, 2, NSA_G, NSA_DK),
            win_rows_s,
            s_p[None], s_s[None], conv_p[None], conv_s[None])
```
